```python
import jax, jax.numpy as jnp
from jax import lax
import numpy as np

D_MODEL = 1024
BATCH = 2
SEQ = 16384
DEPTH = 2

GRID_W = 64
CTX_LEN = 256
POS_BASE = 10000.0
LN_EPS = 1e-6

SSM_HEADS = 24
SSM_HEAD_DIM = 64
D_INNER = SSM_HEADS * SSM_HEAD_DIM
SSM_GROUPS = 4
HEADS_PER_GROUP = SSM_HEADS // SSM_GROUPS
SSM_STATE = 128
CONV_W = 5
CONV_CH = D_INNER + 2 * SSM_GROUPS * SSM_STATE
CHUNK = 128

FOURIER_GROUPS = 4
FOURIER_GROUP_DIM = 256
FOURIER_W = FOURIER_GROUPS * FOURIER_GROUP_DIM
N_BRANCH = 2

OFF_XBC = D_INNER
OFF_DT = OFF_XBC + CONV_CH
OFF_FOUR = OFF_DT + 2 * SSM_HEADS
OFF_GATE = OFF_FOUR + FOURIER_W
N_IN = OFF_GATE + N_BRANCH * D_MODEL

N_EXPERTS = 64
TOP_K = 8
N_EXPERT_GROUPS = 8
TOPK_GROUPS = 4
EXPERT_FF = 256
SHARED_FF = 256
ROUTE_SCALE = 2.5
MOE_BLOCK = 128

DEEPNORM_ALPHA = (2 * DEPTH) ** 0.25
DEEPNORM_BETA = (8 * DEPTH) ** -0.25

kernel_name = 'hybrid_ssd_fourier_moe_diffusion_block'


def _layer_norm(t):
    tf = t.astype(jnp.float32)
    mu = jnp.mean(tf, -1, keepdims=True)
    var = jnp.mean(jnp.square(tf - mu), -1, keepdims=True)
    return ((tf - mu) * lax.rsqrt(var + LN_EPS)).astype(t.dtype)


def _post_norm(t, g, b):
    return _layer_norm(t) * g + b


def _gated_rms_norm(y, z, w):
    h = (y * jax.nn.silu(z)).astype(jnp.float32)
    h = h * lax.rsqrt(jnp.mean(jnp.square(h), -1, keepdims=True) + LN_EPS)
    return h.astype(z.dtype) * w


def _grid_pos(rows):
    quarter = D_MODEL // 4
    omega = 1.0 / (POS_BASE ** (jnp.arange(quarter, dtype=jnp.float32) / quarter))
    ang_r = jnp.arange(rows, dtype=jnp.float32)[:, None] * omega
    ang_c = jnp.arange(GRID_W, dtype=jnp.float32)[:, None] * omega
    emb_r = jnp.concatenate([jnp.sin(ang_r), jnp.cos(ang_r)], -1)
    emb_c = jnp.concatenate([jnp.sin(ang_c), jnp.cos(ang_c)], -1)
    half = D_MODEL // 2
    emb = jnp.concatenate([jnp.broadcast_to(emb_r[:, None], (rows, GRID_W, half)),
                           jnp.broadcast_to(emb_c[None], (rows, GRID_W, half))], -1)
    return emb.reshape(rows * GRID_W, D_MODEL)


def _dwconv(t, w, b):
    out = lax.conv_general_dilated(t, w[:, None, :], window_strides=(1,),
                                   padding=[(CONV_W // 2, CONV_W // 2)],
                                   dimension_numbers=('NWC', 'WIO', 'NWC'),
                                   feature_group_count=t.shape[-1])
    return out + b


def _ssd(xh, dt, a, bm, cm, s0):
    b_, l_ = xh.shape[:2]
    nc = l_ // CHUNK
    chunk = lambda t: t.reshape((b_, nc, CHUNK) + t.shape[2:])
    adt = chunk(dt * a)
    xdt = chunk(xh.astype(jnp.float32) * dt[..., None])
    bc, cc = chunk(bm), chunk(cm)
    acum = jnp.cumsum(adt, axis=2)
    seg = acum[:, :, :, None] - acum[:, :, None, :]
    tri = jnp.tril(jnp.ones((CHUNK, CHUNK), bool))
    lmat = jnp.exp(jnp.where(tri[:, :, None, None], seg, -jnp.inf))
    scores = jnp.einsum('bclgn,bcsgn->bclsg', cc, bc)
    y_diag = jnp.einsum('bclsg,bclsgr,bcsgrp->bclgrp', scores, lmat, xdt)
    decay_states = jnp.exp(acum[:, :, -1:] - acum)
    states = jnp.einsum('bcsgn,bcsgr,bcsgrp->bcgrpn', bc, decay_states, xdt)
    chunk_decay = jnp.exp(acum[:, :, -1])

    def step(s, inp):
        dcy, st = inp
        return s * dcy[..., None, None] + st, s

    s_fin, s_prev = lax.scan(step, s0, (jnp.moveaxis(chunk_decay, 1, 0), jnp.moveaxis(states, 1, 0)))
    s_prev = jnp.moveaxis(s_prev, 0, 1)
    y_off = jnp.einsum('bclgn,bcgrpn,bclgr->bclgrp', cc, s_prev, jnp.exp(acum))
    y = (y_diag + y_off).reshape(xh.shape)
    return y.astype(xh.dtype), s_fin


def _bidir_ssd(xh, dt, a, bm, cm, s0_f, s0_b):
    y_f, s_f = _ssd(xh, dt[:, :, 0], a[0], bm, cm, s0_f)
    flip = lambda t: jnp.flip(t, axis=1)
    y_b, s_b = _ssd(flip(xh), flip(dt[:, :, 1]), a[1], flip(bm), flip(cm), s0_b)
    return y_f + flip(y_b), s_f, s_b


def _ssm_inputs(proj, conv_w, conv_b, dt_bias):
    b_, l_ = proj.shape[:2]
    z = proj[..., :OFF_XBC]
    xbc = jax.nn.silu(_dwconv(proj[..., OFF_XBC:OFF_DT], conv_w, conv_b))
    dt_raw = proj[..., OFF_DT:OFF_FOUR].reshape(b_, l_, 2, SSM_GROUPS, HEADS_PER_GROUP)
    dt = jax.nn.softplus((dt_raw + dt_bias.reshape(2, SSM_GROUPS, HEADS_PER_GROUP)).astype(jnp.float32))
    xs = xbc[..., :D_INNER].reshape(b_, l_, SSM_GROUPS, HEADS_PER_GROUP, SSM_HEAD_DIM)
    gn = SSM_GROUPS * SSM_STATE
    bm = xbc[..., D_INNER:D_INNER + gn].reshape(b_, l_, SSM_GROUPS, SSM_STATE)
    cm = xbc[..., D_INNER + gn:].reshape(b_, l_, SSM_GROUPS, SSM_STATE)
    return z, xs, dt, bm, cm


def _fourier_mix(t):
    b_, l_ = t.shape[:2]
    tg = t.reshape(b_, l_, FOURIER_GROUPS, FOURIER_GROUP_DIM).astype(jnp.float32)
    f = jnp.fft.fft2(tg, axes=(1, 3), norm='ortho').real
    return f.reshape(b_, l_, FOURIER_W).astype(t.dtype)


def _token_mixer(u, uc, w_in, conv_w, conv_b, dt_bias, a_log, d_skip, norm_w,
                 w_br_ssm, w_br_four, w_out, with_ctx):
    a = -jnp.exp(a_log.astype(jnp.float32)).reshape(2, SSM_GROUPS, HEADS_PER_GROUP)
    d = d_skip.reshape(SSM_GROUPS, HEADS_PER_GROUP, 1)
    pc = uc @ w_in
    zc, xsc, dtc, bmc, cmc = _ssm_inputs(pc, conv_w, conv_b, dt_bias)
    s0 = jnp.zeros((uc.shape[0], SSM_GROUPS, HEADS_PER_GROUP, SSM_HEAD_DIM, SSM_STATE), jnp.float32)
    yc, s_f, s_b = _bidir_ssd(xsc, dtc, a, bmc, cmc, s0, s0)
    p = u @ w_in
    z, xs, dt, bm, cm = _ssm_inputs(p, conv_w, conv_b, dt_bias)
    y, _, _ = _bidir_ssd(xs, dt, a, bm, cm, s_f, s_b)

    def merge(proj, z_, x_, y_):
        b_, l_ = proj.shape[:2]
        y_ = (y_ + x_ * d).reshape(b_, l_, D_INNER)
        ssm = _gated_rms_norm(y_, z_, norm_w) @ w_br_ssm
        four = _fourier_mix(proj[..., OFF_FOUR:OFF_GATE]) @ w_br_four
        gates = jax.nn.sigmoid(proj[..., OFF_GATE:].astype(jnp.float32)).astype(proj.dtype)
        merged = gates[..., :D_MODEL] * ssm + gates[..., D_MODEL:] * four
        return merged @ w_out

    out = merge(p, z, xs, y)
    out_c = merge(pc, zc, xsc, yc) if with_ctx else None
    return out, out_c


def _moe(h, router_w, router_bias, w1, w3, w2, ws1, ws3, ws2):
    n_tok = h.shape[0]
    scores = jax.nn.sigmoid((h @ router_w).astype(jnp.float32))
    sel = scores + router_bias.astype(jnp.float32)
    per_group = N_EXPERTS // N_EXPERT_GROUPS
    grp_score = lax.top_k(sel.reshape(n_tok, N_EXPERT_GROUPS, per_group), 2)[0].sum(-1)
    top_grp = lax.top_k(grp_score, TOPK_GROUPS)[1]
    grp_mask = jnp.any(top_grp[:, :, None] == jnp.arange(N_EXPERT_GROUPS)[None, None, :], axis=1)
    sel = jnp.where(jnp.repeat(grp_mask, per_group, axis=1), sel, -jnp.inf)
    top_e = lax.top_k(sel, TOP_K)[1]
    gate = jnp.take_along_axis(scores, top_e, axis=1)
    gate = gate / jnp.sum(gate, -1, keepdims=True) * ROUTE_SCALE
    n_asg = n_tok * TOP_K
    n_blk = -(-(n_asg + N_EXPERTS * (MOE_BLOCK - 1)) // MOE_BLOCK)
    flat_e = top_e.reshape(-1)
    flat_tok = jnp.repeat(jnp.arange(n_tok, dtype=jnp.int32), TOP_K)
    order = jnp.argsort(flat_e)
    se = flat_e[order]
    counts = jnp.bincount(flat_e, length=N_EXPERTS)
    starts = jnp.cumsum(counts) - counts
    padded = (counts + MOE_BLOCK - 1) // MOE_BLOCK * MOE_BLOCK
    pends = jnp.cumsum(padded)
    dest = (pends - padded)[se] + jnp.arange(n_asg) - starts[se]
    buf_tok = jnp.full((n_blk * MOE_BLOCK,), n_tok, jnp.int32).at[dest].set(flat_tok[order])
    buf_gate = jnp.zeros((n_blk * MOE_BLOCK,), h.dtype).at[dest].set(gate.reshape(-1)[order].astype(h.dtype))
    blk_e = jnp.minimum(jnp.searchsorted(pends, jnp.arange(n_blk) * MOE_BLOCK, side='right'), N_EXPERTS - 1)
    h_pad = jnp.concatenate([h, jnp.zeros((1, h.shape[1]), h.dtype)], 0)

    def body(acc, blk):
        tok, g, e = blk
        xb = h_pad[tok]
        hb = jax.nn.silu(xb @ w1[e]) * (xb @ w3[e])
        return acc.at[tok].add((hb @ w2[e]) * g[:, None]), None

    routed, _ = lax.scan(body, jnp.zeros_like(h_pad),
                         (buf_tok.reshape(n_blk, MOE_BLOCK), buf_gate.reshape(n_blk, MOE_BLOCK), blk_e))
    shared = (jax.nn.silu(h @ ws1) * (h @ ws3)) @ ws2
    return routed[:n_tok] + shared


def setup_inputs(seed: int = 0) -> dict:
    key = jax.random.key(seed)
    ks = jax.random.split(key, 32)
    f32 = jnp.float32
    nrm = lambda k, shape, s: jax.random.normal(k, shape, f32) * s
    dt0 = jnp.exp(jax.random.uniform(ks[8], (DEPTH, 2, SSM_HEADS), f32, np.log(1e-3), np.log(1e-1)))
    return {
        'x': nrm(ks[0], (BATCH, SEQ, D_MODEL), 1.0),
        'c': nrm(ks[1], (BATCH, D_MODEL), 1.0),
        'ctx': nrm(ks[2], (BATCH, CTX_LEN, D_MODEL), 1.0),
        'c_ctx': nrm(ks[3], (D_MODEL,), 1.0),
        'w_ada': nrm(ks[4], (DEPTH, D_MODEL, 6 * D_MODEL), 0.5 * D_MODEL ** -0.5),
        'b_ada': nrm(ks[5], (DEPTH, 6 * D_MODEL), 0.01),
        'w_in': nrm(ks[6], (DEPTH, D_MODEL, N_IN), D_MODEL ** -0.5),
        'conv_w': nrm(ks[7], (DEPTH, CONV_W, CONV_CH), CONV_W ** -0.5),
        'conv_b': nrm(ks[9], (DEPTH, CONV_CH), 0.01),
        'dt_bias': dt0 + jnp.log(-jnp.expm1(-dt0)),
        'a_log': jnp.log(jax.random.uniform(ks[10], (DEPTH, 2, SSM_HEADS), f32, 1.0, 16.0)),
        'd_skip': 1.0 + nrm(ks[11], (DEPTH, SSM_HEADS), 0.1),
        'ssm_norm_w': 1.0 + nrm(ks[12], (DEPTH, D_INNER), 0.1),
        'w_br_ssm': nrm(ks[13], (DEPTH, D_INNER, D_MODEL), DEEPNORM_BETA * D_INNER ** -0.5),
        'w_br_four': nrm(ks[14], (DEPTH, FOURIER_W, D_MODEL), DEEPNORM_BETA * FOURIER_W ** -0.5),
        'w_out': nrm(ks[15], (DEPTH, D_MODEL, D_MODEL), DEEPNORM_BETA * D_MODEL ** -0.5),
        'ln1_g': 1.0 + nrm(ks[16], (DEPTH, D_MODEL), 0.05),
        'ln1_b': nrm(ks[17], (DEPTH, D_MODEL), 0.01),
        'ln2_g': 1.0 + nrm(ks[18], (DEPTH, D_MODEL), 0.05),
        'ln2_b': nrm(ks[19], (DEPTH, D_MODEL), 0.01),
        'router_w': nrm(ks[20], (DEPTH, D_MODEL, N_EXPERTS), D_MODEL ** -0.5),
        'router_bias': nrm(ks[21], (DEPTH, N_EXPERTS), 0.01),
        'w1': nrm(ks[22], (DEPTH, N_EXPERTS, D_MODEL, EXPERT_FF), D_MODEL ** -0.5),
        'w3': nrm(ks[23], (DEPTH, N_EXPERTS, D_MODEL, EXPERT_FF), D_MODEL ** -0.5),
        'w2': nrm(ks[24], (DEPTH, N_EXPERTS, EXPERT_FF, D_MODEL), DEEPNORM_BETA * EXPERT_FF ** -0.5),
        'ws1': nrm(ks[25], (DEPTH, D_MODEL, SHARED_FF), D_MODEL ** -0.5),
        'ws3': nrm(ks[26], (DEPTH, D_MODEL, SHARED_FF), D_MODEL ** -0.5),
        'ws2': nrm(ks[27], (DEPTH, SHARED_FF, D_MODEL), DEEPNORM_BETA * SHARED_FF ** -0.5),
    }


def reference(x, c, ctx, c_ctx, w_ada, b_ada, w_in, conv_w, conv_b, dt_bias, a_log, d_skip,
              ssm_norm_w, w_br_ssm, w_br_four, w_out, ln1_g, ln1_b, ln2_g, ln2_b,
              router_w, router_bias, w1, w3, w2, ws1, ws3, ws2):
    rows = x.shape[1] // GRID_W
    x = x + _grid_pos(rows).astype(x.dtype)[None]
    xc = ctx
    silu_c = jax.nn.silu(c)
    silu_cc = jax.nn.silu(c_ctx)
    for i in range(DEPTH):
        last = i == DEPTH - 1
        mod = (silu_c @ w_ada[i] + b_ada[i])[:, None, :]
        mod_c = silu_cc @ w_ada[i] + b_ada[i]
        sh1, sc1, g1, sh2, sc2, g2 = jnp.split(mod, 6, axis=-1)
        sh1c, sc1c, g1c, sh2c, sc2c, g2c = jnp.split(mod_c, 6, axis=-1)
        u = _layer_norm(x) * (1.0 + sc1) + sh1
        uc = _layer_norm(xc) * (1.0 + sc1c) + sh1c
        f, fc = _token_mixer(u, uc, w_in[i], conv_w[i], conv_b[i], dt_bias[i], a_log[i], d_skip[i],
                             ssm_norm_w[i], w_br_ssm[i], w_br_four[i], w_out[i], not last)
        x = _post_norm(DEEPNORM_ALPHA * x + g1 * f, ln1_g[i], ln1_b[i])
        u2 = _layer_norm(x) * (1.0 + sc2) + sh2
        n_lat = u2.shape[0] * u2.shape[1]
        if last:
            m = _moe(u2.reshape(n_lat, D_MODEL), router_w[i], router_bias[i], w1[i], w3[i], w2[i],
                     ws1[i], ws3[i], ws2[i]).reshape(x.shape)
        else:
            xc = _post_norm(DEEPNORM_ALPHA * xc + g1c * fc, ln1_g[i], ln1_b[i])
            u2c = _layer_norm(xc) * (1.0 + sc2c) + sh2c
            tokens = jnp.concatenate([u2.reshape(n_lat, D_MODEL), u2c.reshape(-1, D_MODEL)], 0)
            mm = _moe(tokens, router_w[i], router_bias[i], w1[i], w3[i], w2[i], ws1[i], ws3[i], ws2[i])
            m = mm[:n_lat].reshape(x.shape)
            xc = _post_norm(DEEPNORM_ALPHA * xc + g2c * mm[n_lat:].reshape(xc.shape), ln2_g[i], ln2_b[i])
        x = _post_norm(DEEPNORM_ALPHA * x + g2 * m, ln2_g[i], ln2_b[i])
    return x
```

```python
import functools

import jax
import jax.numpy as jnp
import numpy as np
from jax import lax
from jax.experimental import pallas as pl
from jax.experimental.pallas import tpu as pltpu

D_MODEL = 1024
GRID_W = 64
POS_BASE = 10000.0
LN_EPS = 1e-6

SSM_HEADS = 24
SSM_HEAD_DIM = 64
D_INNER = SSM_HEADS * SSM_HEAD_DIM
SSM_GROUPS = 4
HEADS_PER_GROUP = SSM_HEADS // SSM_GROUPS
SSM_STATE = 128
CONV_W = 5
CONV_CH = D_INNER + 2 * SSM_GROUPS * SSM_STATE
CHUNK = 128

FOURIER_GROUPS = 4
FOURIER_GROUP_DIM = 256
FOURIER_W = FOURIER_GROUPS * FOURIER_GROUP_DIM

OFF_XBC = D_INNER
OFF_DT = OFF_XBC + CONV_CH
OFF_FOUR = OFF_DT + 2 * SSM_HEADS
OFF_GATE = OFF_FOUR + FOURIER_W

N_EXPERTS = 64
TOP_K = 8
N_EXPERT_GROUPS = 8
EXPERTS_PER_GROUP = N_EXPERTS // N_EXPERT_GROUPS
TOPK_GROUPS = 4
EXPERT_FF = 256
SHARED_FF = 256
ROUTE_SCALE = 2.5

LANES = 128
VMEM_LIMIT_BYTES = 56 * 1024 * 1024

P_XBC = 0
P_DT = CONV_CH
P_FOUR = P_DT + SSM_GROUPS * LANES
P_GATE = P_FOUR + FOURIER_W
P_Z = P_GATE + 2 * D_MODEL
P_WIDTH = P_Z + D_INNER
DT_BWD_LANE = 8

K1_TM = 512
K1_TN = 1536
ROW_TILE = 256
ROUTER_TM = 512
MOE_BLK = 256


def _cparams(n_axes=1):
    return pltpu.CompilerParams(
        dimension_semantics=("arbitrary",) * n_axes,
        vmem_limit_bytes=VMEM_LIMIT_BYTES)


def _layer_norm_f32(x):
    mu = jnp.mean(x, axis=-1, keepdims=True)
    xc = x - mu
    var = jnp.mean(xc * xc, axis=-1, keepdims=True)
    return xc * lax.rsqrt(var + LN_EPS)


def _sigmoid(x):
    return 1.0 / (1.0 + jnp.exp(-x))


def _silu(x):
    return x * _sigmoid(x)


def _pos_kernel(x_ref, ctx_ref, er_ref, ec_ref, o_ref, *, n_lat):
    i = pl.program_id(0)
    half = D_MODEL // 2

    @pl.when(i < n_lat)
    def _():
        ec = ec_ref[...]
        for r in range(8):
            rows = slice(GRID_W * r, GRID_W * (r + 1))
            o_ref[rows, :half] = x_ref[rows, :half] + er_ref[r:r + 1, :]
            o_ref[rows, half:] = x_ref[rows, half:] + ec

    @pl.when(i >= n_lat)
    def _():
        o_ref[...] = ctx_ref[...]


def _assemble_stream(x, ctx):
    b, l, d = x.shape
    lc = ctx.shape[1]
    tile = 8 * GRID_W
    n_lat = (b * l) // tile
    n_ctx = (b * lc) // tile
    rows = l // GRID_W
    quarter = D_MODEL // 4
    omega = 1.0 / (POS_BASE ** (jnp.arange(quarter, dtype=jnp.float32) / quarter))
    ang_r = jnp.arange(rows, dtype=jnp.float32)[:, None] * omega
    ang_c = jnp.arange(GRID_W, dtype=jnp.float32)[:, None] * omega
    emb_r = jnp.concatenate([jnp.sin(ang_r), jnp.cos(ang_r)], -1)
    emb_c = jnp.concatenate([jnp.sin(ang_c), jnp.cos(ang_c)], -1)
    tiles_per_seq = l // tile
    return pl.pallas_call(
        functools.partial(_pos_kernel, n_lat=n_lat),
        out_shape=jax.ShapeDtypeStruct((b * l + b * lc, d), jnp.float32),
        grid=(n_lat + n_ctx,),
        in_specs=[
            pl.BlockSpec((tile, d), lambda i: (jnp.minimum(i, n_lat - 1), 0)),
            pl.BlockSpec((tile, d), lambda i: (jnp.maximum(i - n_lat, 0), 0)),
            pl.BlockSpec((8, d // 2), lambda i: (i % tiles_per_seq, 0)),
            pl.BlockSpec((GRID_W, d // 2), lambda i: (0, 0)),
        ],
        out_specs=pl.BlockSpec((tile, d), lambda i: (i, 0)),
        compiler_params=_cparams(1),
        name="assemble_stream",
    )(x.reshape(b * l, d), ctx.reshape(b * lc, d), emb_r, emb_c)


def _ada_kernel(c_ref, w_ref, b_ref, o_ref):
    c = c_ref[...]
    o_ref[...] = jnp.dot(_silu(c), w_ref[...], preferred_element_type=jnp.float32,
                         precision=lax.Precision.HIGHEST) + b_ref[...]


def _ada_mod(c_rows, w_ada, b_ada):
    depth, d, n6 = w_ada.shape
    tn = 1536
    return pl.pallas_call(
        _ada_kernel,
        out_shape=jax.ShapeDtypeStruct((depth, 8, n6), jnp.float32),
        grid=(depth, n6 // tn),
        in_specs=[
            pl.BlockSpec((8, d), lambda a, j: (0, 0)),
            pl.BlockSpec((None, d, tn), lambda a, j: (a, 0, j)),
            pl.BlockSpec((None, 1, tn), lambda a, j: (a, 0, j)),
        ],
        out_specs=pl.BlockSpec((None, 8, tn), lambda a, j: (a, 0, j)),
        compiler_params=_cparams(2),
        name="ada_mod",
    )(c_rows, w_ada, b_ada.reshape(depth, 1, n6))


def _k1_kernel(x_ref, sc_ref, sh_ref, w_ref, o_ref, u_ref, *, tiles_per_seq, n_batch):
    i = pl.program_id(0)

    @pl.when(pl.program_id(1) == 0)
    def _():
        bidx = jnp.minimum(i // tiles_per_seq, n_batch)
        xn = _layer_norm_f32(x_ref[...])
        u = xn * (1.0 + sc_ref[pl.ds(bidx, 1), :]) + sh_ref[pl.ds(bidx, 1), :]
        u_ref[...] = u.astype(jnp.bfloat16)

    o_ref[...] = jnp.dot(u_ref[...], w_ref[...],
                         preferred_element_type=jnp.float32).astype(jnp.bfloat16)


def _in_proj(x_all, mod, w_all, seq_len, n_batch):
    nt, d = x_all.shape
    tm, tn = K1_TM, K1_TN
    return pl.pallas_call(
        functools.partial(_k1_kernel, tiles_per_seq=seq_len // tm, n_batch=n_batch),
        out_shape=jax.ShapeDtypeStruct((nt, P_WIDTH), jnp.bfloat16),
        grid=(nt // tm, P_WIDTH // tn),
        in_specs=[
            pl.BlockSpec((tm, d), lambda i, j: (i, 0)),
            pl.BlockSpec((8, d), lambda i, j: (0, 1)),
            pl.BlockSpec((8, d), lambda i, j: (0, 0)),
            pl.BlockSpec((d, tn), lambda i, j: (0, j)),
        ],
        out_specs=pl.BlockSpec((tm, tn), lambda i, j: (i, j)),
        scratch_shapes=[pltpu.VMEM((tm, d), jnp.bfloat16)],
        compiler_params=_cparams(2),
        name="in_proj",
    )(x_all, mod, mod, w_all)


def _conv_kernel(cur_ref, prev_ref, next_ref, w_ref, b_ref, o_ref, *, tiles_per_seq, n_lat):
    i = pl.program_id(0)
    is_ctx = i >= n_lat
    is_start = jnp.logical_or(i % tiles_per_seq == 0, is_ctx)
    is_end = jnp.logical_or(i % tiles_per_seq == tiles_per_seq - 1, is_ctx)
    halo = prev_ref.shape[0]
    rows = cur_ref.shape[0]
    prev = jnp.where(is_start, 0.0, prev_ref[...].astype(jnp.float32))
    nxt = jnp.where(is_end, 0.0, next_ref[...].astype(jnp.float32))
    ext = jnp.concatenate([prev, cur_ref[...].astype(jnp.float32), nxt], axis=0)
    acc = b_ref[...] + w_ref[0:1, :] * ext[halo - 2:halo - 2 + rows, :]
    for k in range(1, CONV_W):
        acc = acc + w_ref[k:k + 1, :] * ext[halo - 2 + k:halo - 2 + k + rows, :]
    o_ref[...] = _silu(acc).astype(jnp.bfloat16)


def _conv(p_all, conv_w, conv_b, seq_len, n_lat_rows, ctx_len):
    nt = p_all.shape[0]
    tl = ROW_TILE
    assert ctx_len == tl, "context sequences must span exactly one conv tile"
    halo = 16
    hb = tl // halo
    n_halo_blocks = nt // halo
    w8 = jnp.zeros((8, CONV_CH), jnp.float32).at[:CONV_W].set(conv_w)
    return pl.pallas_call(
        functools.partial(_conv_kernel, tiles_per_seq=seq_len // tl, n_lat=n_lat_rows // tl),
        out_shape=jax.ShapeDtypeStruct((nt, CONV_CH), jnp.bfloat16),
        grid=(nt // tl,),
        in_specs=[
            pl.BlockSpec((tl, CONV_CH), lambda i: (i, 0)),
            pl.BlockSpec((halo, CONV_CH), lambda i: (jnp.maximum(i * hb - 1, 0), 0)),
            pl.BlockSpec((halo, CONV_CH),
                         lambda i: (jnp.minimum((i + 1) * hb, n_halo_blocks - 1), 0)),
            pl.BlockSpec((8, CONV_CH), lambda i: (0, 0)),
            pl.BlockSpec((1, CONV_CH), lambda i: (0, 0)),
        ],
        out_specs=pl.BlockSpec((tl, CONV_CH), lambda i: (i, 0)),
        compiler_params=_cparams(1),
        name="dwconv_silu",
    )(p_all, p_all, p_all, w8, conv_b.reshape(1, CONV_CH))


def _softplus(x):
    return jnp.maximum(x, 0.0) + jnp.log1p(jnp.exp(-jnp.abs(x)))


def _ssd_direction(x_ref, b_ref, c_ref, dt_ref, bias, a_row, s_ref, y_ref, *, lane0, forward):
    q = CHUNK
    row_i = lax.broadcasted_iota(jnp.int32, (q, q), 0)
    col_i = lax.broadcasted_iota(jnp.int32, (q, q), 1)
    tri = (row_i >= col_i) if forward else (row_i <= col_i)
    lane_lo = lax.broadcasted_iota(jnp.int32, (q, LANES), 1) < SSM_HEAD_DIM
    lane_lo_row = lax.broadcasted_iota(jnp.int32, (1, LANES), 1) < SSM_HEAD_DIM

    dt = _softplus(dt_ref[...].astype(jnp.float32) + bias)
    adt = dt * a_row
    cum = jnp.dot(tri.astype(jnp.float32), adt, preferred_element_type=jnp.float32,
                  precision=lax.Precision.HIGHEST)
    cum_t = cum.T
    total = cum[q - 1:q, :] if forward else cum[0:1, :]

    cm = c_ref[...]
    bm = b_ref[...]
    scores = lax.dot_general(cm, bm, (((1,), (1,)), ((), ())),
                             preferred_element_type=jnp.float32)
    bm_t = bm.astype(jnp.float32).T.astype(jnp.bfloat16)

    for pr in range(HEADS_PER_GROUP // 2):
        r0 = lane0 + 2 * pr
        r1 = r0 + 1
        lanes = slice(LANES * pr, LANES * (pr + 1))
        xp = x_ref[:, lanes].astype(jnp.float32)
        dt_pair = jnp.where(lane_lo, dt[:, r0:r0 + 1], dt[:, r1:r1 + 1])
        cum_pair = jnp.where(lane_lo, cum[:, r0:r0 + 1], cum[:, r1:r1 + 1])
        tot_pair = jnp.where(lane_lo_row, total[:, r0:r0 + 1], total[:, r1:r1 + 1])
        xdt = xp * dt_pair
        l0 = jnp.exp(jnp.where(tri, cum[:, r0:r0 + 1] - cum_t[r0:r0 + 1, :], -jnp.inf))
        l1 = jnp.exp(jnp.where(tri, cum[:, r1:r1 + 1] - cum_t[r1:r1 + 1, :], -jnp.inf))
        w = jnp.concatenate([(scores * l0).astype(jnp.bfloat16),
                             (scores * l1).astype(jnp.bfloat16)], axis=1)
        xdt_b = xdt.astype(jnp.bfloat16)
        zero = jnp.zeros_like(xdt_b)
        rhs = jnp.concatenate([jnp.where(lane_lo, xdt_b, zero),
                               jnp.where(lane_lo, zero, xdt_b)], axis=0)
        y_diag = jnp.dot(w, rhs, preferred_element_type=jnp.float32)
        s_old = s_ref[:, lanes]
        y_off = jnp.dot(cm, s_old.astype(jnp.bfloat16),
                        preferred_element_type=jnp.float32) * jnp.exp(cum_pair)
        y_ref[:, lanes] = (y_diag + y_off).astype(jnp.bfloat16)
        decayed = (xdt * jnp.exp(tot_pair - cum_pair)).astype(jnp.bfloat16)
        s_ref[:, lanes] = jnp.exp(tot_pair) * s_old + jnp.dot(
            bm_t, decayed, preferred_element_type=jnp.float32)


def _ssd_kernel(xf_ref, bf_ref, cf_ref, dtf_ref, xb_ref, bb_ref, cb_ref, dtb_ref,
                bias_ref, a_ref, yf_ref, yb_ref, sf_ref, sb_ref):
    @pl.when(pl.program_id(2) == 0)
    def _():
        sf_ref[...] = jnp.zeros_like(sf_ref)
        sb_ref[...] = jnp.zeros_like(sb_ref)

    bias = bias_ref[...]
    a_row = a_ref[...]
    _ssd_direction(xf_ref, bf_ref, cf_ref, dtf_ref, bias, a_row, sf_ref, yf_ref,
                   lane0=0, forward=True)
    _ssd_direction(xb_ref, bb_ref, cb_ref, dtb_ref, bias, a_row, sb_ref, yb_ref,
                   lane0=DT_BWD_LANE, forward=False)


def _ssd(xbc, p_all, dt_bias_rows, a_rows, n_batch, seq_len, ctx_len):
    nt = xbc.shape[0]
    q = CHUNK
    nc_lat = seq_len // q
    nc_ctx = ctx_len // q
    ctx_blk0 = (n_batch * seq_len) // q
    n_steps = nc_ctx + nc_lat
    gw = HEADS_PER_GROUP * SSM_HEAD_DIM
    b_blk0 = D_INNER // SSM_STATE
    c_blk0 = b_blk0 + SSM_GROUPS
    dt_blk0 = P_DT // LANES

    def fwd_row(b, j):
        return jnp.where(j < nc_ctx, ctx_blk0 + nc_ctx * b + j, nc_lat * b + (j - nc_ctx))

    def bwd_row(b, j):
        return jnp.where(j < nc_ctx, ctx_blk0 + nc_ctx * b + (nc_ctx - 1 - j),
                         nc_lat * b + (n_steps - 1 - j))

    def specs(row):
        return [
            pl.BlockSpec((q, gw), lambda b, g, j: (row(b, j), g)),
            pl.BlockSpec((q, SSM_STATE), lambda b, g, j: (row(b, j), b_blk0 + g)),
            pl.BlockSpec((q, SSM_STATE), lambda b, g, j: (row(b, j), c_blk0 + g)),
            pl.BlockSpec((q, LANES), lambda b, g, j: (row(b, j), dt_blk0 + g)),
        ]

    vec_spec = pl.BlockSpec((None, 1, LANES), lambda b, g, j: (g, 0, 0))
    out_sds = jax.ShapeDtypeStruct((nt, D_INNER), jnp.bfloat16)
    return pl.pallas_call(
        _ssd_kernel,
        out_shape=(out_sds, out_sds),
        grid=(n_batch, SSM_GROUPS, n_steps),
        in_specs=specs(fwd_row) + specs(bwd_row) + [vec_spec, vec_spec],
        out_specs=(pl.BlockSpec((q, gw), lambda b, g, j: (fwd_row(b, j), g)),
                   pl.BlockSpec((q, gw), lambda b, g, j: (bwd_row(b, j), g))),
        scratch_shapes=[pltpu.VMEM((SSM_STATE, gw), jnp.float32),
                        pltpu.VMEM((SSM_STATE, gw), jnp.float32)],
        compiler_params=_cparams(3),
        name="ssd_scan",
    )(xbc, xbc, xbc, p_all, xbc, xbc, xbc, p_all, dt_bias_rows, a_rows)


def _dft_tables(seq_len):
    l1n = seq_len // LANES
    two_pi = 2.0 * np.pi
    gd = FOURIER_GROUP_DIM
    jj = jnp.arange(gd, dtype=jnp.int32)
    ang_c = ((jj[:, None] * jj[None, :]) % gd).astype(jnp.float32) * (two_pi / gd)
    cc, sc = jnp.cos(ang_c), jnp.sin(ang_c)
    k1 = jnp.arange(l1n, dtype=jnp.int32)
    ang1 = ((k1[:, None] * k1[None, :]) % l1n).astype(jnp.float32) * (two_pi / l1n)
    w1 = jnp.concatenate([jnp.cos(ang1), -jnp.sin(ang1)], axis=0)
    k2 = jnp.arange(LANES, dtype=jnp.int32)
    kk = k1[:, None, None] + l1n * k2[None, :, None]
    ang2 = ((kk * k2[None, None, :]) % seq_len).astype(jnp.float32) * (two_pi / seq_len)
    er, ei = jnp.cos(ang2), -jnp.sin(ang2)
    e = jnp.concatenate([jnp.concatenate([er, -ei], axis=2),
                         jnp.concatenate([ei, er], axis=2)], axis=1)
    return cc, sc, w1, e


def _dft1_kernel(w_ref, x_ref, o_ref):
    o_ref[...] = jnp.dot(w_ref[...], x_ref[...],
                         preferred_element_type=jnp.float32).astype(jnp.bfloat16)


def _channel_mix(gr, gi, cs_ref, o_ref):
    gd = FOURIER_GROUP_DIM
    for g in range(FOURIER_GROUPS):
        cols = slice(gd * g, gd * (g + 1))
        lhs = jnp.concatenate([gr[:, cols], gi[:, cols]], axis=1).astype(jnp.bfloat16)
        o_ref[:, cols] = jnp.dot(lhs, cs_ref[...],
                                 preferred_element_type=jnp.float32).astype(jnp.bfloat16)


def _dft2_kernel(e_ref, ar_ref, ai_ref, cs_ref, o_ref):
    a = jnp.concatenate([ar_ref[...], ai_ref[...]], axis=0)
    g = jnp.dot(e_ref[...], a, preferred_element_type=jnp.float32)
    half = g.shape[0] // 2
    _channel_mix(g[:half], g[half:], cs_ref, o_ref)


def _dft_ctx_kernel(w_ref, x_ref, cs_ref, o_ref):
    g = jnp.dot(w_ref[...], x_ref[...], preferred_element_type=jnp.float32)
    half = g.shape[0] // 2
    _channel_mix(g[:half], g[half:], cs_ref, o_ref)


def _fourier_latent(p_all, n_batch, seq_len, tables):
    cc, sc, w1, e = tables
    c = FOURIER_W
    l1n = seq_len // LANES
    ncol = LANES * c
    four = p_all[:n_batch * seq_len, P_FOUR:P_FOUR + c].reshape(n_batch, l1n, ncol)
    tn = 4096
    a = pl.pallas_call(
        _dft1_kernel,
        out_shape=jax.ShapeDtypeStruct((n_batch, 2 * l1n, ncol), jnp.bfloat16),
        grid=(n_batch, ncol // tn),
        in_specs=[pl.BlockSpec((2 * l1n, l1n), lambda b, j: (0, 0)),
                  pl.BlockSpec((None, l1n, tn), lambda b, j: (b, 0, j))],
        out_specs=pl.BlockSpec((None, 2 * l1n, tn), lambda b, j: (b, 0, j)),
        compiler_params=_cparams(2),
        name="dft_stage1",
    )(w1.astype(jnp.bfloat16), four)
    a4 = a.reshape(n_batch, 2 * l1n, LANES, c)
    norm = 1.0 / np.sqrt(float(seq_len) * FOURIER_GROUP_DIM)
    cs = (jnp.concatenate([cc, sc], axis=0) * norm).astype(jnp.bfloat16)
    o = pl.pallas_call(
        _dft2_kernel,
        out_shape=jax.ShapeDtypeStruct((n_batch, l1n, LANES, c), jnp.bfloat16),
        grid=(n_batch, l1n),
        in_specs=[pl.BlockSpec((None, 2 * LANES, 2 * LANES), lambda b, k: (k, 0, 0)),
                  pl.BlockSpec((None, None, LANES, c), lambda b, k: (b, k, 0, 0)),
                  pl.BlockSpec((None, None, LANES, c), lambda b, k: (b, l1n + k, 0, 0)),
                  pl.BlockSpec((2 * FOURIER_GROUP_DIM, FOURIER_GROUP_DIM), lambda b, k: (0, 0))],
        out_specs=pl.BlockSpec((None, None, LANES, c), lambda b, k: (b, k, 0, 0)),
        compiler_params=_cparams(2),
        name="dft_stage2",
    )(e.astype(jnp.bfloat16), a4, a4, cs)
    return o.transpose(0, 2, 1, 3).reshape(n_batch * seq_len, c)


def _fourier_ctx(p_all, n_batch, seq_len, ctx_len, tables):
    cc, sc, _, _ = tables
    assert ctx_len == FOURIER_GROUP_DIM
    c = FOURIER_W
    wc = jnp.concatenate([cc, -sc], axis=0).astype(jnp.bfloat16)
    norm = 1.0 / np.sqrt(float(ctx_len) * FOURIER_GROUP_DIM)
    cs = (jnp.concatenate([cc, sc], axis=0) * norm).astype(jnp.bfloat16)
    blk0 = (n_batch * seq_len) // ctx_len
    return pl.pallas_call(
        _dft_ctx_kernel,
        out_shape=jax.ShapeDtypeStruct((n_batch * ctx_len, c), jnp.bfloat16),
        grid=(n_batch,),
        in_specs=[pl.BlockSpec((2 * ctx_len, ctx_len), lambda b: (0, 0)),
                  pl.BlockSpec((ctx_len, c), lambda b: (blk0 + b, P_FOUR // c)),
                  pl.BlockSpec((2 * FOURIER_GROUP_DIM, FOURIER_GROUP_DIM), lambda b: (0, 0))],
        out_specs=pl.BlockSpec((ctx_len, c), lambda b: (b, 0)),
        compiler_params=_cparams(1),
        name="dft_ctx",
    )(wc, p_all, cs)


def _merge_kernel(gate_ref, z_ref, xs_ref, yf_ref, yb_ref, f_ref, x_ref,
                  g1_ref, sc2_ref, sh2_ref, d_ref, nw_ref, lng_ref, lnb_ref,
                  wssm_ref, wfour_ref, wout_ref, xo_ref, u2_ref,
                  *, tiles_per_seq, n_batch, alpha):
    bidx = jnp.minimum(pl.program_id(0) // tiles_per_seq, n_batch)
    y = (yf_ref[...].astype(jnp.float32) + yb_ref[...].astype(jnp.float32)
         + xs_ref[...].astype(jnp.float32) * d_ref[...])
    h = y * _silu(z_ref[...].astype(jnp.float32))
    h = h * lax.rsqrt(jnp.mean(h * h, axis=-1, keepdims=True) + LN_EPS) * nw_ref[...]
    ssm = jnp.dot(h.astype(jnp.bfloat16), wssm_ref[...], preferred_element_type=jnp.float32)
    four = jnp.dot(f_ref[...], wfour_ref[...], preferred_element_type=jnp.float32)
    gates = _sigmoid(gate_ref[...].astype(jnp.float32))
    merged = gates[:, :D_MODEL] * ssm + gates[:, D_MODEL:] * four
    out = jnp.dot(merged.astype(jnp.bfloat16), wout_ref[...], preferred_element_type=jnp.float32)
    r = alpha * x_ref[...] + g1_ref[pl.ds(bidx, 1), :] * out
    xn = _layer_norm_f32(r) * lng_ref[...] + lnb_ref[...]
    xo_ref[...] = xn
    u2_ref[...] = (_layer_norm_f32(xn) * (1.0 + sc2_ref[pl.ds(bidx, 1), :])
                   + sh2_ref[pl.ds(bidx, 1), :])


def _merge(p_all, xbc, yf, yb, f_all, x_all, mod, d_cols, norm_w, ln_g, ln_b,
           w_ssm, w_four, w_out, n_rows, seq_len, n_batch, alpha):
    tm = ROW_TILE
    d = D_MODEL
    row = lambda i: (i, 0)
    const = lambda i: (0, 0)
    out_sds = jax.ShapeDtypeStruct((n_rows, d), jnp.float32)
    return pl.pallas_call(
        functools.partial(_merge_kernel, tiles_per_seq=seq_len // tm, n_batch=n_batch, alpha=alpha),
        out_shape=(out_sds, out_sds),
        grid=(n_rows // tm,),
        in_specs=[
            pl.BlockSpec((tm, 2 * d), lambda i: (i, P_GATE // (2 * d))),
            pl.BlockSpec((tm, D_INNER), lambda i: (i, P_Z // D_INNER)),
            pl.BlockSpec((tm, D_INNER), row),
            pl.BlockSpec((tm, D_INNER), row),
            pl.BlockSpec((tm, D_INNER), row),
            pl.BlockSpec((tm, FOURIER_W), row),
            pl.BlockSpec((tm, d), row),
            pl.BlockSpec((8, d), lambda i: (0, 2)),
            pl.BlockSpec((8, d), lambda i: (0, 4)),
            pl.BlockSpec((8, d), lambda i: (0, 3)),
            pl.BlockSpec((1, D_INNER), const),
            pl.BlockSpec((1, D_INNER), const),
            pl.BlockSpec((1, d), const),
            pl.BlockSpec((1, d), const),
            pl.BlockSpec((D_INNER, d), const),
            pl.BlockSpec((FOURIER_W, d), const),
            pl.BlockSpec((d, d), const),
        ],
        out_specs=(pl.BlockSpec((tm, d), row), pl.BlockSpec((tm, d), row)),
        compiler_params=_cparams(1),
        name="merge_postnorm",
    )(p_all, p_all, xbc, yf, yb, f_all, x_all, mod, mod, mod, d_cols, norm_w,
      ln_g, ln_b, w_ssm, w_four, w_out)


def _first_index_of_max(v, iota, big):
    m = jnp.max(v, axis=0, keepdims=True)
    idx = jnp.min(jnp.where(v == m, iota, big), axis=0, keepdims=True)
    return m, idx


def _router_kernel(u_ref, wt_ref, bias_ref, tri_ref, e_ref, g_ref, p_ref, cnt_ref, base_ref):
    @pl.when(pl.program_id(0) == 0)
    def _():
        base_ref[...] = jnp.zeros_like(base_ref)

    tm = u_ref.shape[0]
    ne, epg = N_EXPERTS, EXPERTS_PER_GROUP
    neg = -jnp.inf
    logits = lax.dot_general(wt_ref[...], u_ref[...], (((1,), (1,)), ((), ())),
                             preferred_element_type=jnp.float32,
                             precision=lax.Precision.HIGHEST)
    scores = _sigmoid(logits)
    sel = scores + bias_ref[...]
    iota_g = lax.broadcasted_iota(jnp.int32, (epg, tm), 0).astype(jnp.float32)
    grp_rows = []
    for g in range(N_EXPERT_GROUPS):
        v = sel[epg * g:epg * (g + 1), :]
        m1, i1 = _first_index_of_max(v, iota_g, epg)
        m2 = jnp.max(jnp.where(iota_g == i1, neg, v), axis=0, keepdims=True)
        grp_rows.append(m1 + m2)
    grp = jnp.concatenate(grp_rows, axis=0)
    iota_n = lax.broadcasted_iota(jnp.int32, (N_EXPERT_GROUPS, tm), 0).astype(jnp.float32)
    chosen = jnp.zeros((N_EXPERT_GROUPS, tm), jnp.float32)
    for _ in range(TOPK_GROUPS):
        _, gi = _first_index_of_max(grp, iota_n, N_EXPERT_GROUPS)
        hit = iota_n == gi
        chosen = jnp.where(hit, 1.0, chosen)
        grp = jnp.where(hit, neg, grp)
    masked = jnp.concatenate(
        [jnp.where(chosen[g:g + 1, :] > 0.0, sel[epg * g:epg * (g + 1), :], neg)
         for g in range(N_EXPERT_GROUPS)], axis=0)
    iota_e = lax.broadcasted_iota(jnp.int32, (ne, tm), 0).astype(jnp.float32)
    picked = jnp.zeros((ne, tm), jnp.float32)
    e_rows, g_rows = [], []
    for _ in range(TOP_K):
        _, ei = _first_index_of_max(masked, iota_e, ne)
        hit = iota_e == ei
        e_rows.append(ei)
        g_rows.append(jnp.sum(jnp.where(hit, scores, 0.0), axis=0, keepdims=True))
        picked = jnp.where(hit, 1.0, picked)
        masked = jnp.where(hit, neg, masked)
    top_e = jnp.concatenate(e_rows, axis=0).astype(jnp.int32)
    gate = jnp.concatenate(g_rows, axis=0)
    gate = gate / jnp.sum(gate, axis=0, keepdims=True) * ROUTE_SCALE
    before = jnp.dot(picked.astype(jnp.bfloat16), tri_ref[...],
                     preferred_element_type=jnp.float32)
    rank = before + base_ref[...]
    p_rows = [jnp.sum(jnp.where(iota_e == e_rows[k], rank, 0.0), axis=0, keepdims=True)
              for k in range(TOP_K)]
    e_ref[...] = top_e
    g_ref[...] = gate
    p_ref[...] = jnp.concatenate(p_rows, axis=0).astype(jnp.int32)
    base_ref[...] = base_ref[...] + jnp.sum(picked, axis=1, keepdims=True)
    cnt_ref[...] = base_ref[...]


def _router(u2, router_w, router_bias, n_tok):
    tm = ROUTER_TM
    tri = jnp.triu(jnp.ones((tm, tm), jnp.float32), k=1).astype(jnp.bfloat16)
    kt = lambda i: (0, i)
    return pl.pallas_call(
        _router_kernel,
        out_shape=(jax.ShapeDtypeStruct((TOP_K, n_tok), jnp.int32),
                   jax.ShapeDtypeStruct((TOP_K, n_tok), jnp.float32),
                   jax.ShapeDtypeStruct((TOP_K, n_tok), jnp.int32),
                   jax.ShapeDtypeStruct((N_EXPERTS, 1), jnp.float32)),
        grid=(n_tok // tm,),
        in_specs=[pl.BlockSpec((tm, D_MODEL), lambda i: (i, 0)),
                  pl.BlockSpec((N_EXPERTS, D_MODEL), lambda i: (0, 0)),
                  pl.BlockSpec((N_EXPERTS, 1), lambda i: (0, 0)),
                  pl.BlockSpec((tm, tm), lambda i: (0, 0))],
        out_specs=(pl.BlockSpec((TOP_K, tm), kt), pl.BlockSpec((TOP_K, tm), kt),
                   pl.BlockSpec((TOP_K, tm), kt),
                   pl.BlockSpec((N_EXPERTS, 1), lambda i: (0, 0))),
        scratch_shapes=[pltpu.VMEM((N_EXPERTS, 1), jnp.float32)],
        compiler_params=_cparams(1),
        name="moe_router",
    )(u2, router_w.T, router_bias.reshape(N_EXPERTS, 1), tri)


def _gather_rows(idx_vmem_ref, idx_smem, table_hbm, dst_ref, n_rows, row_of, sem_idx, sem_rows):
    cp = pltpu.make_async_copy(idx_vmem_ref.at[0], idx_smem, sem_idx)
    cp.start()
    cp.wait()

    def row_copy(r):
        return pltpu.make_async_copy(table_hbm.at[pl.ds(row_of(r), 1), :],
                                     dst_ref.at[pl.ds(r, 1), :], sem_rows)

    def start(r, carry):
        row_copy(r).start()
        return carry

    def wait(r, carry):
        row_copy(r).wait()
        return carry

    lax.fori_loop(0, n_rows, start, 0)
    lax.fori_loop(0, n_rows, wait, 0)


def _expert_kernel(blk_e_ref, blk_valid_ref, tok_ref, u_hbm, w1_ref, w3_ref, w2_ref, y_ref,
                   xs_ref, idx_smem, sem_idx, sem_rows):
    del blk_e_ref
    i = pl.program_id(0)
    blk = y_ref.shape[0]

    @pl.when(blk_valid_ref[i] > 0)
    def _():
        _gather_rows(tok_ref, idx_smem, u_hbm, xs_ref, blk, lambda r: idx_smem[0, r],
                     sem_idx, sem_rows)
        x = xs_ref[...].astype(jnp.bfloat16)
        h1 = jnp.dot(x, w1_ref[...], preferred_element_type=jnp.float32)
        h3 = jnp.dot(x, w3_ref[...], preferred_element_type=jnp.float32)
        hb = (_silu(h1) * h3).astype(jnp.bfloat16)
        y_ref[...] = jnp.dot(hb, w2_ref[...], preferred_element_type=jnp.float32)

    @pl.when(blk_valid_ref[i] == 0)
    def _():
        y_ref[...] = jnp.zeros_like(y_ref)


def _experts(u2, buf_tok, blk_e, blk_valid, w1, w3, w2, n_blk):
    blk = MOE_BLK
    d = D_MODEL
    grid_spec = pltpu.PrefetchScalarGridSpec(
        num_scalar_prefetch=2,
        grid=(n_blk,),
        in_specs=[
            pl.BlockSpec((1, 1, blk), lambda i, be, bv: (i, 0, 0)),
            pl.BlockSpec(memory_space=pl.ANY),
            pl.BlockSpec((None, d, EXPERT_FF), lambda i, be, bv: (be[i], 0, 0)),
            pl.BlockSpec((None, d, EXPERT_FF), lambda i, be, bv: (be[i], 0, 0)),
            pl.BlockSpec((None, EXPERT_FF, d), lambda i, be, bv: (be[i], 0, 0)),
        ],
        out_specs=pl.BlockSpec((blk, d), lambda i, be, bv: (i, 0)),
        scratch_shapes=[pltpu.VMEM((blk, d), jnp.float32),
                        pltpu.SMEM((1, blk), jnp.int32),
                        pltpu.SemaphoreType.DMA,
                        pltpu.SemaphoreType.DMA],
    )
    return pl.pallas_call(
        _expert_kernel,
        out_shape=jax.ShapeDtypeStruct((n_blk * blk, d), jnp.float32),
        grid_spec=grid_spec,
        compiler_params=_cparams(1),
        name="moe_experts",
    )(blk_e, blk_valid, buf_tok.reshape(n_blk, 1, blk), u2, w1, w3, w2)


def _final_kernel(slot_ref, gate_ref, y_hbm, x_ref, u_ref, g2_ref, lng_ref, lnb_ref,
                  ws1_ref, ws3_ref, ws2_ref, o_ref, rows_ref, idx_smem, sem_idx, sem_rows,
                  *, tiles_per_seq, n_batch, alpha):
    bidx = jnp.minimum(pl.program_id(0) // tiles_per_seq, n_batch)
    tm = x_ref.shape[0]
    _gather_rows(slot_ref, idx_smem, y_hbm, rows_ref, TOP_K * tm, lambda r: idx_smem[0, r],
                 sem_idx, sem_rows)
    gate = gate_ref[...]
    routed = rows_ref[0:tm, :] * gate[:, 0:1]
    for k in range(1, TOP_K):
        routed = routed + rows_ref[k * tm:(k + 1) * tm, :] * gate[:, k:k + 1]
    u = u_ref[...].astype(jnp.bfloat16)
    h1 = jnp.dot(u, ws1_ref[...], preferred_element_type=jnp.float32)
    h3 = jnp.dot(u, ws3_ref[...], preferred_element_type=jnp.float32)
    shared = jnp.dot((_silu(h1) * h3).astype(jnp.bfloat16), ws2_ref[...],
                     preferred_element_type=jnp.float32)
    r = alpha * x_ref[...] + g2_ref[pl.ds(bidx, 1), :] * (routed + shared)
    o_ref[...] = _layer_norm_f32(r) * lng_ref[...] + lnb_ref[...]


def _final(slot_km, gate_t, y_slots, x_mid, u2, mod, ln_g, ln_b, ws1, ws3, ws2,
           n_rows, seq_len, n_batch, alpha):
    tm = ROW_TILE
    d = D_MODEL
    n_tiles = n_rows // tm
    row = lambda i: (i, 0)
    const = lambda i: (0, 0)
    return pl.pallas_call(
        functools.partial(_final_kernel, tiles_per_seq=seq_len // tm, n_batch=n_batch, alpha=alpha),
        out_shape=jax.ShapeDtypeStruct((n_rows, d), jnp.float32),
        grid=(n_tiles,),
        in_specs=[
            pl.BlockSpec((1, 1, TOP_K * tm), lambda i: (i, 0, 0)),
            pl.BlockSpec((tm, TOP_K), row),
            pl.BlockSpec(memory_space=pl.ANY),
            pl.BlockSpec((tm, d), row),
            pl.BlockSpec((tm, d), row),
            pl.BlockSpec((8, d), lambda i: (0, 5)),
            pl.BlockSpec((1, d), const),
            pl.BlockSpec((1, d), const),
            pl.BlockSpec((d, SHARED_FF), const),
            pl.BlockSpec((d, SHARED_FF), const),
            pl.BlockSpec((SHARED_FF, d), const),
        ],
        out_specs=pl.BlockSpec((tm, d), row),
        scratch_shapes=[pltpu.VMEM((TOP_K * tm, d), jnp.float32),
                        pltpu.SMEM((1, TOP_K * tm), jnp.int32),
                        pltpu.SemaphoreType.DMA,
                        pltpu.SemaphoreType.DMA],
        compiler_params=_cparams(1),
        name="moe_combine_postnorm",
    )(slot_km, gate_t, y_slots, x_mid, u2, mod, ln_g, ln_b, ws1, ws3, ws2)


def _moe_sublayer(x_mid, u2, mod, ln_g, ln_b, router_w, router_bias, w1, w3, w2, ws1, ws3, ws2,
                  n_rows, seq_len, n_batch, alpha):
    bf = jnp.bfloat16
    blk = MOE_BLK
    tm = ROW_TILE
    top_e, gate, pos, counts = _router(u2, router_w, router_bias, n_rows)
    counts = counts.reshape(N_EXPERTS).astype(jnp.int32)
    padded = (counts + blk - 1) // blk * blk
    pends = jnp.cumsum(padded)
    pstart = pends - padded
    slot = jnp.take(pstart, top_e, axis=0) + pos
    n_asg = n_rows * TOP_K
    n_blk = -(-(n_asg + N_EXPERTS * (blk - 1)) // blk)
    tok_ids = jnp.broadcast_to(jnp.arange(n_rows, dtype=jnp.int32)[None, :], slot.shape)
    buf_tok = jnp.zeros((n_blk * blk,), jnp.int32).at[slot.reshape(-1)].set(
        tok_ids.reshape(-1), unique_indices=True)
    blk_start = jnp.arange(n_blk, dtype=jnp.int32) * blk
    blk_e = jnp.minimum(jnp.searchsorted(pends, blk_start, side='right'),
                        N_EXPERTS - 1).astype(jnp.int32)
    blk_valid = (blk_start < pends[-1]).astype(jnp.int32)
    y_slots = _experts(u2, buf_tok, blk_e, blk_valid, w1.astype(bf), w3.astype(bf),
                       w2.astype(bf), n_blk)
    n_tiles = n_rows // tm
    slot_km = slot.reshape(TOP_K, n_tiles, tm).transpose(1, 0, 2).reshape(n_tiles, 1, TOP_K * tm)
    return _final(slot_km, gate.T, y_slots, x_mid, u2, mod, ln_g, ln_b,
                  ws1.astype(bf), ws3.astype(bf), ws2.astype(bf),
                  n_rows, seq_len, n_batch, alpha)


def _pack_in_proj(w_in):
    d = w_in.shape[0]
    w_dt = jnp.zeros((d, SSM_GROUPS * LANES), w_in.dtype)
    for g in range(SSM_GROUPS):
        for direction, lane0 in ((0, 0), (1, DT_BWD_LANE)):
            src = OFF_DT + direction * SSM_HEADS + g * HEADS_PER_GROUP
            w_dt = w_dt.at[:, g * LANES + lane0:g * LANES + lane0 + HEADS_PER_GROUP].set(
                w_in[:, src:src + HEADS_PER_GROUP])
    return jnp.concatenate([
        w_in[:, OFF_XBC:OFF_DT], w_dt, w_in[:, OFF_FOUR:OFF_GATE],
        w_in[:, OFF_GATE:], w_in[:, :OFF_XBC]], axis=1).astype(jnp.bfloat16)


def _pack_head_rows(v):
    out = jnp.zeros((SSM_GROUPS, 1, LANES), jnp.float32)
    v = v.reshape(2, SSM_GROUPS, HEADS_PER_GROUP).astype(jnp.float32)
    out = out.at[:, 0, 0:HEADS_PER_GROUP].set(v[0])
    out = out.at[:, 0, DT_BWD_LANE:DT_BWD_LANE + HEADS_PER_GROUP].set(v[1])
    return out


def kernel(x, c, ctx, c_ctx, w_ada, b_ada, w_in, conv_w, conv_b, dt_bias, a_log, d_skip,
           ssm_norm_w, w_br_ssm, w_br_four, w_out, ln1_g, ln1_b, ln2_g, ln2_b,
           router_w, router_bias, w1, w3, w2, ws1, ws3, ws2):
    n_batch, seq_len, d = x.shape
    ctx_len = ctx.shape[1]
    depth = w_ada.shape[0]
    bf = jnp.bfloat16
    alpha = float((2 * depth) ** 0.25)
    n_lat = n_batch * seq_len
    assert d == D_MODEL and n_batch + 1 <= 8
    assert seq_len % K1_TM == 0 and (n_batch * ctx_len) % K1_TM == 0
    assert seq_len % (LANES * 8) == 0

    x_all = _assemble_stream(x, ctx)
    c_rows = jnp.zeros((8, d), jnp.float32).at[:n_batch].set(c).at[n_batch].set(c_ctx)
    mod_all = _ada_mod(c_rows, w_ada, b_ada)
    tables = _dft_tables(seq_len)

    for i in range(depth):
        last = i == depth - 1
        mod = mod_all[i]
        p_all = _in_proj(x_all, mod, _pack_in_proj(w_in[i]), seq_len, n_batch)
        xbc = _conv(p_all, conv_w[i], conv_b[i], seq_len, n_lat, ctx_len)
        a_rows = _pack_head_rows(-jnp.exp(a_log[i].astype(jnp.float32)))
        yf, yb = _ssd(xbc, p_all, _pack_head_rows(dt_bias[i]), a_rows, n_batch, seq_len, ctx_len)
        f_all = _fourier_latent(p_all, n_batch, seq_len, tables)
        n_rows = n_lat if last else x_all.shape[0]
        if not last:
            f_all = jnp.concatenate(
                [f_all, _fourier_ctx(p_all, n_batch, seq_len, ctx_len, tables)], axis=0)
        d_cols = jnp.repeat(d_skip[i].astype(jnp.float32), SSM_HEAD_DIM).reshape(1, D_INNER)
        x_mid, u2 = _merge(p_all, xbc, yf, yb, f_all, x_all, mod, d_cols,
                           ssm_norm_w[i].reshape(1, D_INNER), ln1_g[i].reshape(1, d),
                           ln1_b[i].reshape(1, d), w_br_ssm[i].astype(bf),
                           w_br_four[i].astype(bf), w_out[i].astype(bf),
                           n_rows, seq_len, n_batch, alpha)
        x_all = _moe_sublayer(x_mid, u2, mod, ln2_g[i].reshape(1, d), ln2_b[i].reshape(1, d),
                              router_w[i], router_bias[i], w1[i], w3[i], w2[i],
                              ws1[i], ws3[i], ws2[i], n_rows, seq_len, n_batch, alpha)
    return x_all[:n_lat].reshape(n_batch, seq_len, d)
```

```python
import functools

import jax
import jax.numpy as jnp
import numpy as np
from jax import lax
from jax.experimental import pallas as pl
from jax.experimental.pallas import tpu as pltpu

D_MODEL = 1024
GRID_W = 64
POS_BASE = 10000.0
LN_EPS = 1e-6

SSM_HEADS = 24
SSM_HEAD_DIM = 64
D_INNER = SSM_HEADS * SSM_HEAD_DIM
SSM_GROUPS = 4
HEADS_PER_GROUP = SSM_HEADS // SSM_GROUPS
SSM_STATE = 128
CONV_W = 5
CONV_CH = D_INNER + 2 * SSM_GROUPS * SSM_STATE
CHUNK = 128

FOURIER_GROUPS = 4
FOURIER_GROUP_DIM = 256
FOURIER_W = FOURIER_GROUPS * FOURIER_GROUP_DIM

OFF_XBC = D_INNER
OFF_DT = OFF_XBC + CONV_CH
OFF_FOUR = OFF_DT + 2 * SSM_HEADS
OFF_GATE = OFF_FOUR + FOURIER_W

N_EXPERTS = 64
TOP_K = 8
N_EXPERT_GROUPS = 8
EXPERTS_PER_GROUP = N_EXPERTS // N_EXPERT_GROUPS
TOPK_GROUPS = 4
EXPERT_FF = 256
SHARED_FF = 256
ROUTE_SCALE = 2.5

LANES = 128
VMEM_LIMIT_BYTES = 56 * 1024 * 1024

P_XBC = 0
P_DT = CONV_CH
P_FOUR = P_DT + SSM_GROUPS * LANES
P_GATE = P_FOUR + FOURIER_W
P_Z = P_GATE + 2 * D_MODEL
P_WIDTH = P_Z + D_INNER
DT_BWD_LANE = 8

K1_TM = 512
K1_TN = 1536
ROW_TILE = 256
MOE_TILE = 512
MOE_BLK = 256
ROW_ALIGN = 16
RUN_PIECES = tuple(MOE_TILE >> s for s in range(6))
TAIL_PIECES = tuple(p for p in RUN_PIECES if p < MOE_BLK)
SORTED_ROWS = -(-(MOE_TILE * TOP_K + N_EXPERTS * (ROW_ALIGN - 1)) // MOE_TILE) * MOE_TILE


def _cparams(n_axes=1):
    return pltpu.CompilerParams(
        dimension_semantics=("arbitrary",) * n_axes,
        vmem_limit_bytes=VMEM_LIMIT_BYTES)


def _layer_norm_f32(x):
    mu = jnp.mean(x, axis=-1, keepdims=True)
    xc = x - mu
    var = jnp.mean(xc * xc, axis=-1, keepdims=True)
    return xc * lax.rsqrt(var + LN_EPS)


def _sigmoid(x):
    return 1.0 / (1.0 + jnp.exp(-x))


def _silu(x):
    return x * _sigmoid(x)


def _pos_kernel(x_ref, ctx_ref, er_ref, ec_ref, o_ref, *, n_lat):
    i = pl.program_id(0)
    half = D_MODEL // 2

    @pl.when(i < n_lat)
    def _():
        ec = ec_ref[...]
        for r in range(8):
            rows = slice(GRID_W * r, GRID_W * (r + 1))
            o_ref[rows, :half] = x_ref[rows, :half] + er_ref[r:r + 1, :]
            o_ref[rows, half:] = x_ref[rows, half:] + ec

    @pl.when(i >= n_lat)
    def _():
        o_ref[...] = ctx_ref[...]


def _assemble_stream(x, ctx):
    b, l, d = x.shape
    lc = ctx.shape[1]
    tile = 8 * GRID_W
    n_lat = (b * l) // tile
    n_ctx = (b * lc) // tile
    rows = l // GRID_W
    quarter = D_MODEL // 4
    omega = 1.0 / (POS_BASE ** (jnp.arange(quarter, dtype=jnp.float32) / quarter))
    ang_r = jnp.arange(rows, dtype=jnp.float32)[:, None] * omega
    ang_c = jnp.arange(GRID_W, dtype=jnp.float32)[:, None] * omega
    emb_r = jnp.concatenate([jnp.sin(ang_r), jnp.cos(ang_r)], -1)
    emb_c = jnp.concatenate([jnp.sin(ang_c), jnp.cos(ang_c)], -1)
    tiles_per_seq = l // tile
    return pl.pallas_call(
        functools.partial(_pos_kernel, n_lat=n_lat),
        out_shape=jax.ShapeDtypeStruct((b * l + b * lc, d), jnp.float32),
        grid=(n_lat + n_ctx,),
        in_specs=[
            pl.BlockSpec((tile, d), lambda i: (jnp.minimum(i, n_lat - 1), 0)),
            pl.BlockSpec((tile, d), lambda i: (jnp.maximum(i - n_lat, 0), 0)),
            pl.BlockSpec((8, d // 2), lambda i: (i % tiles_per_seq, 0)),
            pl.BlockSpec((GRID_W, d // 2), lambda i: (0, 0)),
        ],
        out_specs=pl.BlockSpec((tile, d), lambda i: (i, 0)),
        compiler_params=_cparams(1),
        name="assemble_stream",
    )(x.reshape(b * l, d), ctx.reshape(b * lc, d), emb_r, emb_c)


def _ada_kernel(c_ref, w_ref, b_ref, o_ref):
    c = c_ref[...]
    o_ref[...] = jnp.dot(_silu(c), w_ref[...], preferred_element_type=jnp.float32,
                         precision=lax.Precision.HIGHEST) + b_ref[...]


def _ada_mod(c_rows, w_ada, b_ada):
    depth, d, n6 = w_ada.shape
    tn = 1536
    return pl.pallas_call(
        _ada_kernel,
        out_shape=jax.ShapeDtypeStruct((depth, 8, n6), jnp.float32),
        grid=(depth, n6 // tn),
        in_specs=[
            pl.BlockSpec((8, d), lambda a, j: (0, 0)),
            pl.BlockSpec((None, d, tn), lambda a, j: (a, 0, j)),
            pl.BlockSpec((None, 1, tn), lambda a, j: (a, 0, j)),
        ],
        out_specs=pl.BlockSpec((None, 8, tn), lambda a, j: (a, 0, j)),
        compiler_params=_cparams(2),
        name="ada_mod",
    )(c_rows, w_ada, b_ada.reshape(depth, 1, n6))


def _k1_kernel(x_ref, sc_ref, sh_ref, w_ref, o_ref, u_ref, *, tiles_per_seq, n_batch):
    i = pl.program_id(0)

    @pl.when(pl.program_id(1) == 0)
    def _():
        bidx = jnp.minimum(i // tiles_per_seq, n_batch)
        xn = _layer_norm_f32(x_ref[...])
        u = xn * (1.0 + sc_ref[pl.ds(bidx, 1), :]) + sh_ref[pl.ds(bidx, 1), :]
        u_ref[...] = u.astype(jnp.bfloat16)

    o_ref[...] = jnp.dot(u_ref[...], w_ref[...],
                         preferred_element_type=jnp.float32).astype(jnp.bfloat16)


def _in_proj(x_all, mod, w_all, seq_len, n_batch):
    nt, d = x_all.shape
    tm, tn = K1_TM, K1_TN
    return pl.pallas_call(
        functools.partial(_k1_kernel, tiles_per_seq=seq_len // tm, n_batch=n_batch),
        out_shape=jax.ShapeDtypeStruct((nt, P_WIDTH), jnp.bfloat16),
        grid=(nt // tm, P_WIDTH // tn),
        in_specs=[
            pl.BlockSpec((tm, d), lambda i, j: (i, 0)),
            pl.BlockSpec((8, d), lambda i, j: (0, 1)),
            pl.BlockSpec((8, d), lambda i, j: (0, 0)),
            pl.BlockSpec((d, tn), lambda i, j: (0, j)),
        ],
        out_specs=pl.BlockSpec((tm, tn), lambda i, j: (i, j)),
        scratch_shapes=[pltpu.VMEM((tm, d), jnp.bfloat16)],
        compiler_params=_cparams(2),
        name="in_proj",
    )(x_all, mod, mod, w_all)


def _conv_kernel(cur_ref, prev_ref, next_ref, w_ref, b_ref, o_ref, *, tiles_per_seq, n_lat):
    i = pl.program_id(0)
    is_ctx = i >= n_lat
    is_start = jnp.logical_or(i % tiles_per_seq == 0, is_ctx)
    is_end = jnp.logical_or(i % tiles_per_seq == tiles_per_seq - 1, is_ctx)
    halo = prev_ref.shape[0]
    rows = cur_ref.shape[0]
    prev = jnp.where(is_start, 0.0, prev_ref[...].astype(jnp.float32))
    nxt = jnp.where(is_end, 0.0, next_ref[...].astype(jnp.float32))
    ext = jnp.concatenate([prev, cur_ref[...].astype(jnp.float32), nxt], axis=0)
    acc = b_ref[...] + w_ref[0:1, :] * ext[halo - 2:halo - 2 + rows, :]
    for k in range(1, CONV_W):
        acc = acc + w_ref[k:k + 1, :] * ext[halo - 2 + k:halo - 2 + k + rows, :]
    o_ref[...] = _silu(acc).astype(jnp.bfloat16)


def _conv(p_all, conv_w, conv_b, seq_len, n_lat_rows, ctx_len):
    nt = p_all.shape[0]
    tl = ROW_TILE
    assert ctx_len == tl, "context sequences must span exactly one conv tile"
    halo = 16
    hb = tl // halo
    n_halo_blocks = nt // halo
    w8 = jnp.zeros((8, CONV_CH), jnp.float32).at[:CONV_W].set(conv_w)
    return pl.pallas_call(
        functools.partial(_conv_kernel, tiles_per_seq=seq_len // tl, n_lat=n_lat_rows // tl),
        out_shape=jax.ShapeDtypeStruct((nt, CONV_CH), jnp.bfloat16),
        grid=(nt // tl,),
        in_specs=[
            pl.BlockSpec((tl, CONV_CH), lambda i: (i, 0)),
            pl.BlockSpec((halo, CONV_CH), lambda i: (jnp.maximum(i * hb - 1, 0), 0)),
            pl.BlockSpec((halo, CONV_CH),
                         lambda i: (jnp.minimum((i + 1) * hb, n_halo_blocks - 1), 0)),
            pl.BlockSpec((8, CONV_CH), lambda i: (0, 0)),
            pl.BlockSpec((1, CONV_CH), lambda i: (0, 0)),
        ],
        out_specs=pl.BlockSpec((tl, CONV_CH), lambda i: (i, 0)),
        compiler_params=_cparams(1),
        name="dwconv_silu",
    )(p_all, p_all, p_all, w8, conv_b.reshape(1, CONV_CH))


def _softplus(x):
    return jnp.maximum(x, 0.0) + jnp.log1p(jnp.exp(-jnp.abs(x)))


def _ssd_direction(x_ref, b_ref, c_ref, dt_ref, bias, a_row, s_ref, y_ref, *, lane0, forward):
    q = CHUNK
    row_i = lax.broadcasted_iota(jnp.int32, (q, q), 0)
    col_i = lax.broadcasted_iota(jnp.int32, (q, q), 1)
    tri = (row_i >= col_i) if forward else (row_i <= col_i)
    lane_lo = lax.broadcasted_iota(jnp.int32, (q, LANES), 1) < SSM_HEAD_DIM
    lane_lo_row = lax.broadcasted_iota(jnp.int32, (1, LANES), 1) < SSM_HEAD_DIM

    dt = _softplus(dt_ref[...].astype(jnp.float32) + bias)
    adt = dt * a_row
    cum = jnp.dot(tri.astype(jnp.float32), adt, preferred_element_type=jnp.float32,
                  precision=lax.Precision.HIGHEST)
    cum_t = cum.T
    total = cum[q - 1:q, :] if forward else cum[0:1, :]

    cm = c_ref[...]
    bm = b_ref[...]
    scores = lax.dot_general(cm, bm, (((1,), (1,)), ((), ())),
                             preferred_element_type=jnp.float32)
    bm_t = bm.astype(jnp.float32).T.astype(jnp.bfloat16)

    for pr in range(HEADS_PER_GROUP // 2):
        r0 = lane0 + 2 * pr
        r1 = r0 + 1
        lanes = slice(LANES * pr, LANES * (pr + 1))
        xp = x_ref[:, lanes].astype(jnp.float32)
        dt_pair = jnp.where(lane_lo, dt[:, r0:r0 + 1], dt[:, r1:r1 + 1])
        cum_pair = jnp.where(lane_lo, cum[:, r0:r0 + 1], cum[:, r1:r1 + 1])
        tot_pair = jnp.where(lane_lo_row, total[:, r0:r0 + 1], total[:, r1:r1 + 1])
        xdt = xp * dt_pair
        l0 = jnp.exp(jnp.where(tri, cum[:, r0:r0 + 1] - cum_t[r0:r0 + 1, :], -jnp.inf))
        l1 = jnp.exp(jnp.where(tri, cum[:, r1:r1 + 1] - cum_t[r1:r1 + 1, :], -jnp.inf))
        w = jnp.concatenate([(scores * l0).astype(jnp.bfloat16),
                             (scores * l1).astype(jnp.bfloat16)], axis=1)
        xdt_b = xdt.astype(jnp.bfloat16)
        zero = jnp.zeros_like(xdt_b)
        rhs = jnp.concatenate([jnp.where(lane_lo, xdt_b, zero),
                               jnp.where(lane_lo, zero, xdt_b)], axis=0)
        y_diag = jnp.dot(w, rhs, preferred_element_type=jnp.float32)
        s_old = s_ref[:, lanes]
        y_off = jnp.dot(cm, s_old.astype(jnp.bfloat16),
                        preferred_element_type=jnp.float32) * jnp.exp(cum_pair)
        y_ref[:, lanes] = (y_diag + y_off).astype(jnp.bfloat16)
        decayed = (xdt * jnp.exp(tot_pair - cum_pair)).astype(jnp.bfloat16)
        s_ref[:, lanes] = jnp.exp(tot_pair) * s_old + jnp.dot(
            bm_t, decayed, preferred_element_type=jnp.float32)


def _ssd_kernel(xf_ref, bf_ref, cf_ref, dtf_ref, xb_ref, bb_ref, cb_ref, dtb_ref,
                bias_ref, a_ref, yf_ref, yb_ref, sf_ref, sb_ref):
    @pl.when(pl.program_id(2) == 0)
    def _():
        sf_ref[...] = jnp.zeros_like(sf_ref)
        sb_ref[...] = jnp.zeros_like(sb_ref)

    bias = bias_ref[...]
    a_row = a_ref[...]
    _ssd_direction(xf_ref, bf_ref, cf_ref, dtf_ref, bias, a_row, sf_ref, yf_ref,
                   lane0=0, forward=True)
    _ssd_direction(xb_ref, bb_ref, cb_ref, dtb_ref, bias, a_row, sb_ref, yb_ref,
                   lane0=DT_BWD_LANE, forward=False)


def _ssd(xbc, p_all, dt_bias_rows, a_rows, n_batch, seq_len, ctx_len):
    nt = xbc.shape[0]
    q = CHUNK
    nc_lat = seq_len // q
    nc_ctx = ctx_len // q
    ctx_blk0 = (n_batch * seq_len) // q
    n_steps = nc_ctx + nc_lat
    gw = HEADS_PER_GROUP * SSM_HEAD_DIM
    b_blk0 = D_INNER // SSM_STATE
    c_blk0 = b_blk0 + SSM_GROUPS
    dt_blk0 = P_DT // LANES

    def fwd_row(b, j):
        return jnp.where(j < nc_ctx, ctx_blk0 + nc_ctx * b + j, nc_lat * b + (j - nc_ctx))

    def bwd_row(b, j):
        return jnp.where(j < nc_ctx, ctx_blk0 + nc_ctx * b + (nc_ctx - 1 - j),
                         nc_lat * b + (n_steps - 1 - j))

    def specs(row):
        return [
            pl.BlockSpec((q, gw), lambda b, g, j: (row(b, j), g)),
            pl.BlockSpec((q, SSM_STATE), lambda b, g, j: (row(b, j), b_blk0 + g)),
            pl.BlockSpec((q, SSM_STATE), lambda b, g, j: (row(b, j), c_blk0 + g)),
            pl.BlockSpec((q, LANES), lambda b, g, j: (row(b, j), dt_blk0 + g)),
        ]

    vec_spec = pl.BlockSpec((None, 1, LANES), lambda b, g, j: (g, 0, 0))
    out_sds = jax.ShapeDtypeStruct((nt, D_INNER), jnp.bfloat16)
    return pl.pallas_call(
        _ssd_kernel,
        out_shape=(out_sds, out_sds),
        grid=(n_batch, SSM_GROUPS, n_steps),
        in_specs=specs(fwd_row) + specs(bwd_row) + [vec_spec, vec_spec],
        out_specs=(pl.BlockSpec((q, gw), lambda b, g, j: (fwd_row(b, j), g)),
                   pl.BlockSpec((q, gw), lambda b, g, j: (bwd_row(b, j), g))),
        scratch_shapes=[pltpu.VMEM((SSM_STATE, gw), jnp.float32),
                        pltpu.VMEM((SSM_STATE, gw), jnp.float32)],
        compiler_params=_cparams(3),
        name="ssd_scan",
    )(xbc, xbc, xbc, p_all, xbc, xbc, xbc, p_all, dt_bias_rows, a_rows)


def _dft_tables(seq_len):
    l1n = seq_len // LANES
    two_pi = 2.0 * np.pi
    gd = FOURIER_GROUP_DIM
    jj = jnp.arange(gd, dtype=jnp.int32)
    ang_c = ((jj[:, None] * jj[None, :]) % gd).astype(jnp.float32) * (two_pi / gd)
    cc, sc = jnp.cos(ang_c), jnp.sin(ang_c)
    k1 = jnp.arange(l1n, dtype=jnp.int32)
    ang1 = ((k1[:, None] * k1[None, :]) % l1n).astype(jnp.float32) * (two_pi / l1n)
    w1 = jnp.concatenate([jnp.cos(ang1), -jnp.sin(ang1)], axis=0)
    k2 = jnp.arange(LANES, dtype=jnp.int32)
    kk = k1[:, None, None] + l1n * k2[None, :, None]
    ang2 = ((kk * k2[None, None, :]) % seq_len).astype(jnp.float32) * (two_pi / seq_len)
    er, ei = jnp.cos(ang2), -jnp.sin(ang2)
    e = jnp.concatenate([jnp.concatenate([er, -ei], axis=2),
                         jnp.concatenate([ei, er], axis=2)], axis=1)
    return cc, sc, w1, e


def _dft1_kernel(w_ref, x_ref, o_ref):
    o_ref[...] = jnp.dot(w_ref[...], x_ref[...],
                         preferred_element_type=jnp.float32).astype(jnp.bfloat16)


def _channel_mix(gr, gi, cs_ref, o_ref):
    gd = FOURIER_GROUP_DIM
    for g in range(FOURIER_GROUPS):
        cols = slice(gd * g, gd * (g + 1))
        lhs = jnp.concatenate([gr[:, cols], gi[:, cols]], axis=1).astype(jnp.bfloat16)
        o_ref[:, cols] = jnp.dot(lhs, cs_ref[...],
                                 preferred_element_type=jnp.float32).astype(jnp.bfloat16)


def _dft2_kernel(e_ref, ar_ref, ai_ref, cs_ref, o_ref):
    a = jnp.concatenate([ar_ref[...], ai_ref[...]], axis=0)
    g = jnp.dot(e_ref[...], a, preferred_element_type=jnp.float32)
    half = g.shape[0] // 2
    _channel_mix(g[:half], g[half:], cs_ref, o_ref)


def _dft_ctx_kernel(w_ref, x_ref, cs_ref, o_ref):
    g = jnp.dot(w_ref[...], x_ref[...], preferred_element_type=jnp.float32)
    half = g.shape[0] // 2
    _channel_mix(g[:half], g[half:], cs_ref, o_ref)


def _fourier_latent(p_all, n_batch, seq_len, tables):
    cc, sc, w1, e = tables
    c = FOURIER_W
    l1n = seq_len // LANES
    ncol = LANES * c
    four = p_all[:n_batch * seq_len, P_FOUR:P_FOUR + c].reshape(n_batch, l1n, ncol)
    tn = 4096
    a = pl.pallas_call(
        _dft1_kernel,
        out_shape=jax.ShapeDtypeStruct((n_batch, 2 * l1n, ncol), jnp.bfloat16),
        grid=(n_batch, ncol // tn),
        in_specs=[pl.BlockSpec((2 * l1n, l1n), lambda b, j: (0, 0)),
                  pl.BlockSpec((None, l1n, tn), lambda b, j: (b, 0, j))],
        out_specs=pl.BlockSpec((None, 2 * l1n, tn), lambda b, j: (b, 0, j)),
        compiler_params=_cparams(2),
        name="dft_stage1",
    )(w1.astype(jnp.bfloat16), four)
    a4 = a.reshape(n_batch, 2 * l1n, LANES, c)
    norm = 1.0 / np.sqrt(float(seq_len) * FOURIER_GROUP_DIM)
    cs = (jnp.concatenate([cc, sc], axis=0) * norm).astype(jnp.bfloat16)
    o = pl.pallas_call(
        _dft2_kernel,
        out_shape=jax.ShapeDtypeStruct((n_batch, l1n, LANES, c), jnp.bfloat16),
        grid=(n_batch, l1n),
        in_specs=[pl.BlockSpec((None, 2 * LANES, 2 * LANES), lambda b, k: (k, 0, 0)),
                  pl.BlockSpec((None, None, LANES, c), lambda b, k: (b, k, 0, 0)),
                  pl.BlockSpec((None, None, LANES, c), lambda b, k: (b, l1n + k, 0, 0)),
                  pl.BlockSpec((2 * FOURIER_GROUP_DIM, FOURIER_GROUP_DIM), lambda b, k: (0, 0))],
        out_specs=pl.BlockSpec((None, None, LANES, c), lambda b, k: (b, k, 0, 0)),
        compiler_params=_cparams(2),
        name="dft_stage2",
    )(e.astype(jnp.bfloat16), a4, a4, cs)
    return o.transpose(0, 2, 1, 3).reshape(n_batch * seq_len, c)


def _fourier_ctx(p_all, n_batch, seq_len, ctx_len, tables):
    cc, sc, _, _ = tables
    assert ctx_len == FOURIER_GROUP_DIM
    c = FOURIER_W
    wc = jnp.concatenate([cc, -sc], axis=0).astype(jnp.bfloat16)
    norm = 1.0 / np.sqrt(float(ctx_len) * FOURIER_GROUP_DIM)
    cs = (jnp.concatenate([cc, sc], axis=0) * norm).astype(jnp.bfloat16)
    blk0 = (n_batch * seq_len) // ctx_len
    return pl.pallas_call(
        _dft_ctx_kernel,
        out_shape=jax.ShapeDtypeStruct((n_batch * ctx_len, c), jnp.bfloat16),
        grid=(n_batch,),
        in_specs=[pl.BlockSpec((2 * ctx_len, ctx_len), lambda b: (0, 0)),
                  pl.BlockSpec((ctx_len, c), lambda b: (blk0 + b, P_FOUR // c)),
                  pl.BlockSpec((2 * FOURIER_GROUP_DIM, FOURIER_GROUP_DIM), lambda b: (0, 0))],
        out_specs=pl.BlockSpec((ctx_len, c), lambda b: (b, 0)),
        compiler_params=_cparams(1),
        name="dft_ctx",
    )(wc, p_all, cs)


def _merge_kernel(gate_ref, z_ref, xs_ref, yf_ref, yb_ref, f_ref, x_ref,
                  g1_ref, sc2_ref, sh2_ref, d_ref, nw_ref, lng_ref, lnb_ref,
                  wssm_ref, wfour_ref, wout_ref, xo_ref, u2_ref, u2b_ref,
                  *, tiles_per_seq, n_batch, alpha):
    bidx = jnp.minimum(pl.program_id(0) // tiles_per_seq, n_batch)
    y = (yf_ref[...].astype(jnp.float32) + yb_ref[...].astype(jnp.float32)
         + xs_ref[...].astype(jnp.float32) * d_ref[...])
    h = y * _silu(z_ref[...].astype(jnp.float32))
    h = h * lax.rsqrt(jnp.mean(h * h, axis=-1, keepdims=True) + LN_EPS) * nw_ref[...]
    ssm = jnp.dot(h.astype(jnp.bfloat16), wssm_ref[...], preferred_element_type=jnp.float32)
    four = jnp.dot(f_ref[...], wfour_ref[...], preferred_element_type=jnp.float32)
    gates = _sigmoid(gate_ref[...].astype(jnp.float32))
    merged = gates[:, :D_MODEL] * ssm + gates[:, D_MODEL:] * four
    out = jnp.dot(merged.astype(jnp.bfloat16), wout_ref[...], preferred_element_type=jnp.float32)
    r = alpha * x_ref[...] + g1_ref[pl.ds(bidx, 1), :] * out
    xn = _layer_norm_f32(r) * lng_ref[...] + lnb_ref[...]
    xo_ref[...] = xn
    u2 = (_layer_norm_f32(xn) * (1.0 + sc2_ref[pl.ds(bidx, 1), :])
          + sh2_ref[pl.ds(bidx, 1), :])
    u2_ref[...] = u2
    u2b_ref[...] = u2.astype(jnp.bfloat16)


def _merge(p_all, xbc, yf, yb, f_all, x_all, mod, d_cols, norm_w, ln_g, ln_b,
           w_ssm, w_four, w_out, n_rows, seq_len, n_batch, alpha):
    tm = ROW_TILE
    d = D_MODEL
    row = lambda i: (i, 0)
    const = lambda i: (0, 0)
    out_sds = jax.ShapeDtypeStruct((n_rows, d), jnp.float32)
    return pl.pallas_call(
        functools.partial(_merge_kernel, tiles_per_seq=seq_len // tm, n_batch=n_batch, alpha=alpha),
        out_shape=(out_sds, out_sds, jax.ShapeDtypeStruct((n_rows, d), jnp.bfloat16)),
        grid=(n_rows // tm,),
        in_specs=[
            pl.BlockSpec((tm, 2 * d), lambda i: (i, P_GATE // (2 * d))),
            pl.BlockSpec((tm, D_INNER), lambda i: (i, P_Z // D_INNER)),
            pl.BlockSpec((tm, D_INNER), row),
            pl.BlockSpec((tm, D_INNER), row),
            pl.BlockSpec((tm, D_INNER), row),
            pl.BlockSpec((tm, FOURIER_W), row),
            pl.BlockSpec((tm, d), row),
            pl.BlockSpec((8, d), lambda i: (0, 2)),
            pl.BlockSpec((8, d), lambda i: (0, 4)),
            pl.BlockSpec((8, d), lambda i: (0, 3)),
            pl.BlockSpec((1, D_INNER), const),
            pl.BlockSpec((1, D_INNER), const),
            pl.BlockSpec((1, d), const),
            pl.BlockSpec((1, d), const),
            pl.BlockSpec((D_INNER, d), const),
            pl.BlockSpec((FOURIER_W, d), const),
            pl.BlockSpec((d, d), const),
        ],
        out_specs=(pl.BlockSpec((tm, d), row), pl.BlockSpec((tm, d), row),
                   pl.BlockSpec((tm, d), row)),
        compiler_params=_cparams(1),
        name="merge_postnorm",
    )(p_all, p_all, xbc, yf, yb, f_all, x_all, mod, mod, mod, d_cols, norm_w,
      ln_g, ln_b, w_ssm, w_four, w_out)


def _first_index_of_max(v, iota, big):
    m = jnp.max(v, axis=0, keepdims=True)
    idx = jnp.min(jnp.where(v == m, iota, big), axis=0, keepdims=True)
    return m, idx


def _router_kernel(u_ref, wt_ref, bias_ref, tri_ref, g_ref, p_ref, cnt_ref, off_ref,
                   basetab_ref, tot_ref, base_ref):
    @pl.when(pl.program_id(0) == 0)
    def _():
        base_ref[...] = jnp.zeros_like(base_ref)

    tm = u_ref.shape[0]
    ne, epg = N_EXPERTS, EXPERTS_PER_GROUP
    neg = -jnp.inf
    logits = lax.dot_general(wt_ref[...], u_ref[...], (((1,), (1,)), ((), ())),
                             preferred_element_type=jnp.float32,
                             precision=lax.Precision.HIGHEST)
    scores = _sigmoid(logits)
    sel = scores + bias_ref[...]
    iota_g = lax.broadcasted_iota(jnp.int32, (epg, tm), 0).astype(jnp.float32)
    grp_rows = []
    for g in range(N_EXPERT_GROUPS):
        v = sel[epg * g:epg * (g + 1), :]
        m1, i1 = _first_index_of_max(v, iota_g, epg)
        m2 = jnp.max(jnp.where(iota_g == i1, neg, v), axis=0, keepdims=True)
        grp_rows.append(m1 + m2)
    grp = jnp.concatenate(grp_rows, axis=0)
    iota_n = lax.broadcasted_iota(jnp.int32, (N_EXPERT_GROUPS, tm), 0).astype(jnp.float32)
    chosen = jnp.zeros((N_EXPERT_GROUPS, tm), jnp.float32)
    for _ in range(TOPK_GROUPS):
        _, gi = _first_index_of_max(grp, iota_n, N_EXPERT_GROUPS)
        hit = iota_n == gi
        chosen = jnp.where(hit, 1.0, chosen)
        grp = jnp.where(hit, neg, grp)
    masked = jnp.concatenate(
        [jnp.where(chosen[g:g + 1, :] > 0.0, sel[epg * g:epg * (g + 1), :], neg)
         for g in range(N_EXPERT_GROUPS)], axis=0)
    iota_e = lax.broadcasted_iota(jnp.int32, (ne, tm), 0).astype(jnp.float32)
    picked = jnp.zeros((ne, tm), jnp.float32)
    e_rows, g_rows = [], []
    for _ in range(TOP_K):
        _, ei = _first_index_of_max(masked, iota_e, ne)
        hit = iota_e == ei
        e_rows.append(ei)
        g_rows.append(jnp.sum(jnp.where(hit, scores, 0.0), axis=0, keepdims=True))
        picked = jnp.where(hit, 1.0, picked)
        masked = jnp.where(hit, neg, masked)
    gate = jnp.concatenate(g_rows, axis=0)
    gate = gate / jnp.sum(gate, axis=0, keepdims=True) * ROUTE_SCALE
    before = jnp.dot(picked.astype(jnp.bfloat16), tri_ref[...],
                     preferred_element_type=jnp.float32)
    cnt = jnp.sum(picked, axis=1, keepdims=True)
    cnt16 = jnp.ceil(cnt * (1.0 / ROW_ALIGN)) * ROW_ALIGN
    cnt16_b = jnp.broadcast_to(cnt16, (ne, LANES))
    e_row = lax.broadcasted_iota(jnp.int32, (ne, ne), 0)
    e_col = lax.broadcasted_iota(jnp.int32, (ne, ne), 1)
    off_b = jnp.dot((e_col < e_row).astype(jnp.float32), cnt16_b,
                    preferred_element_type=jnp.float32,
                    precision=lax.Precision.HIGHEST)
    local = before + off_b[:, 0:1]
    p_rows = [jnp.sum(jnp.where(iota_e == e_rows[k], local, 0.0), axis=0, keepdims=True)
              for k in range(TOP_K)]
    g_ref[...] = gate
    p_ref[...] = jnp.concatenate(p_rows, axis=0).astype(jnp.int32)
    cnt_ref[...] = cnt16_b
    off_ref[...] = off_b
    basetab_ref[...] = jnp.broadcast_to(base_ref[...], (ne, LANES))
    base_ref[...] = base_ref[...] + cnt16
    tot_ref[...] = base_ref[...]


def _router(u2, router_w, router_bias, n_tok):
    tm = MOE_TILE
    n_tiles = n_tok // tm
    tri = jnp.triu(jnp.ones((tm, tm), jnp.float32), k=1).astype(jnp.bfloat16)
    kt = lambda i: (0, i)
    tab_sds = jax.ShapeDtypeStruct((n_tiles, N_EXPERTS, LANES), jnp.float32)
    tab_spec = pl.BlockSpec((None, N_EXPERTS, LANES), lambda i: (i, 0, 0))
    return pl.pallas_call(
        _router_kernel,
        out_shape=(jax.ShapeDtypeStruct((TOP_K, n_tok), jnp.float32),
                   jax.ShapeDtypeStruct((TOP_K, n_tok), jnp.int32),
                   tab_sds, tab_sds, tab_sds,
                   jax.ShapeDtypeStruct((N_EXPERTS, 1), jnp.float32)),
        grid=(n_tiles,),
        in_specs=[pl.BlockSpec((tm, D_MODEL), lambda i: (i, 0)),
                  pl.BlockSpec((N_EXPERTS, D_MODEL), lambda i: (0, 0)),
                  pl.BlockSpec((N_EXPERTS, 1), lambda i: (0, 0)),
                  pl.BlockSpec((tm, tm), lambda i: (0, 0))],
        out_specs=(pl.BlockSpec((TOP_K, tm), kt), pl.BlockSpec((TOP_K, tm), kt),
                   tab_spec, tab_spec, tab_spec,
                   pl.BlockSpec((N_EXPERTS, 1), lambda i: (0, 0))),
        scratch_shapes=[pltpu.VMEM((N_EXPERTS, 1), jnp.float32)],
        compiler_params=_cparams(1),
        name="moe_router",
    )(u2, router_w.T, router_bias.reshape(N_EXPERTS, 1), tri)


def _run_copies(cnt, make_copy, pieces):
    done = jnp.int32(0)
    for piece in pieces:
        hit = (cnt & piece) != 0

        @pl.when(hit)
        def _(done=done, piece=piece):
            make_copy(done, piece)

        done = done + (cnt & piece)


def _for_each_run(tile, cnt_ref, off_ref, dst_ref, make_copy, start):
    def body(e, carry):
        idx = tile * N_EXPERTS + e
        off = pl.multiple_of(off_ref[idx], ROW_ALIGN)
        dst = pl.multiple_of(dst_ref[idx], ROW_ALIGN)

        def piece_copy(done, piece):
            cp = make_copy(pl.multiple_of(off + done, ROW_ALIGN),
                           pl.multiple_of(dst + done, ROW_ALIGN), piece)
            if start:
                cp.start()
            else:
                cp.wait()

        _run_copies(cnt_ref[idx], piece_copy, RUN_PIECES)
        return carry

    lax.fori_loop(0, N_EXPERTS, body, 0)


def _dispatch_kernel(cnt_ref, off_ref, dst_ref, tail_ref, nblk_ref, u_ref, lp_ref, xs_hbm,
                     sorted_ref, zero_ref, sem):
    i = pl.program_id(0)
    tm = u_ref.shape[0]
    n_chunks = sorted_ref.shape[0] // tm
    u = u_ref[...]
    col = lax.broadcasted_iota(jnp.int32, (tm, tm), 0)
    for j in range(n_chunks):
        perm = jnp.zeros((tm, tm), jnp.float32)
        for k in range(TOP_K):
            perm = jnp.where(col == lp_ref[k:k + 1, :] - j * tm, 1.0, perm)
        sorted_ref[j * tm:(j + 1) * tm, :] = jnp.dot(
            perm.astype(jnp.bfloat16), u, preferred_element_type=jnp.float32
        ).astype(jnp.bfloat16)

    def run_copy(off, dst, piece):
        return pltpu.make_async_copy(sorted_ref.at[pl.ds(off, piece), :],
                                     xs_hbm.at[pl.ds(dst, piece), :], sem)

    _for_each_run(i, cnt_ref, off_ref, dst_ref, run_copy, start=True)
    _for_each_run(i, cnt_ref, off_ref, dst_ref, run_copy, start=False)

    @pl.when(i == pl.num_programs(0) - 1)
    def _():
        zero_ref[...] = jnp.zeros_like(zero_ref)
        blk = zero_ref.shape[0]
        n_blk_total = xs_hbm.shape[0] // blk

        def zero_copy(dst, piece):
            return pltpu.make_async_copy(zero_ref.at[pl.ds(0, piece), :],
                                         xs_hbm.at[pl.ds(dst, piece), :], sem)

        def tail_body(start):
            def body(e, carry):
                dst = pl.multiple_of(tail_ref[e], ROW_ALIGN)

                def piece_copy(done, piece):
                    cp = zero_copy(pl.multiple_of(dst + done, ROW_ALIGN), piece)
                    if start:
                        cp.start()
                    else:
                        cp.wait()

                _run_copies(tail_ref[N_EXPERTS + e], piece_copy, TAIL_PIECES)
                return carry
            return body

        def blk_body(start):
            def body(b, carry):
                cp = zero_copy(pl.multiple_of(b * blk, blk), blk)
                if start:
                    cp.start()
                else:
                    cp.wait()
                return carry
            return body

        lax.fori_loop(0, N_EXPERTS, tail_body(True), 0)
        lax.fori_loop(nblk_ref[0], n_blk_total, blk_body(True), 0)
        lax.fori_loop(0, N_EXPERTS, tail_body(False), 0)
        lax.fori_loop(nblk_ref[0], n_blk_total, blk_body(False), 0)


def _dispatch(u2b, lp, cnt, off, dst, tail, n_used_blk, n_blk):
    tm = MOE_TILE
    d = D_MODEL
    n_tok = u2b.shape[0]
    grid_spec = pltpu.PrefetchScalarGridSpec(
        num_scalar_prefetch=5,
        grid=(n_tok // tm,),
        in_specs=[pl.BlockSpec((tm, d), lambda i, *_: (i, 0)),
                  pl.BlockSpec((TOP_K, tm), lambda i, *_: (0, i))],
        out_specs=pl.BlockSpec(memory_space=pl.ANY),
        scratch_shapes=[pltpu.VMEM((SORTED_ROWS, d), jnp.bfloat16),
                        pltpu.VMEM((MOE_BLK, d), jnp.bfloat16),
                        pltpu.SemaphoreType.DMA],
    )
    return pl.pallas_call(
        _dispatch_kernel,
        out_shape=jax.ShapeDtypeStruct((n_blk * MOE_BLK, d), jnp.bfloat16),
        grid_spec=grid_spec,
        compiler_params=_cparams(1),
        name="moe_dispatch",
    )(cnt, off, dst, tail, n_used_blk, u2b, lp)


def _expert_kernel(blk_e_ref, blk_valid_ref, x_ref, w1_ref, w3_ref, w2_ref, y_ref):
    del blk_e_ref
    i = pl.program_id(0)

    @pl.when(blk_valid_ref[i] > 0)
    def _():
        x = x_ref[...]
        h1 = jnp.dot(x, w1_ref[...], preferred_element_type=jnp.float32)
        h3 = jnp.dot(x, w3_ref[...], preferred_element_type=jnp.float32)
        hb = (_silu(h1) * h3).astype(jnp.bfloat16)
        y_ref[...] = jnp.dot(hb, w2_ref[...],
                             preferred_element_type=jnp.float32).astype(jnp.bfloat16)

    @pl.when(blk_valid_ref[i] == 0)
    def _():
        y_ref[...] = jnp.zeros_like(y_ref)


def _experts(xs, blk_e, blk_valid, w1, w3, w2, n_blk):
    blk = MOE_BLK
    d = D_MODEL
    grid_spec = pltpu.PrefetchScalarGridSpec(
        num_scalar_prefetch=2,
        grid=(n_blk,),
        in_specs=[
            pl.BlockSpec((blk, d), lambda i, be, bv: (i, 0)),
            pl.BlockSpec((None, d, EXPERT_FF), lambda i, be, bv: (be[i], 0, 0)),
            pl.BlockSpec((None, d, EXPERT_FF), lambda i, be, bv: (be[i], 0, 0)),
            pl.BlockSpec((None, EXPERT_FF, d), lambda i, be, bv: (be[i], 0, 0)),
        ],
        out_specs=pl.BlockSpec((blk, d), lambda i, be, bv: (i, 0)),
    )
    return pl.pallas_call(
        _expert_kernel,
        out_shape=jax.ShapeDtypeStruct((n_blk * blk, d), jnp.bfloat16),
        grid_spec=grid_spec,
        compiler_params=_cparams(1),
        name="moe_experts",
    )(blk_e, blk_valid, xs, w1, w3, w2)


def _final_kernel(cnt_ref, off_ref, dst_ref, lp_ref, gate_ref, y_hbm, x_ref, u_ref, g2_ref,
                  lng_ref, lnb_ref, ws1_ref, ws3_ref, ws2_ref, o_ref, sorted_ref, sem,
                  *, tiles_per_seq, n_batch, alpha):
    i = pl.program_id(0)
    bidx = jnp.minimum(i // tiles_per_seq, n_batch)
    tm = x_ref.shape[0]
    n_chunks = sorted_ref.shape[0] // tm

    @pl.when(i == 0)
    def _():
        sorted_ref[...] = jnp.zeros_like(sorted_ref)

    def run_copy(off, dst, piece):
        return pltpu.make_async_copy(y_hbm.at[pl.ds(dst, piece), :],
                                     sorted_ref.at[pl.ds(off, piece), :], sem)

    _for_each_run(i, cnt_ref, off_ref, dst_ref, run_copy, start=True)
    _for_each_run(i, cnt_ref, off_ref, dst_ref, run_copy, start=False)

    lp = lp_ref[...]
    gate = gate_ref[...]
    lane = lax.broadcasted_iota(jnp.int32, (tm, tm), 1)
    routed = jnp.zeros((tm, x_ref.shape[1]), jnp.float32)
    for j in range(n_chunks):
        comb = jnp.zeros((tm, tm), jnp.float32)
        for k in range(TOP_K):
            comb = jnp.where(lane == lp[:, k:k + 1] - j * tm, gate[:, k:k + 1], comb)
        routed = routed + jnp.dot(comb.astype(jnp.bfloat16),
                                  sorted_ref[j * tm:(j + 1) * tm, :],
                                  preferred_element_type=jnp.float32)
    u = u_ref[...]
    h1 = jnp.dot(u, ws1_ref[...], preferred_element_type=jnp.float32)
    h3 = jnp.dot(u, ws3_ref[...], preferred_element_type=jnp.float32)
    shared = jnp.dot((_silu(h1) * h3).astype(jnp.bfloat16), ws2_ref[...],
                     preferred_element_type=jnp.float32)
    r = alpha * x_ref[...] + g2_ref[pl.ds(bidx, 1), :] * (routed + shared)
    o_ref[...] = _layer_norm_f32(r) * lng_ref[...] + lnb_ref[...]


def _final(cnt, off, dst, lp_t, gate_t, y_slots, x_mid, u2b, mod, ln_g, ln_b, ws1, ws3, ws2,
           n_rows, seq_len, n_batch, alpha):
    tm = MOE_TILE
    d = D_MODEL
    row = lambda i, *_: (i, 0)
    const = lambda i, *_: (0, 0)
    grid_spec = pltpu.PrefetchScalarGridSpec(
        num_scalar_prefetch=3,
        grid=(n_rows // tm,),
        in_specs=[
            pl.BlockSpec((tm, TOP_K), row),
            pl.BlockSpec((tm, TOP_K), row),
            pl.BlockSpec(memory_space=pl.ANY),
            pl.BlockSpec((tm, d), row),
            pl.BlockSpec((tm, d), row),
            pl.BlockSpec((8, d), lambda i, *_: (0, 5)),
            pl.BlockSpec((1, d), const),
            pl.BlockSpec((1, d), const),
            pl.BlockSpec((d, SHARED_FF), const),
            pl.BlockSpec((d, SHARED_FF), const),
            pl.BlockSpec((SHARED_FF, d), const),
        ],
        out_specs=pl.BlockSpec((tm, d), row),
        scratch_shapes=[pltpu.VMEM((SORTED_ROWS, d), jnp.bfloat16),
                        pltpu.SemaphoreType.DMA],
    )
    return pl.pallas_call(
        functools.partial(_final_kernel, tiles_per_seq=seq_len // tm, n_batch=n_batch, alpha=alpha),
        out_shape=jax.ShapeDtypeStruct((n_rows, d), jnp.float32),
        grid_spec=grid_spec,
        compiler_params=_cparams(1),
        name="moe_combine_postnorm",
    )(cnt, off, dst, lp_t, gate_t, y_slots, x_mid, u2b, mod, ln_g, ln_b, ws1, ws3, ws2)


def _moe_sublayer(x_mid, u2, u2b, mod, ln_g, ln_b, router_w, router_bias, w1, w3, w2,
                  ws1, ws3, ws2, n_rows, seq_len, n_batch, alpha):
    bf = jnp.bfloat16
    blk = MOE_BLK
    n_tiles = n_rows // MOE_TILE
    gate, lp, cnt_tab, off_tab, base_tab, total = _router(u2, router_w, router_bias, n_rows)
    as_int = lambda t: t[:, :, 0].astype(jnp.int32)
    cnt, off, base = as_int(cnt_tab), as_int(off_tab), as_int(base_tab)
    total = total.reshape(N_EXPERTS).astype(jnp.int32)
    padded = (total + blk - 1) // blk * blk
    pends = jnp.cumsum(padded)
    pstart = pends - padded
    dst = pstart[None, :] + base
    tail = jnp.concatenate([pstart + total, padded - total])
    n_used_blk = (pends[-1:] // blk).astype(jnp.int32)
    max_rows = n_rows * TOP_K + n_tiles * N_EXPERTS * (ROW_ALIGN - 1) + N_EXPERTS * (blk - 1)
    n_blk = -(-max_rows // blk)
    blk_start = jnp.arange(n_blk, dtype=jnp.int32) * blk
    blk_e = jnp.minimum(jnp.sum((blk_start[:, None] >= pends[None, :]).astype(jnp.int32), axis=1),
                        N_EXPERTS - 1)
    blk_valid = (blk_start < pends[-1]).astype(jnp.int32)
    flat = lambda t: t.reshape(n_tiles * N_EXPERTS)
    xs = _dispatch(u2b, lp, flat(cnt), flat(off), flat(dst), tail, n_used_blk, n_blk)
    y_slots = _experts(xs, blk_e, blk_valid, w1.astype(bf), w3.astype(bf), w2.astype(bf), n_blk)
    return _final(flat(cnt), flat(off), flat(dst), lp.T, gate.T, y_slots, x_mid, u2b, mod,
                  ln_g, ln_b, ws1.astype(bf), ws3.astype(bf), ws2.astype(bf),
                  n_rows, seq_len, n_batch, alpha)


def _pack_in_proj(w_in):
    d = w_in.shape[0]
    w_dt = jnp.zeros((d, SSM_GROUPS * LANES), w_in.dtype)
    for g in range(SSM_GROUPS):
        for direction, lane0 in ((0, 0), (1, DT_BWD_LANE)):
            src = OFF_DT + direction * SSM_HEADS + g * HEADS_PER_GROUP
            w_dt = w_dt.at[:, g * LANES + lane0:g * LANES + lane0 + HEADS_PER_GROUP].set(
                w_in[:, src:src + HEADS_PER_GROUP])
    return jnp.concatenate([
        w_in[:, OFF_XBC:OFF_DT], w_dt, w_in[:, OFF_FOUR:OFF_GATE],
        w_in[:, OFF_GATE:], w_in[:, :OFF_XBC]], axis=1).astype(jnp.bfloat16)


def _pack_head_rows(v):
    out = jnp.zeros((SSM_GROUPS, 1, LANES), jnp.float32)
    v = v.reshape(2, SSM_GROUPS, HEADS_PER_GROUP).astype(jnp.float32)
    out = out.at[:, 0, 0:HEADS_PER_GROUP].set(v[0])
    out = out.at[:, 0, DT_BWD_LANE:DT_BWD_LANE + HEADS_PER_GROUP].set(v[1])
    return out


def kernel(x, c, ctx, c_ctx, w_ada, b_ada, w_in, conv_w, conv_b, dt_bias, a_log, d_skip,
           ssm_norm_w, w_br_ssm, w_br_four, w_out, ln1_g, ln1_b, ln2_g, ln2_b,
           router_w, router_bias, w1, w3, w2, ws1, ws3, ws2):
    n_batch, seq_len, d = x.shape
    ctx_len = ctx.shape[1]
    depth = w_ada.shape[0]
    bf = jnp.bfloat16
    alpha = float((2 * depth) ** 0.25)
    n_lat = n_batch * seq_len
    assert d == D_MODEL and n_batch + 1 <= 8
    assert seq_len % K1_TM == 0 and (n_batch * ctx_len) % K1_TM == 0
    assert seq_len % (LANES * 8) == 0

    x_all = _assemble_stream(x, ctx)
    c_rows = jnp.zeros((8, d), jnp.float32).at[:n_batch].set(c).at[n_batch].set(c_ctx)
    mod_all = _ada_mod(c_rows, w_ada, b_ada)
    tables = _dft_tables(seq_len)

    for i in range(depth):
        last = i == depth - 1
        mod = mod_all[i]
        p_all = _in_proj(x_all, mod, _pack_in_proj(w_in[i]), seq_len, n_batch)
        xbc = _conv(p_all, conv_w[i], conv_b[i], seq_len, n_lat, ctx_len)
        a_rows = _pack_head_rows(-jnp.exp(a_log[i].astype(jnp.float32)))
        yf, yb = _ssd(xbc, p_all, _pack_head_rows(dt_bias[i]), a_rows, n_batch, seq_len, ctx_len)
        f_all = _fourier_latent(p_all, n_batch, seq_len, tables)
        n_rows = n_lat if last else x_all.shape[0]
        if not last:
            f_all = jnp.concatenate(
                [f_all, _fourier_ctx(p_all, n_batch, seq_len, ctx_len, tables)], axis=0)
        d_cols = jnp.repeat(d_skip[i].astype(jnp.float32), SSM_HEAD_DIM).reshape(1, D_INNER)
        x_mid, u2, u2b = _merge(p_all, xbc, yf, yb, f_all, x_all, mod, d_cols,
                           ssm_norm_w[i].reshape(1, D_INNER), ln1_g[i].reshape(1, d),
                           ln1_b[i].reshape(1, d), w_br_ssm[i].astype(bf),
                           w_br_four[i].astype(bf), w_out[i].astype(bf),
                           n_rows, seq_len, n_batch, alpha)
        x_all = _moe_sublayer(x_mid, u2, u2b, mod, ln2_g[i].reshape(1, d), ln2_b[i].reshape(1, d),
                              router_w[i], router_bias[i], w1[i], w3[i], w2[i],
                              ws1[i], ws3[i], ws2[i], n_rows, seq_len, n_batch, alpha)
    return x_all[:n_lat].reshape(n_batch, seq_len, d)
```

```python
import functools

import jax
import jax.numpy as jnp
import numpy as np
from jax import lax
from jax.experimental import pallas as pl
from jax.experimental.pallas import tpu as pltpu

D_MODEL = 1024
GRID_W = 64
POS_BASE = 10000.0
LN_EPS = 1e-6

SSM_HEADS = 24
SSM_HEAD_DIM = 64
D_INNER = SSM_HEADS * SSM_HEAD_DIM
SSM_GROUPS = 4
HEADS_PER_GROUP = SSM_HEADS // SSM_GROUPS
SSM_STATE = 128
CONV_W = 5
CONV_CH = D_INNER + 2 * SSM_GROUPS * SSM_STATE
CHUNK = 128

FOURIER_GROUPS = 4
FOURIER_GROUP_DIM = 256
FOURIER_W = FOURIER_GROUPS * FOURIER_GROUP_DIM

OFF_XBC = D_INNER
OFF_DT = OFF_XBC + CONV_CH
OFF_FOUR = OFF_DT + 2 * SSM_HEADS
OFF_GATE = OFF_FOUR + FOURIER_W

N_EXPERTS = 64
TOP_K = 8
N_EXPERT_GROUPS = 8
EXPERTS_PER_GROUP = N_EXPERTS // N_EXPERT_GROUPS
TOPK_GROUPS = 4
EXPERT_FF = 256
SHARED_FF = 256
ROUTE_SCALE = 2.5

LANES = 128
VMEM_LIMIT_BYTES = 56 * 1024 * 1024

P_XBC = 0
P_DT = CONV_CH
P_FOUR = P_DT + SSM_GROUPS * LANES
P_GATE = P_FOUR + FOURIER_W
P_Z = P_GATE + 2 * D_MODEL
P_WIDTH = P_Z + D_INNER
DT_BWD_LANE = 32
CUMT_ROWS = 64

K1_TM = 512
K1_TN = 1536
ROW_TILE = 256
MOE_TILE = 512
MOE_BLK = 512
ROW_ALIGN = 16
EXPERT_PAD = LANES
ROW_SPLIT = 64.0
RUN_PIECES = tuple(MOE_TILE >> s for s in range(6))
TAIL_PIECES = tuple(p for p in RUN_PIECES if p < MOE_BLK)
SORTED_ROWS = -(-(MOE_TILE * TOP_K + N_EXPERTS * (ROW_ALIGN - 1)) // MOE_TILE) * MOE_TILE


def _cparams(n_axes=1):
    return pltpu.CompilerParams(
        dimension_semantics=("arbitrary",) * n_axes,
        vmem_limit_bytes=VMEM_LIMIT_BYTES)


def _layer_norm_f32(x):
    mu = jnp.mean(x, axis=-1, keepdims=True)
    xc = x - mu
    var = jnp.mean(xc * xc, axis=-1, keepdims=True)
    return xc * lax.rsqrt(var + LN_EPS)


def _sigmoid(x):
    return 1.0 / (1.0 + jnp.exp(-x))


def _silu(x):
    return x * _sigmoid(x)


def _pos_kernel(x_ref, ctx_ref, er_ref, ec_ref, o_ref, *, n_lat):
    i = pl.program_id(0)
    half = D_MODEL // 2

    @pl.when(i < n_lat)
    def _():
        ec = ec_ref[...]
        for r in range(8):
            rows = slice(GRID_W * r, GRID_W * (r + 1))
            o_ref[rows, :half] = x_ref[rows, :half] + er_ref[r:r + 1, :]
            o_ref[rows, half:] = x_ref[rows, half:] + ec

    @pl.when(i >= n_lat)
    def _():
        o_ref[...] = ctx_ref[...]


def _assemble_stream(x, ctx):
    b, l, d = x.shape
    lc = ctx.shape[1]
    tile = 8 * GRID_W
    n_lat = (b * l) // tile
    n_ctx = (b * lc) // tile
    rows = l // GRID_W
    quarter = D_MODEL // 4
    omega = 1.0 / (POS_BASE ** (jnp.arange(quarter, dtype=jnp.float32) / quarter))
    ang_r = jnp.arange(rows, dtype=jnp.float32)[:, None] * omega
    ang_c = jnp.arange(GRID_W, dtype=jnp.float32)[:, None] * omega
    emb_r = jnp.concatenate([jnp.sin(ang_r), jnp.cos(ang_r)], -1)
    emb_c = jnp.concatenate([jnp.sin(ang_c), jnp.cos(ang_c)], -1)
    tiles_per_seq = l // tile
    return pl.pallas_call(
        functools.partial(_pos_kernel, n_lat=n_lat),
        out_shape=jax.ShapeDtypeStruct((b * l + b * lc, d), jnp.float32),
        grid=(n_lat + n_ctx,),
        in_specs=[
            pl.BlockSpec((tile, d), lambda i: (jnp.minimum(i, n_lat - 1), 0)),
            pl.BlockSpec((tile, d), lambda i: (jnp.maximum(i - n_lat, 0), 0)),
            pl.BlockSpec((8, d // 2), lambda i: (i % tiles_per_seq, 0)),
            pl.BlockSpec((GRID_W, d // 2), lambda i: (0, 0)),
        ],
        out_specs=pl.BlockSpec((tile, d), lambda i: (i, 0)),
        compiler_params=_cparams(1),
        name="assemble_stream",
    )(x.reshape(b * l, d), ctx.reshape(b * lc, d), emb_r, emb_c)


def _ada_kernel(c_ref, w_ref, b_ref, o_ref):
    c = c_ref[...]
    o_ref[...] = jnp.dot(_silu(c), w_ref[...], preferred_element_type=jnp.float32,
                         precision=lax.Precision.HIGHEST) + b_ref[...]


def _ada_mod(c_rows, w_ada, b_ada):
    depth, d, n6 = w_ada.shape
    tn = 1536
    return pl.pallas_call(
        _ada_kernel,
        out_shape=jax.ShapeDtypeStruct((depth, 8, n6), jnp.float32),
        grid=(depth, n6 // tn),
        in_specs=[
            pl.BlockSpec((8, d), lambda a, j: (0, 0)),
            pl.BlockSpec((None, d, tn), lambda a, j: (a, 0, j)),
            pl.BlockSpec((None, 1, tn), lambda a, j: (a, 0, j)),
        ],
        out_specs=pl.BlockSpec((None, 8, tn), lambda a, j: (a, 0, j)),
        compiler_params=_cparams(2),
        name="ada_mod",
    )(c_rows, w_ada, b_ada.reshape(depth, 1, n6))


def _k1_kernel(x_ref, sc_ref, sh_ref, w_ref, o_ref, u_ref, *, tiles_per_seq, n_batch):
    i = pl.program_id(0)

    @pl.when(pl.program_id(1) == 0)
    def _():
        bidx = jnp.minimum(i // tiles_per_seq, n_batch)
        xn = _layer_norm_f32(x_ref[...])
        u = xn * (1.0 + sc_ref[pl.ds(bidx, 1), :]) + sh_ref[pl.ds(bidx, 1), :]
        u_ref[...] = u.astype(jnp.bfloat16)

    o_ref[...] = jnp.dot(u_ref[...], w_ref[...],
                         preferred_element_type=jnp.float32).astype(jnp.bfloat16)


def _in_proj(x_all, mod, w_all, seq_len, n_batch):
    nt, d = x_all.shape
    tm, tn = K1_TM, K1_TN
    return pl.pallas_call(
        functools.partial(_k1_kernel, tiles_per_seq=seq_len // tm, n_batch=n_batch),
        out_shape=jax.ShapeDtypeStruct((nt, P_WIDTH), jnp.bfloat16),
        grid=(nt // tm, P_WIDTH // tn),
        in_specs=[
            pl.BlockSpec((tm, d), lambda i, j: (i, 0)),
            pl.BlockSpec((8, d), lambda i, j: (0, 1)),
            pl.BlockSpec((8, d), lambda i, j: (0, 0)),
            pl.BlockSpec((d, tn), lambda i, j: (0, j)),
        ],
        out_specs=pl.BlockSpec((tm, tn), lambda i, j: (i, j)),
        scratch_shapes=[pltpu.VMEM((tm, d), jnp.bfloat16)],
        compiler_params=_cparams(2),
        name="in_proj",
    )(x_all, mod, mod, w_all)


def _softplus(x):
    return jnp.maximum(x, 0.0) + jnp.log1p(jnp.exp(-jnp.abs(x)))


def _conv_kernel(cur_ref, prev_ref, next_ref, w_ref, b_ref, dtraw_ref, bias_ref, a_ref,
                 o_ref, dt_ref, cum_ref, cumt_ref, bt_ref, ext_ref, *, tiles_per_seq, n_lat):
    i = pl.program_id(0)
    is_ctx = i >= n_lat
    is_start = jnp.logical_or(i % tiles_per_seq == 0, is_ctx)
    is_end = jnp.logical_or(i % tiles_per_seq == tiles_per_seq - 1, is_ctx)
    halo = prev_ref.shape[0]
    rows = cur_ref.shape[0]
    ext_ref[0:halo, :] = jnp.where(is_start, 0.0, prev_ref[...].astype(jnp.float32))
    ext_ref[halo:halo + rows, :] = cur_ref[...].astype(jnp.float32)
    ext_ref[halo + rows:, :] = jnp.where(is_end, 0.0, next_ref[...].astype(jnp.float32))

    def conv_lane_block(c, carry):
        lanes = pl.ds(pl.multiple_of(c * LANES, LANES), LANES)
        acc = b_ref[:, lanes] + w_ref[0:1, lanes] * ext_ref[pl.ds(halo - 2, rows), lanes]
        for k in range(1, CONV_W):
            acc = acc + w_ref[k:k + 1, lanes] * ext_ref[pl.ds(halo - 2 + k, rows), lanes]
        o_ref[:, lanes] = _silu(acc).astype(jnp.bfloat16)
        return carry

    lax.fori_loop(0, CONV_CH // LANES, conv_lane_block, 0)

    q = CHUNK
    dt = _softplus(dtraw_ref[...].astype(jnp.float32) + bias_ref[...])
    dt_ref[...] = dt
    adt = dt * a_ref[...]
    row_i = lax.broadcasted_iota(jnp.int32, (q, q), 0)
    col_i = lax.broadcasted_iota(jnp.int32, (q, q), 1)
    lower = (row_i >= col_i).astype(jnp.float32)
    upper = (row_i <= col_i).astype(jnp.float32)
    is_fwd = lax.broadcasted_iota(jnp.int32, (1, LANES), 1) < DT_BWD_LANE
    for ch in range(rows // q):
        rs = slice(q * ch, q * (ch + 1))
        cum_f = jnp.dot(lower, adt[rs, :], preferred_element_type=jnp.float32,
                        precision=lax.Precision.HIGHEST)
        cum_b = jnp.dot(upper, adt[rs, :], preferred_element_type=jnp.float32,
                        precision=lax.Precision.HIGHEST)
        cum = jnp.where(is_fwd, cum_f, cum_b)
        cum_ref[rs, :] = cum
        cumt_ref[ch] = cum.T[:CUMT_ROWS, :]
        for g in range(SSM_GROUPS):
            b_cols = slice(D_INNER + SSM_STATE * g, D_INNER + SSM_STATE * (g + 1))
            bt_ref[ch, SSM_STATE * g:SSM_STATE * (g + 1), :] = (
                o_ref[rs, b_cols].astype(jnp.float32).T.astype(jnp.bfloat16))


def _conv(p_all, conv_w, conv_b, dt_bias_row, a_row, seq_len, n_lat_rows, ctx_len):
    nt = p_all.shape[0]
    tl = ROW_TILE
    assert ctx_len == tl, "context sequences must span exactly one conv tile"
    halo = 16
    hb = tl // halo
    n_halo_blocks = nt // halo
    dtw = LANES
    cpt = tl // CHUNK
    w8 = jnp.zeros((8, CONV_CH), jnp.float32).at[:CONV_W].set(conv_w)
    return pl.pallas_call(
        functools.partial(_conv_kernel, tiles_per_seq=seq_len // tl, n_lat=n_lat_rows // tl),
        out_shape=(jax.ShapeDtypeStruct((nt, CONV_CH), jnp.bfloat16),
                   jax.ShapeDtypeStruct((nt, dtw), jnp.float32),
                   jax.ShapeDtypeStruct((nt, dtw), jnp.float32),
                   jax.ShapeDtypeStruct((nt // CHUNK, CUMT_ROWS, CHUNK), jnp.float32),
                   jax.ShapeDtypeStruct((nt // CHUNK, SSM_GROUPS * SSM_STATE, CHUNK), jnp.bfloat16)),
        grid=(nt // tl,),
        in_specs=[
            pl.BlockSpec((tl, CONV_CH), lambda i: (i, 0)),
            pl.BlockSpec((halo, CONV_CH), lambda i: (jnp.maximum(i * hb - 1, 0), 0)),
            pl.BlockSpec((halo, CONV_CH),
                         lambda i: (jnp.minimum((i + 1) * hb, n_halo_blocks - 1), 0)),
            pl.BlockSpec((8, CONV_CH), lambda i: (0, 0)),
            pl.BlockSpec((1, CONV_CH), lambda i: (0, 0)),
            pl.BlockSpec((tl, dtw), lambda i: (i, P_DT // dtw)),
            pl.BlockSpec((1, dtw), lambda i: (0, 0)),
            pl.BlockSpec((1, dtw), lambda i: (0, 0)),
        ],
        out_specs=(pl.BlockSpec((tl, CONV_CH), lambda i: (i, 0)),
                   pl.BlockSpec((tl, dtw), lambda i: (i, 0)),
                   pl.BlockSpec((tl, dtw), lambda i: (i, 0)),
                   pl.BlockSpec((cpt, CUMT_ROWS, CHUNK), lambda i: (i, 0, 0)),
                   pl.BlockSpec((cpt, SSM_GROUPS * SSM_STATE, CHUNK), lambda i: (i, 0, 0))),
        scratch_shapes=[pltpu.VMEM((tl + 2 * halo, CONV_CH), jnp.float32)],
        compiler_params=_cparams(1),
        name="dwconv_silu",
    )(p_all, p_all, p_all, w8, conv_b.reshape(1, CONV_CH), p_all, dt_bias_row, a_row)


def _ssd_direction(refs, s_ref, y_ref, gi, *, lane0, forward):
    x_ref, b_ref, c_ref, dt_ref, cum_ref, cumt_ref, bt_ref = refs
    q = CHUNK
    gw = HEADS_PER_GROUP * SSM_HEAD_DIM
    row_i = lax.broadcasted_iota(jnp.int32, (q, q), 0)
    col_i = lax.broadcasted_iota(jnp.int32, (q, q), 1)
    tri = (row_i >= col_i) if forward else (row_i <= col_i)
    lane_lo = lax.broadcasted_iota(jnp.int32, (q, LANES), 1) < SSM_HEAD_DIM
    lane_lo_row = lax.broadcasted_iota(jnp.int32, (1, LANES), 1) < SSM_HEAD_DIM

    g_lanes = slice(LANES * gi, LANES * (gi + 1))
    dt = dt_ref[...]
    cum = cum_ref[...]
    cum_t = cumt_ref[...]
    total = cum[q - 1:q, :] if forward else cum[0:1, :]

    cm = c_ref[:, g_lanes]
    bm = b_ref[:, g_lanes]
    scores = lax.dot_general(cm, bm, (((1,), (1,)), ((), ())),
                             preferred_element_type=jnp.float32)
    bm_t = bt_ref[SSM_STATE * gi:SSM_STATE * (gi + 1), :]

    for pr in range(HEADS_PER_GROUP // 2):
        r0 = lane0 + 2 * pr
        r1 = r0 + 1
        lanes = slice(gw * gi + LANES * pr, gw * gi + LANES * (pr + 1))
        xp = x_ref[:, lanes].astype(jnp.float32)
        dt_pair = jnp.where(lane_lo, dt[:, r0:r0 + 1], dt[:, r1:r1 + 1])
        cum_pair = jnp.where(lane_lo, cum[:, r0:r0 + 1], cum[:, r1:r1 + 1])
        tot_pair = jnp.where(lane_lo_row, total[:, r0:r0 + 1], total[:, r1:r1 + 1])
        xdt = xp * dt_pair
        l0 = jnp.exp(jnp.where(tri, cum[:, r0:r0 + 1] - cum_t[r0:r0 + 1, :], -jnp.inf))
        l1 = jnp.exp(jnp.where(tri, cum[:, r1:r1 + 1] - cum_t[r1:r1 + 1, :], -jnp.inf))
        w = jnp.concatenate([(scores * l0).astype(jnp.bfloat16),
                             (scores * l1).astype(jnp.bfloat16)], axis=1)
        xdt_b = xdt.astype(jnp.bfloat16)
        zero = jnp.zeros_like(xdt_b)
        rhs = jnp.concatenate([jnp.where(lane_lo, xdt_b, zero),
                               jnp.where(lane_lo, zero, xdt_b)], axis=0)
        y_diag = jnp.dot(w, rhs, preferred_element_type=jnp.float32)
        s_old = s_ref[:, lanes]
        y_off = jnp.dot(cm, s_old.astype(jnp.bfloat16),
                        preferred_element_type=jnp.float32) * jnp.exp(cum_pair)
        y_ref[:, lanes] = (y_diag + y_off).astype(jnp.bfloat16)
        decayed = (xdt * jnp.exp(tot_pair - cum_pair)).astype(jnp.bfloat16)
        s_ref[:, lanes] = jnp.exp(tot_pair) * s_old + jnp.dot(
            bm_t, decayed, preferred_element_type=jnp.float32)


def _ssd_kernel(*refs):
    n_in = 7
    fwd_refs, bwd_refs = refs[:n_in], refs[n_in:2 * n_in]
    yf_ref, yb_ref, sf_ref, sb_ref = refs[2 * n_in:]

    @pl.when(pl.program_id(2) == 0)
    def _():
        sf_ref[...] = jnp.zeros_like(sf_ref)
        sb_ref[...] = jnp.zeros_like(sb_ref)

    for gi in range(SSM_GROUPS):
        _ssd_direction(fwd_refs, sf_ref, yf_ref, gi, lane0=HEADS_PER_GROUP * gi, forward=True)
        _ssd_direction(bwd_refs, sb_ref, yb_ref, gi,
                       lane0=DT_BWD_LANE + HEADS_PER_GROUP * gi, forward=False)


def _ssd(xbc, dt_all, cum_all, cumt_all, bt_all, n_batch, seq_len, ctx_len):
    nt = xbc.shape[0]
    q = CHUNK
    nc_lat = seq_len // q
    nc_ctx = ctx_len // q
    ctx_blk0 = (n_batch * seq_len) // q
    n_steps = nc_ctx + nc_lat
    gw = D_INNER
    sw = SSM_GROUPS * SSM_STATE
    b_blk0 = D_INNER // sw
    c_blk0 = b_blk0 + 1

    def fwd_row(b, j):
        return jnp.where(j < nc_ctx, ctx_blk0 + nc_ctx * b + j, nc_lat * b + (j - nc_ctx))

    def bwd_row(b, j):
        return jnp.where(j < nc_ctx, ctx_blk0 + nc_ctx * b + (nc_ctx - 1 - j),
                         nc_lat * b + (n_steps - 1 - j))

    def specs(row):
        return [
            pl.BlockSpec((q, gw), lambda b, g, j: (row(b, j), g)),
            pl.BlockSpec((q, sw), lambda b, g, j: (row(b, j), b_blk0 + g)),
            pl.BlockSpec((q, sw), lambda b, g, j: (row(b, j), c_blk0 + g)),
            pl.BlockSpec((q, LANES), lambda b, g, j: (row(b, j), 0)),
            pl.BlockSpec((q, LANES), lambda b, g, j: (row(b, j), 0)),
            pl.BlockSpec((None, CUMT_ROWS, q), lambda b, g, j: (row(b, j), 0, 0)),
            pl.BlockSpec((None, sw, q), lambda b, g, j: (row(b, j), 0, 0)),
        ]

    out_sds = jax.ShapeDtypeStruct((nt, D_INNER), jnp.bfloat16)
    operands = (xbc, xbc, xbc, dt_all, cum_all, cumt_all, bt_all)
    return pl.pallas_call(
        _ssd_kernel,
        out_shape=(out_sds, out_sds),
        grid=(n_batch, 1, n_steps),
        in_specs=specs(fwd_row) + specs(bwd_row),
        out_specs=(pl.BlockSpec((q, gw), lambda b, g, j: (fwd_row(b, j), g)),
                   pl.BlockSpec((q, gw), lambda b, g, j: (bwd_row(b, j), g))),
        scratch_shapes=[pltpu.VMEM((SSM_STATE, gw), jnp.float32),
                        pltpu.VMEM((SSM_STATE, gw), jnp.float32)],
        compiler_params=_cparams(3),
        name="ssd_scan",
    )(*operands, *operands)


def _dft_tables(seq_len):
    l1n = seq_len // LANES
    two_pi = 2.0 * np.pi
    gd = FOURIER_GROUP_DIM
    jj = jnp.arange(gd, dtype=jnp.int32)
    ang_c = ((jj[:, None] * jj[None, :]) % gd).astype(jnp.float32) * (two_pi / gd)
    cc, sc = jnp.cos(ang_c), jnp.sin(ang_c)
    k1 = jnp.arange(l1n, dtype=jnp.int32)
    ang1 = ((k1[:, None] * k1[None, :]) % l1n).astype(jnp.float32) * (two_pi / l1n)
    w1 = jnp.concatenate([jnp.cos(ang1), -jnp.sin(ang1)], axis=0)
    k2 = jnp.arange(LANES, dtype=jnp.int32)
    kk = k1[:, None, None] + l1n * k2[None, :, None]
    ang2 = ((kk * k2[None, None, :]) % seq_len).astype(jnp.float32) * (two_pi / seq_len)
    er, ei = jnp.cos(ang2), -jnp.sin(ang2)
    e = jnp.concatenate([jnp.concatenate([er, -ei], axis=2),
                         jnp.concatenate([ei, er], axis=2)], axis=1)
    return cc, sc, w1, e


def _dft1_kernel(w_ref, x_ref, o_ref):
    o_ref[...] = jnp.dot(w_ref[...], x_ref[...],
                         preferred_element_type=jnp.float32).astype(jnp.bfloat16)


def _channel_mix(gr, gi, cs_ref, o_ref):
    gd = FOURIER_GROUP_DIM
    for g in range(FOURIER_GROUPS):
        cols = slice(gd * g, gd * (g + 1))
        lhs = jnp.concatenate([gr[:, cols], gi[:, cols]], axis=1).astype(jnp.bfloat16)
        o_ref[:, cols] = jnp.dot(lhs, cs_ref[...],
                                 preferred_element_type=jnp.float32).astype(jnp.bfloat16)


def _dft2_kernel(e_ref, ar_ref, ai_ref, cs_ref, o_ref):
    a = jnp.concatenate([ar_ref[...], ai_ref[...]], axis=0)
    g = jnp.dot(e_ref[...], a, preferred_element_type=jnp.float32)
    half = g.shape[0] // 2
    _channel_mix(g[:half], g[half:], cs_ref, o_ref)


def _dft_ctx_kernel(w_ref, x_ref, cs_ref, o_ref):
    g = jnp.dot(w_ref[...], x_ref[...], preferred_element_type=jnp.float32)
    half = g.shape[0] // 2
    _channel_mix(g[:half], g[half:], cs_ref, o_ref)


def _fourier_latent(p_all, n_batch, seq_len, tables):
    cc, sc, w1, e = tables
    c = FOURIER_W
    l1n = seq_len // LANES
    ncol = LANES * c
    four = p_all[:n_batch * seq_len, P_FOUR:P_FOUR + c].reshape(n_batch, l1n, ncol)
    tn = 4096
    a = pl.pallas_call(
        _dft1_kernel,
        out_shape=jax.ShapeDtypeStruct((n_batch, 2 * l1n, ncol), jnp.bfloat16),
        grid=(n_batch, ncol // tn),
        in_specs=[pl.BlockSpec((2 * l1n, l1n), lambda b, j: (0, 0)),
                  pl.BlockSpec((None, l1n, tn), lambda b, j: (b, 0, j))],
        out_specs=pl.BlockSpec((None, 2 * l1n, tn), lambda b, j: (b, 0, j)),
        compiler_params=_cparams(2),
        name="dft_stage1",
    )(w1.astype(jnp.bfloat16), four)
    a4 = a.reshape(n_batch, 2 * l1n, LANES, c)
    norm = 1.0 / np.sqrt(float(seq_len) * FOURIER_GROUP_DIM)
    cs = (jnp.concatenate([cc, sc], axis=0) * norm).astype(jnp.bfloat16)
    o = pl.pallas_call(
        _dft2_kernel,
        out_shape=jax.ShapeDtypeStruct((n_batch, l1n, LANES, c), jnp.bfloat16),
        grid=(n_batch, l1n),
        in_specs=[pl.BlockSpec((None, 2 * LANES, 2 * LANES), lambda b, k: (k, 0, 0)),
                  pl.BlockSpec((None, None, LANES, c), lambda b, k: (b, k, 0, 0)),
                  pl.BlockSpec((None, None, LANES, c), lambda b, k: (b, l1n + k, 0, 0)),
                  pl.BlockSpec((2 * FOURIER_GROUP_DIM, FOURIER_GROUP_DIM), lambda b, k: (0, 0))],
        out_specs=pl.BlockSpec((None, None, LANES, c), lambda b, k: (b, k, 0, 0)),
        compiler_params=_cparams(2),
        name="dft_stage2",
    )(e.astype(jnp.bfloat16), a4, a4, cs)
    return o.transpose(0, 2, 1, 3).reshape(n_batch * seq_len, c)


def _fourier_ctx(p_all, n_batch, seq_len, ctx_len, tables):
    cc, sc, _, _ = tables
    assert ctx_len == FOURIER_GROUP_DIM
    c = FOURIER_W
    wc = jnp.concatenate([cc, -sc], axis=0).astype(jnp.bfloat16)
    norm = 1.0 / np.sqrt(float(ctx_len) * FOURIER_GROUP_DIM)
    cs = (jnp.concatenate([cc, sc], axis=0) * norm).astype(jnp.bfloat16)
    blk0 = (n_batch * seq_len) // ctx_len
    return pl.pallas_call(
        _dft_ctx_kernel,
        out_shape=jax.ShapeDtypeStruct((n_batch * ctx_len, c), jnp.bfloat16),
        grid=(n_batch,),
        in_specs=[pl.BlockSpec((2 * ctx_len, ctx_len), lambda b: (0, 0)),
                  pl.BlockSpec((ctx_len, c), lambda b: (blk0 + b, P_FOUR // c)),
                  pl.BlockSpec((2 * FOURIER_GROUP_DIM, FOURIER_GROUP_DIM), lambda b: (0, 0))],
        out_specs=pl.BlockSpec((ctx_len, c), lambda b: (b, 0)),
        compiler_params=_cparams(1),
        name="dft_ctx",
    )(wc, p_all, cs)


def _merge_kernel(gate_ref, z_ref, xs_ref, yf_ref, yb_ref, f_ref, x_ref,
                  g1_ref, sc2_ref, sh2_ref, d_ref, nw_ref, lng_ref, lnb_ref,
                  wssm_ref, wfour_ref, wout_ref, xo_ref, u2_ref, u2b_ref,
                  *, tiles_per_seq, n_batch, alpha):
    bidx = jnp.minimum(pl.program_id(0) // tiles_per_seq, n_batch)
    y = (yf_ref[...].astype(jnp.float32) + yb_ref[...].astype(jnp.float32)
         + xs_ref[...].astype(jnp.float32) * d_ref[...])
    h = y * _silu(z_ref[...].astype(jnp.float32))
    h = h * lax.rsqrt(jnp.mean(h * h, axis=-1, keepdims=True) + LN_EPS) * nw_ref[...]
    ssm = jnp.dot(h.astype(jnp.bfloat16), wssm_ref[...], preferred_element_type=jnp.float32)
    four = jnp.dot(f_ref[...], wfour_ref[...], preferred_element_type=jnp.float32)
    gates = _sigmoid(gate_ref[...].astype(jnp.float32))
    merged = gates[:, :D_MODEL] * ssm + gates[:, D_MODEL:] * four
    out = jnp.dot(merged.astype(jnp.bfloat16), wout_ref[...], preferred_element_type=jnp.float32)
    r = alpha * x_ref[...] + g1_ref[pl.ds(bidx, 1), :] * out
    xn = _layer_norm_f32(r) * lng_ref[...] + lnb_ref[...]
    xo_ref[...] = xn
    u2 = (_layer_norm_f32(xn) * (1.0 + sc2_ref[pl.ds(bidx, 1), :])
          + sh2_ref[pl.ds(bidx, 1), :])
    u2_ref[...] = u2
    u2b_ref[...] = u2.astype(jnp.bfloat16)


def _merge(p_all, xbc, yf, yb, f_all, x_all, mod, d_cols, norm_w, ln_g, ln_b,
           w_ssm, w_four, w_out, n_rows, seq_len, n_batch, alpha):
    tm = ROW_TILE
    d = D_MODEL
    row = lambda i: (i, 0)
    const = lambda i: (0, 0)
    out_sds = jax.ShapeDtypeStruct((n_rows, d), jnp.float32)
    return pl.pallas_call(
        functools.partial(_merge_kernel, tiles_per_seq=seq_len // tm, n_batch=n_batch, alpha=alpha),
        out_shape=(out_sds, out_sds, jax.ShapeDtypeStruct((n_rows, d), jnp.bfloat16)),
        grid=(n_rows // tm,),
        in_specs=[
            pl.BlockSpec((tm, 2 * d), lambda i: (i, P_GATE // (2 * d))),
            pl.BlockSpec((tm, D_INNER), lambda i: (i, P_Z // D_INNER)),
            pl.BlockSpec((tm, D_INNER), row),
            pl.BlockSpec((tm, D_INNER), row),
            pl.BlockSpec((tm, D_INNER), row),
            pl.BlockSpec((tm, FOURIER_W), row),
            pl.BlockSpec((tm, d), row),
            pl.BlockSpec((8, d), lambda i: (0, 2)),
            pl.BlockSpec((8, d), lambda i: (0, 4)),
            pl.BlockSpec((8, d), lambda i: (0, 3)),
            pl.BlockSpec((1, D_INNER), const),
            pl.BlockSpec((1, D_INNER), const),
            pl.BlockSpec((1, d), const),
            pl.BlockSpec((1, d), const),
            pl.BlockSpec((D_INNER, d), const),
            pl.BlockSpec((FOURIER_W, d), const),
            pl.BlockSpec((d, d), const),
        ],
        out_specs=(pl.BlockSpec((tm, d), row), pl.BlockSpec((tm, d), row),
                   pl.BlockSpec((tm, d), row)),
        compiler_params=_cparams(1),
        name="merge_postnorm",
    )(p_all, p_all, xbc, yf, yb, f_all, x_all, mod, mod, mod, d_cols, norm_w,
      ln_g, ln_b, w_ssm, w_four, w_out)


def _first_index_of_max(v, iota, big):
    m = jnp.max(v, axis=0, keepdims=True)
    idx = jnp.min(jnp.where(v == m, iota, big), axis=0, keepdims=True)
    return m, idx


def _router_kernel(u_ref, wt_ref, bias_ref, tri_ref, row_ref, rowt_ref, gatet_ref, cnt_ref,
                   off_ref, basetab_ref, tot_ref, base_ref):
    @pl.when(pl.program_id(0) == 0)
    def _():
        base_ref[...] = jnp.zeros_like(base_ref)

    tm = u_ref.shape[0]
    ne, epg = N_EXPERTS, EXPERTS_PER_GROUP
    neg = -jnp.inf
    logits = lax.dot_general(wt_ref[...], u_ref[...], (((1,), (1,)), ((), ())),
                             preferred_element_type=jnp.float32,
                             precision=lax.Precision.HIGHEST)
    scores = _sigmoid(logits)
    sel = scores + bias_ref[...]
    iota_g = lax.broadcasted_iota(jnp.int32, (epg, tm), 0).astype(jnp.float32)
    grp_rows = []
    for g in range(N_EXPERT_GROUPS):
        v = sel[epg * g:epg * (g + 1), :]
        m1, i1 = _first_index_of_max(v, iota_g, epg)
        m2 = jnp.max(jnp.where(iota_g == i1, neg, v), axis=0, keepdims=True)
        grp_rows.append(m1 + m2)
    grp = jnp.concatenate(grp_rows, axis=0)
    iota_n = lax.broadcasted_iota(jnp.int32, (N_EXPERT_GROUPS, tm), 0).astype(jnp.float32)
    chosen = jnp.zeros((N_EXPERT_GROUPS, tm), jnp.float32)
    for _ in range(TOPK_GROUPS):
        _, gi = _first_index_of_max(grp, iota_n, N_EXPERT_GROUPS)
        hit = iota_n == gi
        chosen = jnp.where(hit, 1.0, chosen)
        grp = jnp.where(hit, neg, grp)
    masked = jnp.concatenate(
        [jnp.where(chosen[g:g + 1, :] > 0.0, sel[epg * g:epg * (g + 1), :], neg)
         for g in range(N_EXPERT_GROUPS)], axis=0)
    iota_e = lax.broadcasted_iota(jnp.int32, (ne, tm), 0).astype(jnp.float32)
    picked = jnp.zeros((ne, tm), jnp.float32)
    for _ in range(TOP_K):
        _, ei = _first_index_of_max(masked, iota_e, ne)
        hit = iota_e == ei
        picked = jnp.where(hit, 1.0, picked)
        masked = jnp.where(hit, neg, masked)
    chosen_scores = picked * scores
    gate = chosen_scores / jnp.sum(chosen_scores, axis=0, keepdims=True) * ROUTE_SCALE
    before = jnp.dot(picked.astype(jnp.bfloat16), tri_ref[...],
                     preferred_element_type=jnp.float32)
    cnt = jnp.sum(picked, axis=1, keepdims=True)
    cnt16 = jnp.ceil(cnt * (1.0 / ROW_ALIGN)) * ROW_ALIGN
    cnt16_b = jnp.broadcast_to(cnt16, (ne, LANES))
    e_row = lax.broadcasted_iota(jnp.int32, (ne, ne), 0)
    e_col = lax.broadcasted_iota(jnp.int32, (ne, ne), 1)
    off_b = jnp.dot((e_col < e_row).astype(jnp.float32), cnt16_b,
                    preferred_element_type=jnp.float32,
                    precision=lax.Precision.HIGHEST)
    row1 = picked * (before + off_b[:, 0:1] + 1.0)
    pad = jnp.zeros((EXPERT_PAD - ne, tm), jnp.float32)
    row1_p = jnp.concatenate([row1, pad], axis=0)
    gate_p = jnp.concatenate([gate, pad], axis=0)
    pad_b = jnp.zeros((EXPERT_PAD - ne, LANES), jnp.float32)
    row_ref[...] = row1_p
    rowt_ref[...] = row1_p.T
    gatet_ref[...] = gate_p.T.astype(jnp.bfloat16)
    cnt_ref[...] = jnp.concatenate([cnt16_b, pad_b], axis=0)
    off_ref[...] = jnp.concatenate([off_b, pad_b], axis=0)
    basetab_ref[...] = jnp.broadcast_to(base_ref[...], (ne, LANES))
    base_ref[...] = base_ref[...] + cnt16
    tot_ref[...] = base_ref[...]


def _router(u2, router_w, router_bias, n_tok):
    tm = MOE_TILE
    n_tiles = n_tok // tm
    tri = jnp.triu(jnp.ones((tm, tm), jnp.float32), k=1).astype(jnp.bfloat16)
    tab = lambda rows: (jax.ShapeDtypeStruct((n_tiles, rows, LANES), jnp.float32),
                        pl.BlockSpec((None, rows, LANES), lambda i: (i, 0, 0)))
    outs = [
        (jax.ShapeDtypeStruct((n_tiles, EXPERT_PAD, tm), jnp.float32),
         pl.BlockSpec((None, EXPERT_PAD, tm), lambda i: (i, 0, 0))),
        (jax.ShapeDtypeStruct((n_tok, EXPERT_PAD), jnp.float32),
         pl.BlockSpec((tm, EXPERT_PAD), lambda i: (i, 0))),
        (jax.ShapeDtypeStruct((n_tok, EXPERT_PAD), jnp.bfloat16),
         pl.BlockSpec((tm, EXPERT_PAD), lambda i: (i, 0))),
        tab(EXPERT_PAD), tab(EXPERT_PAD), tab(N_EXPERTS),
        (jax.ShapeDtypeStruct((N_EXPERTS, 1), jnp.float32),
         pl.BlockSpec((N_EXPERTS, 1), lambda i: (0, 0))),
    ]
    return pl.pallas_call(
        _router_kernel,
        out_shape=tuple(o[0] for o in outs),
        grid=(n_tiles,),
        in_specs=[pl.BlockSpec((tm, D_MODEL), lambda i: (i, 0)),
                  pl.BlockSpec((N_EXPERTS, D_MODEL), lambda i: (0, 0)),
                  pl.BlockSpec((N_EXPERTS, 1), lambda i: (0, 0)),
                  pl.BlockSpec((tm, tm), lambda i: (0, 0))],
        out_specs=tuple(o[1] for o in outs),
        scratch_shapes=[pltpu.VMEM((N_EXPERTS, 1), jnp.float32)],
        compiler_params=_cparams(1),
        name="moe_router",
    )(u2, router_w.T, router_bias.reshape(N_EXPERTS, 1), tri)


def _run_copies(cnt, make_copy, pieces):
    done = jnp.int32(0)
    for piece in pieces:
        hit = (cnt & piece) != 0

        @pl.when(hit)
        def _(done=done, piece=piece):
            make_copy(done, piece)

        done = done + (cnt & piece)


def _for_each_run(tile, cnt_ref, off_ref, dst_ref, make_copy, start):
    def body(e, carry):
        idx = tile * N_EXPERTS + e
        off = pl.multiple_of(off_ref[idx], ROW_ALIGN)
        dst = pl.multiple_of(dst_ref[idx], ROW_ALIGN)

        def piece_copy(done, piece):
            cp = make_copy(pl.multiple_of(off + done, ROW_ALIGN),
                           pl.multiple_of(dst + done, ROW_ALIGN), piece)
            if start:
                cp.start()
            else:
                cp.wait()

        _run_copies(cnt_ref[idx], piece_copy, RUN_PIECES)
        return carry

    lax.fori_loop(0, N_EXPERTS, body, 0)


def _split_rows(row1):
    hi = jnp.floor(row1 * (1.0 / ROW_SPLIT))
    lo = row1 - hi * ROW_SPLIT
    return hi.astype(jnp.bfloat16), lo.astype(jnp.bfloat16)


def _dispatch_kernel(cnt_ref, off_ref, dst_ref, tail_ref, nblk_ref, used_ref, u_ref, row_ref,
                     bounds_ref, xs_hbm, sorted_ref, zero_ref, sem):
    i = pl.program_id(0)
    tm = u_ref.shape[0]
    n_chunks = sorted_ref.shape[0] // tm
    digits = jnp.concatenate(_split_rows(row_ref[...]), axis=0)
    lower = bounds_ref[0:1, :]
    upper = bounds_ref[1:2, :]
    row_e = lax.broadcasted_iota(jnp.int32, (tm, EXPERT_PAD), 0).astype(jnp.float32)
    row_t = lax.broadcasted_iota(jnp.int32, (tm, tm), 0).astype(jnp.float32)

    def sort_chunk(j):
        r0 = float(j * tm)
        in_run = jnp.logical_and(row_e + r0 >= lower, row_e + r0 < upper)
        owner = jnp.concatenate([jnp.where(in_run, ROW_SPLIT, 0.0),
                                 jnp.where(in_run, 1.0, 0.0)], axis=1).astype(jnp.bfloat16)
        want = jnp.dot(owner, digits, preferred_element_type=jnp.float32)
        perm = jnp.where(want == row_t + (r0 + 1.0), 1.0, 0.0).astype(jnp.bfloat16)
        sorted_ref[j * tm:(j + 1) * tm, :] = jnp.dot(
            perm, u_ref[...], preferred_element_type=jnp.float32).astype(jnp.bfloat16)

    for j in range(n_chunks):
        if (j + 1) * tm <= TOP_K * tm:
            sort_chunk(j)
        else:
            pl.when(used_ref[i] > j * tm)(functools.partial(sort_chunk, j))

    def run_copy(off, dst, piece):
        return pltpu.make_async_copy(sorted_ref.at[pl.ds(off, piece), :],
                                     xs_hbm.at[pl.ds(dst, piece), :], sem)

    _for_each_run(i, cnt_ref, off_ref, dst_ref, run_copy, start=True)
    _for_each_run(i, cnt_ref, off_ref, dst_ref, run_copy, start=False)

    @pl.when(i == pl.num_programs(0) - 1)
    def _():
        zero_ref[...] = jnp.zeros_like(zero_ref)
        blk = zero_ref.shape[0]
        n_blk_total = xs_hbm.shape[0] // blk

        def zero_copy(dst, piece):
            return pltpu.make_async_copy(zero_ref.at[pl.ds(0, piece), :],
                                         xs_hbm.at[pl.ds(dst, piece), :], sem)

        def tail_body(start):
            def body(e, carry):
                dst = pl.multiple_of(tail_ref[e], ROW_ALIGN)

                def piece_copy(done, piece):
                    cp = zero_copy(pl.multiple_of(dst + done, ROW_ALIGN), piece)
                    if start:
                        cp.start()
                    else:
                        cp.wait()

                _run_copies(tail_ref[N_EXPERTS + e], piece_copy, TAIL_PIECES)
                return carry
            return body

        def blk_body(start):
            def body(b, carry):
                cp = zero_copy(pl.multiple_of(b * blk, blk), blk)
                if start:
                    cp.start()
                else:
                    cp.wait()
                return carry
            return body

        lax.fori_loop(0, N_EXPERTS, tail_body(True), 0)
        lax.fori_loop(nblk_ref[0], n_blk_total, blk_body(True), 0)
        lax.fori_loop(0, N_EXPERTS, tail_body(False), 0)
        lax.fori_loop(nblk_ref[0], n_blk_total, blk_body(False), 0)


def _dispatch(u2b, row1, bounds, cnt, off, dst, tail, n_used_blk, used, n_blk):
    tm = MOE_TILE
    d = D_MODEL
    n_tok = u2b.shape[0]
    grid_spec = pltpu.PrefetchScalarGridSpec(
        num_scalar_prefetch=6,
        grid=(n_tok // tm,),
        in_specs=[pl.BlockSpec((tm, d), lambda i, *_: (i, 0)),
                  pl.BlockSpec((None, EXPERT_PAD, tm), lambda i, *_: (i, 0, 0)),
                  pl.BlockSpec((None, 2, EXPERT_PAD), lambda i, *_: (i, 0, 0))],
        out_specs=pl.BlockSpec(memory_space=pl.ANY),
        scratch_shapes=[pltpu.VMEM((SORTED_ROWS, d), jnp.bfloat16),
                        pltpu.VMEM((MOE_BLK, d), jnp.bfloat16),
                        pltpu.SemaphoreType.DMA],
    )
    return pl.pallas_call(
        _dispatch_kernel,
        out_shape=jax.ShapeDtypeStruct((n_blk * MOE_BLK, d), jnp.bfloat16),
        grid_spec=grid_spec,
        compiler_params=_cparams(1),
        name="moe_dispatch",
    )(cnt, off, dst, tail, n_used_blk, used, u2b, row1, bounds)


def _expert_kernel(blk_e_ref, blk_valid_ref, x_ref, w1_ref, w3_ref, w2_ref, y_ref):
    del blk_e_ref
    i = pl.program_id(0)

    @pl.when(blk_valid_ref[i] > 0)
    def _():
        x = x_ref[...]
        h1 = jnp.dot(x, w1_ref[...], preferred_element_type=jnp.float32)
        h3 = jnp.dot(x, w3_ref[...], preferred_element_type=jnp.float32)
        hb = (_silu(h1) * h3).astype(jnp.bfloat16)
        y_ref[...] = jnp.dot(hb, w2_ref[...],
                             preferred_element_type=jnp.float32).astype(jnp.bfloat16)

    @pl.when(blk_valid_ref[i] == 0)
    def _():
        y_ref[...] = jnp.zeros_like(y_ref)


def _experts(xs, blk_e, blk_valid, w1, w3, w2, n_blk):
    blk = MOE_BLK
    d = D_MODEL
    grid_spec = pltpu.PrefetchScalarGridSpec(
        num_scalar_prefetch=2,
        grid=(n_blk,),
        in_specs=[
            pl.BlockSpec((blk, d), lambda i, be, bv: (i, 0)),
            pl.BlockSpec((None, d, EXPERT_FF), lambda i, be, bv: (be[i], 0, 0)),
            pl.BlockSpec((None, d, EXPERT_FF), lambda i, be, bv: (be[i], 0, 0)),
            pl.BlockSpec((None, EXPERT_FF, d), lambda i, be, bv: (be[i], 0, 0)),
        ],
        out_specs=pl.BlockSpec((blk, d), lambda i, be, bv: (i, 0)),
    )
    return pl.pallas_call(
        _expert_kernel,
        out_shape=jax.ShapeDtypeStruct((n_blk * blk, d), jnp.bfloat16),
        grid_spec=grid_spec,
        compiler_params=_cparams(1),
        name="moe_experts",
    )(blk_e, blk_valid, xs, w1, w3, w2)


def _final_kernel(cnt_ref, off_ref, dst_ref, used_ref, rowt_ref, gatet_ref, cnt_tab_ref,
                  off_tab_ref, y_hbm, x_ref, u_ref, g2_ref, lng_ref, lnb_ref, ws1_ref, ws3_ref,
                  ws2_ref, o_ref, sorted_ref, acc_ref, sem, *, tiles_per_seq, n_batch, alpha):
    i = pl.program_id(0)
    bidx = jnp.minimum(i // tiles_per_seq, n_batch)
    tm = x_ref.shape[0]
    n_chunks = sorted_ref.shape[0] // tm

    @pl.when(i == 0)
    def _():
        sorted_ref[...] = jnp.zeros_like(sorted_ref)

    def run_copy(off, dst, piece):
        return pltpu.make_async_copy(y_hbm.at[pl.ds(dst, piece), :],
                                     sorted_ref.at[pl.ds(off, piece), :], sem)

    _for_each_run(i, cnt_ref, off_ref, dst_ref, run_copy, start=True)
    _for_each_run(i, cnt_ref, off_ref, dst_ref, run_copy, start=False)

    digits = jnp.concatenate(_split_rows(rowt_ref[...]), axis=1)
    gate = gatet_ref[...]
    lower = off_tab_ref[:, 0:1]
    upper = lower + cnt_tab_ref[:, 0:1]
    lane_e = lax.broadcasted_iota(jnp.int32, (EXPERT_PAD, tm), 1).astype(jnp.float32)
    lane_t = lax.broadcasted_iota(jnp.int32, (tm, tm), 1).astype(jnp.float32)

    def chunk_sum(j):
        r0 = float(j * tm)
        in_run = jnp.logical_and(lane_e + r0 >= lower, lane_e + r0 < upper)
        owner = jnp.where(in_run, 1.0, 0.0).astype(jnp.bfloat16)
        owner_digits = jnp.concatenate([jnp.where(in_run, ROW_SPLIT, 0.0).astype(jnp.bfloat16),
                                        owner], axis=0)
        want = jnp.dot(digits, owner_digits, preferred_element_type=jnp.float32)
        weight = jnp.dot(gate, owner, preferred_element_type=jnp.float32)
        comb = jnp.where(want == lane_t + (r0 + 1.0), weight, 0.0).astype(jnp.bfloat16)
        return jnp.dot(comb, sorted_ref[j * tm:(j + 1) * tm, :],
                       preferred_element_type=jnp.float32)

    always = TOP_K
    routed = chunk_sum(0)
    for j in range(1, always):
        routed = routed + chunk_sum(j)
    acc_ref[...] = routed
    for j in range(always, n_chunks):
        @pl.when(used_ref[i] > j * tm)
        def _(j=j):
            acc_ref[...] += chunk_sum(j)
    routed = acc_ref[...]
    u = u_ref[...]
    h1 = jnp.dot(u, ws1_ref[...], preferred_element_type=jnp.float32)
    h3 = jnp.dot(u, ws3_ref[...], preferred_element_type=jnp.float32)
    shared = jnp.dot((_silu(h1) * h3).astype(jnp.bfloat16), ws2_ref[...],
                     preferred_element_type=jnp.float32)
    r = alpha * x_ref[...] + g2_ref[pl.ds(bidx, 1), :] * (routed + shared)
    o_ref[...] = _layer_norm_f32(r) * lng_ref[...] + lnb_ref[...]


def _final(cnt, off, dst, used, row1_t, gate_t, cnt_tab, off_tab, y_slots, x_mid, u2b, mod,
           ln_g, ln_b, ws1, ws3, ws2, n_rows, seq_len, n_batch, alpha):
    tm = MOE_TILE
    d = D_MODEL
    row = lambda i, *_: (i, 0)
    const = lambda i, *_: (0, 0)
    tab_spec = pl.BlockSpec((None, EXPERT_PAD, LANES), lambda i, *_: (i, 0, 0))
    grid_spec = pltpu.PrefetchScalarGridSpec(
        num_scalar_prefetch=4,
        grid=(n_rows // tm,),
        in_specs=[
            pl.BlockSpec((tm, EXPERT_PAD), row),
            pl.BlockSpec((tm, EXPERT_PAD), row),
            tab_spec,
            tab_spec,
            pl.BlockSpec(memory_space=pl.ANY),
            pl.BlockSpec((tm, d), row),
            pl.BlockSpec((tm, d), row),
            pl.BlockSpec((8, d), lambda i, *_: (0, 5)),
            pl.BlockSpec((1, d), const),
            pl.BlockSpec((1, d), const),
            pl.BlockSpec((d, SHARED_FF), const),
            pl.BlockSpec((d, SHARED_FF), const),
            pl.BlockSpec((SHARED_FF, d), const),
        ],
        out_specs=pl.BlockSpec((tm, d), row),
        scratch_shapes=[pltpu.VMEM((SORTED_ROWS, d), jnp.bfloat16),
                        pltpu.VMEM((tm, d), jnp.float32),
                        pltpu.SemaphoreType.DMA],
    )
    return pl.pallas_call(
        functools.partial(_final_kernel, tiles_per_seq=seq_len // tm, n_batch=n_batch, alpha=alpha),
        out_shape=jax.ShapeDtypeStruct((n_rows, d), jnp.float32),
        grid_spec=grid_spec,
        compiler_params=_cparams(1),
        name="moe_combine_postnorm",
    )(cnt, off, dst, used, row1_t, gate_t, cnt_tab, off_tab, y_slots, x_mid, u2b, mod,
      ln_g, ln_b, ws1, ws3, ws2)


def _moe_sublayer(x_mid, u2, u2b, mod, ln_g, ln_b, router_w, router_bias, w1, w3, w2,
                  ws1, ws3, ws2, n_rows, seq_len, n_batch, alpha):
    bf = jnp.bfloat16
    blk = MOE_BLK
    n_tiles = n_rows // MOE_TILE
    row1, row1_t, gate_t, cnt_tab, off_tab, base_tab, total = _router(
        u2, router_w, router_bias, n_rows)
    as_int = lambda t: t[:, :N_EXPERTS, 0].astype(jnp.int32)
    cnt, off, base = as_int(cnt_tab), as_int(off_tab), as_int(base_tab)
    used = off[:, -1] + cnt[:, -1]
    bounds = jnp.stack([off_tab[:, :, 0], off_tab[:, :, 0] + cnt_tab[:, :, 0]], axis=1)
    total = total.reshape(N_EXPERTS).astype(jnp.int32)
    padded = (total + blk - 1) // blk * blk
    pends = jnp.cumsum(padded)
    pstart = pends - padded
    dst = pstart[None, :] + base
    tail = jnp.concatenate([pstart + total, padded - total])
    n_used_blk = (pends[-1:] // blk).astype(jnp.int32)
    max_rows = n_rows * TOP_K + n_tiles * N_EXPERTS * (ROW_ALIGN - 1) + N_EXPERTS * (blk - 1)
    n_blk = -(-max_rows // blk)
    blk_start = jnp.arange(n_blk, dtype=jnp.int32) * blk
    blk_e = jnp.minimum(jnp.sum((blk_start[:, None] >= pends[None, :]).astype(jnp.int32), axis=1),
                        N_EXPERTS - 1)
    blk_valid = (blk_start < pends[-1]).astype(jnp.int32)
    flat = lambda t: t.reshape(n_tiles * N_EXPERTS)
    xs = _dispatch(u2b, row1, bounds, flat(cnt), flat(off), flat(dst), tail, n_used_blk, used,
                   n_blk)
    y_slots = _experts(xs, blk_e, blk_valid, w1.astype(bf), w3.astype(bf), w2.astype(bf), n_blk)
    return _final(flat(cnt), flat(off), flat(dst), used, row1_t, gate_t, cnt_tab, off_tab,
                  y_slots, x_mid, u2b, mod, ln_g, ln_b, ws1.astype(bf), ws3.astype(bf),
                  ws2.astype(bf), n_rows, seq_len, n_batch, alpha)


def _pack_in_proj(w_in):
    d = w_in.shape[0]
    w_dt = jnp.zeros((d, P_FOUR - P_DT), w_in.dtype)
    for direction, lane0 in ((0, 0), (1, DT_BWD_LANE)):
        src = OFF_DT + direction * SSM_HEADS
        w_dt = w_dt.at[:, lane0:lane0 + SSM_HEADS].set(w_in[:, src:src + SSM_HEADS])
    return jnp.concatenate([
        w_in[:, OFF_XBC:OFF_DT], w_dt, w_in[:, OFF_FOUR:OFF_GATE],
        w_in[:, OFF_GATE:], w_in[:, :OFF_XBC]], axis=1).astype(jnp.bfloat16)


def _pack_head_rows(v):
    out = jnp.zeros((1, LANES), jnp.float32)
    v = v.reshape(2, SSM_HEADS).astype(jnp.float32)
    out = out.at[0, 0:SSM_HEADS].set(v[0])
    return out.at[0, DT_BWD_LANE:DT_BWD_LANE + SSM_HEADS].set(v[1])


def kernel(x, c, ctx, c_ctx, w_ada, b_ada, w_in, conv_w, conv_b, dt_bias, a_log, d_skip,
           ssm_norm_w, w_br_ssm, w_br_four, w_out, ln1_g, ln1_b, ln2_g, ln2_b,
           router_w, router_bias, w1, w3, w2, ws1, ws3, ws2):
    n_batch, seq_len, d = x.shape
    ctx_len = ctx.shape[1]
    depth = w_ada.shape[0]
    bf = jnp.bfloat16
    alpha = float((2 * depth) ** 0.25)
    n_lat = n_batch * seq_len
    assert d == D_MODEL and n_batch + 1 <= 8
    assert seq_len % K1_TM == 0 and (n_batch * ctx_len) % K1_TM == 0
    assert seq_len % (LANES * 8) == 0

    x_all = _assemble_stream(x, ctx)
    c_rows = jnp.zeros((8, d), jnp.float32).at[:n_batch].set(c).at[n_batch].set(c_ctx)
    mod_all = _ada_mod(c_rows, w_ada, b_ada)
    tables = _dft_tables(seq_len)

    for i in range(depth):
        last = i == depth - 1
        mod = mod_all[i]
        p_all = _in_proj(x_all, mod, _pack_in_proj(w_in[i]), seq_len, n_batch)
        a_row = _pack_head_rows(-jnp.exp(a_log[i].astype(jnp.float32)))
        xbc, dt_all, cum_all, cumt_all, bt_all = _conv(
            p_all, conv_w[i], conv_b[i], _pack_head_rows(dt_bias[i]), a_row,
            seq_len, n_lat, ctx_len)
        yf, yb = _ssd(xbc, dt_all, cum_all, cumt_all, bt_all, n_batch, seq_len, ctx_len)
        f_all = _fourier_latent(p_all, n_batch, seq_len, tables)
        n_rows = n_lat if last else x_all.shape[0]
        if not last:
            f_all = jnp.concatenate(
                [f_all, _fourier_ctx(p_all, n_batch, seq_len, ctx_len, tables)], axis=0)
        d_cols = jnp.repeat(d_skip[i].astype(jnp.float32), SSM_HEAD_DIM).reshape(1, D_INNER)
        x_mid, u2, u2b = _merge(p_all, xbc, yf, yb, f_all, x_all, mod, d_cols,
                           ssm_norm_w[i].reshape(1, D_INNER), ln1_g[i].reshape(1, d),
                           ln1_b[i].reshape(1, d), w_br_ssm[i].astype(bf),
                           w_br_four[i].astype(bf), w_out[i].astype(bf),
                           n_rows, seq_len, n_batch, alpha)
        x_all = _moe_sublayer(x_mid, u2, u2b, mod, ln2_g[i].reshape(1, d), ln2_b[i].reshape(1, d),
                              router_w[i], router_bias[i], w1[i], w3[i], w2[i],
                              ws1[i], ws3[i], ws2[i], n_rows, seq_len, n_batch, alpha)
    return x_all[:n_lat].reshape(n_batch, seq_len, d)
```

```python
import functools

import jax
import jax.numpy as jnp
import numpy as np
from jax import lax
from jax.experimental import pallas as pl
from jax.experimental.pallas import tpu as pltpu

D_MODEL = 1024
GRID_W = 64
POS_BASE = 10000.0
LN_EPS = 1e-6

SSM_HEADS = 24
SSM_HEAD_DIM = 64
D_INNER = SSM_HEADS * SSM_HEAD_DIM
SSM_GROUPS = 4
HEADS_PER_GROUP = SSM_HEADS // SSM_GROUPS
SSM_STATE = 128
CONV_W = 5
CONV_CH = D_INNER + 2 * SSM_GROUPS * SSM_STATE
CHUNK = 128

FOURIER_GROUPS = 4
FOURIER_GROUP_DIM = 256
FOURIER_W = FOURIER_GROUPS * FOURIER_GROUP_DIM

OFF_XBC = D_INNER
OFF_DT = OFF_XBC + CONV_CH
OFF_FOUR = OFF_DT + 2 * SSM_HEADS
OFF_GATE = OFF_FOUR + FOURIER_W

N_EXPERTS = 64
TOP_K = 8
N_EXPERT_GROUPS = 8
EXPERTS_PER_GROUP = N_EXPERTS // N_EXPERT_GROUPS
TOPK_GROUPS = 4
EXPERT_FF = 256
SHARED_FF = 256
ROUTE_SCALE = 2.5

LANES = 128
VMEM_LIMIT_BYTES = 56 * 1024 * 1024

P_XBC = 0
P_DT = CONV_CH
P_FOUR = P_DT + SSM_GROUPS * LANES
P_GATE = P_FOUR + FOURIER_W
P_Z = P_GATE + 2 * D_MODEL
P_WIDTH = P_Z + D_INNER
DT_BWD_LANE = 32
CUMT_ROWS = 64

K1_TM = 512
K1_TN = 1536
ROW_TILE = 256
MOE_TILE = 512
MOE_BLK = 512
ROW_ALIGN = 16
EXPERT_PAD = LANES
ROW_SPLIT = 64.0
RUN_PIECES = tuple(MOE_TILE >> s for s in range(6))
TAIL_PIECES = tuple(p for p in RUN_PIECES if p < MOE_BLK)
SORTED_ROWS = -(-(MOE_TILE * TOP_K + N_EXPERTS * (ROW_ALIGN - 1)) // MOE_TILE) * MOE_TILE


def _cparams(n_axes=1):
    return pltpu.CompilerParams(
        dimension_semantics=("arbitrary",) * n_axes,
        vmem_limit_bytes=VMEM_LIMIT_BYTES)


def _layer_norm_f32(x):
    mu = jnp.mean(x, axis=-1, keepdims=True)
    xc = x - mu
    var = jnp.mean(xc * xc, axis=-1, keepdims=True)
    return xc * lax.rsqrt(var + LN_EPS)


def _sigmoid(x):
    return 1.0 / (1.0 + jnp.exp(-x))


def _silu(x):
    return x * _sigmoid(x)


def _pos_kernel(x_ref, ctx_ref, er_ref, ec_ref, o_ref, *, n_lat):
    i = pl.program_id(0)
    half = D_MODEL // 2

    @pl.when(i < n_lat)
    def _():
        ec = ec_ref[...]
        for r in range(8):
            rows = slice(GRID_W * r, GRID_W * (r + 1))
            o_ref[rows, :half] = x_ref[rows, :half] + er_ref[r:r + 1, :]
            o_ref[rows, half:] = x_ref[rows, half:] + ec

    @pl.when(i >= n_lat)
    def _():
        o_ref[...] = ctx_ref[...]


def _assemble_stream(x, ctx):
    b, l, d = x.shape
    lc = ctx.shape[1]
    tile = 8 * GRID_W
    n_lat = (b * l) // tile
    n_ctx = (b * lc) // tile
    rows = l // GRID_W
    quarter = D_MODEL // 4
    omega = 1.0 / (POS_BASE ** (jnp.arange(quarter, dtype=jnp.float32) / quarter))
    ang_r = jnp.arange(rows, dtype=jnp.float32)[:, None] * omega
    ang_c = jnp.arange(GRID_W, dtype=jnp.float32)[:, None] * omega
    emb_r = jnp.concatenate([jnp.sin(ang_r), jnp.cos(ang_r)], -1)
    emb_c = jnp.concatenate([jnp.sin(ang_c), jnp.cos(ang_c)], -1)
    tiles_per_seq = l // tile
    return pl.pallas_call(
        functools.partial(_pos_kernel, n_lat=n_lat),
        out_shape=jax.ShapeDtypeStruct((b * l + b * lc, d), jnp.float32),
        grid=(n_lat + n_ctx,),
        in_specs=[
            pl.BlockSpec((tile, d), lambda i: (jnp.minimum(i, n_lat - 1), 0)),
            pl.BlockSpec((tile, d), lambda i: (jnp.maximum(i - n_lat, 0), 0)),
            pl.BlockSpec((8, d // 2), lambda i: (i % tiles_per_seq, 0)),
            pl.BlockSpec((GRID_W, d // 2), lambda i: (0, 0)),
        ],
        out_specs=pl.BlockSpec((tile, d), lambda i: (i, 0)),
        compiler_params=_cparams(1),
        name="assemble_stream",
    )(x.reshape(b * l, d), ctx.reshape(b * lc, d), emb_r, emb_c)


def _ada_kernel(c_ref, w_ref, b_ref, o_ref):
    c = c_ref[...]
    o_ref[...] = jnp.dot(_silu(c), w_ref[...], preferred_element_type=jnp.float32,
                         precision=lax.Precision.HIGHEST) + b_ref[...]


def _ada_mod(c_rows, w_ada, b_ada):
    depth, d, n6 = w_ada.shape
    tn = 1536
    return pl.pallas_call(
        _ada_kernel,
        out_shape=jax.ShapeDtypeStruct((depth, 8, n6), jnp.float32),
        grid=(depth, n6 // tn),
        in_specs=[
            pl.BlockSpec((8, d), lambda a, j: (0, 0)),
            pl.BlockSpec((None, d, tn), lambda a, j: (a, 0, j)),
            pl.BlockSpec((None, 1, tn), lambda a, j: (a, 0, j)),
        ],
        out_specs=pl.BlockSpec((None, 8, tn), lambda a, j: (a, 0, j)),
        compiler_params=_cparams(2),
        name="ada_mod",
    )(c_rows, w_ada, b_ada.reshape(depth, 1, n6))


def _k1_kernel(x_ref, sc_ref, sh_ref, w_ref, o_ref, u_ref, *, tiles_per_seq, n_batch):
    i = pl.program_id(0)
    bidx = jnp.minimum(i // tiles_per_seq, n_batch)
    xn = _layer_norm_f32(x_ref[...])
    u = xn * (1.0 + sc_ref[pl.ds(bidx, 1), :]) + sh_ref[pl.ds(bidx, 1), :]
    u_ref[...] = u.astype(jnp.bfloat16)
    for c in range(o_ref.shape[1] // K1_TN):
        cols = slice(c * K1_TN, (c + 1) * K1_TN)
        o_ref[:, cols] = jnp.dot(u_ref[...], w_ref[:, cols],
                                 preferred_element_type=jnp.float32).astype(jnp.bfloat16)


def _in_proj(x_all, mod, w_all, seq_len, n_batch):
    nt, d = x_all.shape
    tm = K1_TM
    return pl.pallas_call(
        functools.partial(_k1_kernel, tiles_per_seq=seq_len // tm, n_batch=n_batch),
        out_shape=jax.ShapeDtypeStruct((nt, P_WIDTH), jnp.bfloat16),
        grid=(nt // tm,),
        in_specs=[
            pl.BlockSpec((tm, d), lambda i: (i, 0)),
            pl.BlockSpec((8, d), lambda i: (0, 1)),
            pl.BlockSpec((8, d), lambda i: (0, 0)),
            pl.BlockSpec((d, P_WIDTH), lambda i: (0, 0), pipeline_mode=pl.Buffered(1)),
        ],
        out_specs=pl.BlockSpec((tm, P_WIDTH), lambda i: (i, 0)),
        scratch_shapes=[pltpu.VMEM((tm, d), jnp.bfloat16)],
        compiler_params=_cparams(1),
        name="in_proj",
    )(x_all, mod, mod, w_all)


def _softplus(x):
    return jnp.maximum(x, 0.0) + jnp.log1p(jnp.exp(-jnp.abs(x)))


def _conv_kernel(cur_ref, prev_ref, next_ref, w_ref, b_ref, dtraw_ref, bias_ref, a_ref,
                 o_ref, dt_ref, cum_ref, cumt_ref, bt_ref, ext_ref, *, tiles_per_seq, n_lat):
    i = pl.program_id(0)
    is_ctx = i >= n_lat
    is_start = jnp.logical_or(i % tiles_per_seq == 0, is_ctx)
    is_end = jnp.logical_or(i % tiles_per_seq == tiles_per_seq - 1, is_ctx)
    halo = prev_ref.shape[0]
    rows = cur_ref.shape[0]
    ext_ref[0:halo, :] = jnp.where(is_start, 0.0, prev_ref[...].astype(jnp.float32))
    ext_ref[halo:halo + rows, :] = cur_ref[...].astype(jnp.float32)
    ext_ref[halo + rows:, :] = jnp.where(is_end, 0.0, next_ref[...].astype(jnp.float32))

    def conv_lane_block(c, carry):
        lanes = pl.ds(pl.multiple_of(c * LANES, LANES), LANES)
        acc = b_ref[:, lanes] + w_ref[0:1, lanes] * ext_ref[pl.ds(halo - 2, rows), lanes]
        for k in range(1, CONV_W):
            acc = acc + w_ref[k:k + 1, lanes] * ext_ref[pl.ds(halo - 2 + k, rows), lanes]
        o_ref[:, lanes] = _silu(acc).astype(jnp.bfloat16)
        return carry

    lax.fori_loop(0, CONV_CH // LANES, conv_lane_block, 0)

    q = CHUNK
    dt = _softplus(dtraw_ref[...].astype(jnp.float32) + bias_ref[...])
    dt_ref[...] = dt
    adt = dt * a_ref[...]
    row_i = lax.broadcasted_iota(jnp.int32, (q, q), 0)
    col_i = lax.broadcasted_iota(jnp.int32, (q, q), 1)
    lower = (row_i >= col_i).astype(jnp.float32)
    upper = (row_i <= col_i).astype(jnp.float32)
    is_fwd = lax.broadcasted_iota(jnp.int32, (1, LANES), 1) < DT_BWD_LANE
    for ch in range(rows // q):
        rs = slice(q * ch, q * (ch + 1))
        cum_f = jnp.dot(lower, adt[rs, :], preferred_element_type=jnp.float32,
                        precision=lax.Precision.HIGHEST)
        cum_b = jnp.dot(upper, adt[rs, :], preferred_element_type=jnp.float32,
                        precision=lax.Precision.HIGHEST)
        cum = jnp.where(is_fwd, cum_f, cum_b)
        cum_ref[rs, :] = cum
        cumt_ref[ch] = cum.T[:CUMT_ROWS, :]
        for g in range(SSM_GROUPS):
            b_cols = slice(D_INNER + SSM_STATE * g, D_INNER + SSM_STATE * (g + 1))
            bt_ref[ch, SSM_STATE * g:SSM_STATE * (g + 1), :] = (
                o_ref[rs, b_cols].astype(jnp.float32).T.astype(jnp.bfloat16))


def _conv(p_all, conv_w, conv_b, dt_bias_row, a_row, seq_len, n_lat_rows, ctx_len):
    nt = p_all.shape[0]
    tl = ROW_TILE
    assert ctx_len == tl, "context sequences must span exactly one conv tile"
    halo = 16
    hb = tl // halo
    n_halo_blocks = nt // halo
    dtw = LANES
    cpt = tl // CHUNK
    w8 = jnp.zeros((8, CONV_CH), jnp.float32).at[:CONV_W].set(conv_w)
    return pl.pallas_call(
        functools.partial(_conv_kernel, tiles_per_seq=seq_len // tl, n_lat=n_lat_rows // tl),
        out_shape=(jax.ShapeDtypeStruct((nt, CONV_CH), jnp.bfloat16),
                   jax.ShapeDtypeStruct((nt, dtw), jnp.float32),
                   jax.ShapeDtypeStruct((nt, dtw), jnp.float32),
                   jax.ShapeDtypeStruct((nt // CHUNK, CUMT_ROWS, CHUNK), jnp.float32),
                   jax.ShapeDtypeStruct((nt // CHUNK, SSM_GROUPS * SSM_STATE, CHUNK), jnp.bfloat16)),
        grid=(nt // tl,),
        in_specs=[
            pl.BlockSpec((tl, CONV_CH), lambda i: (i, 0)),
            pl.BlockSpec((halo, CONV_CH), lambda i: (jnp.maximum(i * hb - 1, 0), 0)),
            pl.BlockSpec((halo, CONV_CH),
                         lambda i: (jnp.minimum((i + 1) * hb, n_halo_blocks - 1), 0)),
            pl.BlockSpec((8, CONV_CH), lambda i: (0, 0)),
            pl.BlockSpec((1, CONV_CH), lambda i: (0, 0)),
            pl.BlockSpec((tl, dtw), lambda i: (i, P_DT // dtw)),
            pl.BlockSpec((1, dtw), lambda i: (0, 0)),
            pl.BlockSpec((1, dtw), lambda i: (0, 0)),
        ],
        out_specs=(pl.BlockSpec((tl, CONV_CH), lambda i: (i, 0)),
                   pl.BlockSpec((tl, dtw), lambda i: (i, 0)),
                   pl.BlockSpec((tl, dtw), lambda i: (i, 0)),
                   pl.BlockSpec((cpt, CUMT_ROWS, CHUNK), lambda i: (i, 0, 0)),
                   pl.BlockSpec((cpt, SSM_GROUPS * SSM_STATE, CHUNK), lambda i: (i, 0, 0))),
        scratch_shapes=[pltpu.VMEM((tl + 2 * halo, CONV_CH), jnp.float32)],
        compiler_params=_cparams(1),
        name="dwconv_silu",
    )(p_all, p_all, p_all, w8, conv_b.reshape(1, CONV_CH), p_all, dt_bias_row, a_row)


def _ssd_direction(refs, s_ref, y_ref, gi, *, lane0, forward):
    x_ref, b_ref, c_ref, dt_ref, cum_ref, cumt_ref, bt_ref = refs
    q = CHUNK
    gw = HEADS_PER_GROUP * SSM_HEAD_DIM
    row_i = lax.broadcasted_iota(jnp.int32, (q, q), 0)
    col_i = lax.broadcasted_iota(jnp.int32, (q, q), 1)
    tri = (row_i >= col_i) if forward else (row_i <= col_i)
    lane_lo = lax.broadcasted_iota(jnp.int32, (q, LANES), 1) < SSM_HEAD_DIM
    lane_lo_row = lax.broadcasted_iota(jnp.int32, (1, LANES), 1) < SSM_HEAD_DIM

    g_lanes = slice(LANES * gi, LANES * (gi + 1))
    dt = dt_ref[...]
    cum = cum_ref[...]
    cum_t = cumt_ref[...]
    total = cum[q - 1:q, :] if forward else cum[0:1, :]

    cm = c_ref[:, g_lanes]
    bm = b_ref[:, g_lanes]
    scores = lax.dot_general(cm, bm, (((1,), (1,)), ((), ())),
                             preferred_element_type=jnp.float32)
    bm_t = bt_ref[SSM_STATE * gi:SSM_STATE * (gi + 1), :]

    for pr in range(HEADS_PER_GROUP // 2):
        r0 = lane0 + 2 * pr
        r1 = r0 + 1
        lanes = slice(gw * gi + LANES * pr, gw * gi + LANES * (pr + 1))
        xp = x_ref[:, lanes].astype(jnp.float32)
        dt_pair = jnp.where(lane_lo, dt[:, r0:r0 + 1], dt[:, r1:r1 + 1])
        cum_pair = jnp.where(lane_lo, cum[:, r0:r0 + 1], cum[:, r1:r1 + 1])
        tot_pair = jnp.where(lane_lo_row, total[:, r0:r0 + 1], total[:, r1:r1 + 1])
        xdt = xp * dt_pair
        l0 = jnp.exp(jnp.where(tri, cum[:, r0:r0 + 1] - cum_t[r0:r0 + 1, :], -jnp.inf))
        l1 = jnp.exp(jnp.where(tri, cum[:, r1:r1 + 1] - cum_t[r1:r1 + 1, :], -jnp.inf))
        w = jnp.concatenate([(scores * l0).astype(jnp.bfloat16),
                             (scores * l1).astype(jnp.bfloat16)], axis=1)
        xdt_b = xdt.astype(jnp.bfloat16)
        zero = jnp.zeros_like(xdt_b)
        rhs = jnp.concatenate([jnp.where(lane_lo, xdt_b, zero),
                               jnp.where(lane_lo, zero, xdt_b)], axis=0)
        y_diag = jnp.dot(w, rhs, preferred_element_type=jnp.float32)
        s_old = s_ref[:, lanes]
        y_off = jnp.dot(cm, s_old.astype(jnp.bfloat16),
                        preferred_element_type=jnp.float32) * jnp.exp(cum_pair)
        y_ref[:, lanes] = (y_diag + y_off).astype(jnp.bfloat16)
        decayed = (xdt * jnp.exp(tot_pair - cum_pair)).astype(jnp.bfloat16)
        s_ref[:, lanes] = jnp.exp(tot_pair) * s_old + jnp.dot(
            bm_t, decayed, preferred_element_type=jnp.float32)


def _ssd_kernel(*refs):
    n_in = 7
    fwd_refs, bwd_refs = refs[:n_in], refs[n_in:2 * n_in]
    yf_ref, yb_ref, sf_ref, sb_ref = refs[2 * n_in:]

    @pl.when(pl.program_id(2) == 0)
    def _():
        sf_ref[...] = jnp.zeros_like(sf_ref)
        sb_ref[...] = jnp.zeros_like(sb_ref)

    for gi in range(SSM_GROUPS):
        _ssd_direction(fwd_refs, sf_ref, yf_ref, gi, lane0=HEADS_PER_GROUP * gi, forward=True)
        _ssd_direction(bwd_refs, sb_ref, yb_ref, gi,
                       lane0=DT_BWD_LANE + HEADS_PER_GROUP * gi, forward=False)


def _ssd(xbc, dt_all, cum_all, cumt_all, bt_all, n_batch, seq_len, ctx_len):
    nt = xbc.shape[0]
    q = CHUNK
    nc_lat = seq_len // q
    nc_ctx = ctx_len // q
    ctx_blk0 = (n_batch * seq_len) // q
    n_steps = nc_ctx + nc_lat
    gw = D_INNER
    sw = SSM_GROUPS * SSM_STATE
    b_blk0 = D_INNER // sw
    c_blk0 = b_blk0 + 1

    def fwd_row(b, j):
        return jnp.where(j < nc_ctx, ctx_blk0 + nc_ctx * b + j, nc_lat * b + (j - nc_ctx))

    def bwd_row(b, j):
        return jnp.where(j < nc_ctx, ctx_blk0 + nc_ctx * b + (nc_ctx - 1 - j),
                         nc_lat * b + (n_steps - 1 - j))

    def specs(row):
        return [
            pl.BlockSpec((q, gw), lambda b, g, j: (row(b, j), g)),
            pl.BlockSpec((q, sw), lambda b, g, j: (row(b, j), b_blk0 + g)),
            pl.BlockSpec((q, sw), lambda b, g, j: (row(b, j), c_blk0 + g)),
            pl.BlockSpec((q, LANES), lambda b, g, j: (row(b, j), 0)),
            pl.BlockSpec((q, LANES), lambda b, g, j: (row(b, j), 0)),
            pl.BlockSpec((None, CUMT_ROWS, q), lambda b, g, j: (row(b, j), 0, 0)),
            pl.BlockSpec((None, sw, q), lambda b, g, j: (row(b, j), 0, 0)),
        ]

    out_sds = jax.ShapeDtypeStruct((nt, D_INNER), jnp.bfloat16)
    operands = (xbc, xbc, xbc, dt_all, cum_all, cumt_all, bt_all)
    return pl.pallas_call(
        _ssd_kernel,
        out_shape=(out_sds, out_sds),
        grid=(n_batch, 1, n_steps),
        in_specs=specs(fwd_row) + specs(bwd_row),
        out_specs=(pl.BlockSpec((q, gw), lambda b, g, j: (fwd_row(b, j), g)),
                   pl.BlockSpec((q, gw), lambda b, g, j: (bwd_row(b, j), g))),
        scratch_shapes=[pltpu.VMEM((SSM_STATE, gw), jnp.float32),
                        pltpu.VMEM((SSM_STATE, gw), jnp.float32)],
        compiler_params=_cparams(3),
        name="ssd_scan",
    )(*operands, *operands)


def _dft_tables(seq_len):
    l1n = seq_len // LANES
    two_pi = 2.0 * np.pi
    gd = FOURIER_GROUP_DIM
    jj = jnp.arange(gd, dtype=jnp.int32)
    ang_c = ((jj[:, None] * jj[None, :]) % gd).astype(jnp.float32) * (two_pi / gd)
    cc, sc = jnp.cos(ang_c), jnp.sin(ang_c)
    k1 = jnp.arange(l1n, dtype=jnp.int32)
    ang1 = ((k1[:, None] * k1[None, :]) % l1n).astype(jnp.float32) * (two_pi / l1n)
    w1 = jnp.concatenate([jnp.cos(ang1), -jnp.sin(ang1)], axis=0)
    k2 = jnp.arange(LANES, dtype=jnp.int32)
    kk = k1[:, None, None] + l1n * k2[None, :, None]
    ang2 = ((kk * k2[None, None, :]) % seq_len).astype(jnp.float32) * (two_pi / seq_len)
    er, ei = jnp.cos(ang2), -jnp.sin(ang2)
    e = jnp.concatenate([jnp.concatenate([er, -ei], axis=2),
                         jnp.concatenate([ei, er], axis=2)], axis=1)
    return cc, sc, w1, e


def _dft1_kernel(w_ref, x_ref, o_ref):
    o_ref[...] = jnp.dot(w_ref[...], x_ref[...],
                         preferred_element_type=jnp.float32).astype(jnp.bfloat16)


def _channel_mix(gr, gi, cs_ref, o_ref):
    gd = FOURIER_GROUP_DIM
    for g in range(FOURIER_GROUPS):
        cols = slice(gd * g, gd * (g + 1))
        lhs = jnp.concatenate([gr[:, cols], gi[:, cols]], axis=1).astype(jnp.bfloat16)
        o_ref[:, cols] = jnp.dot(lhs, cs_ref[...],
                                 preferred_element_type=jnp.float32).astype(jnp.bfloat16)


def _dft2_kernel(e_ref, ar_ref, ai_ref, cs_ref, o_ref):
    a = jnp.concatenate([ar_ref[...], ai_ref[...]], axis=0)
    g = jnp.dot(e_ref[...], a, preferred_element_type=jnp.float32)
    half = g.shape[0] // 2
    _channel_mix(g[:half], g[half:], cs_ref, o_ref)


def _dft_ctx_kernel(w_ref, x_ref, cs_ref, o_ref):
    g = jnp.dot(w_ref[...], x_ref[...], preferred_element_type=jnp.float32)
    half = g.shape[0] // 2
    _channel_mix(g[:half], g[half:], cs_ref, o_ref)


def _fourier_latent(p_all, n_batch, seq_len, tables):
    cc, sc, w1, e = tables
    c = FOURIER_W
    l1n = seq_len // LANES
    ncol = LANES * c
    four = p_all[:n_batch * seq_len, P_FOUR:P_FOUR + c].reshape(n_batch, l1n, ncol)
    tn = 4096
    a = pl.pallas_call(
        _dft1_kernel,
        out_shape=jax.ShapeDtypeStruct((n_batch, 2 * l1n, ncol), jnp.bfloat16),
        grid=(n_batch, ncol // tn),
        in_specs=[pl.BlockSpec((2 * l1n, l1n), lambda b, j: (0, 0)),
                  pl.BlockSpec((None, l1n, tn), lambda b, j: (b, 0, j))],
        out_specs=pl.BlockSpec((None, 2 * l1n, tn), lambda b, j: (b, 0, j)),
        compiler_params=_cparams(2),
        name="dft_stage1",
    )(w1.astype(jnp.bfloat16), four)
    a4 = a.reshape(n_batch, 2 * l1n, LANES, c)
    norm = 1.0 / np.sqrt(float(seq_len) * FOURIER_GROUP_DIM)
    cs = (jnp.concatenate([cc, sc], axis=0) * norm).astype(jnp.bfloat16)
    o = pl.pallas_call(
        _dft2_kernel,
        out_shape=jax.ShapeDtypeStruct((n_batch, l1n, LANES, c), jnp.bfloat16),
        grid=(n_batch, l1n),
        in_specs=[pl.BlockSpec((None, 2 * LANES, 2 * LANES), lambda b, k: (k, 0, 0)),
                  pl.BlockSpec((None, None, LANES, c), lambda b, k: (b, k, 0, 0)),
                  pl.BlockSpec((None, None, LANES, c), lambda b, k: (b, l1n + k, 0, 0)),
                  pl.BlockSpec((2 * FOURIER_GROUP_DIM, FOURIER_GROUP_DIM), lambda b, k: (0, 0))],
        out_specs=pl.BlockSpec((None, None, LANES, c), lambda b, k: (b, k, 0, 0)),
        compiler_params=_cparams(2),
        name="dft_stage2",
    )(e.astype(jnp.bfloat16), a4, a4, cs)
    return o.transpose(0, 2, 1, 3).reshape(n_batch * seq_len, c)


def _fourier_ctx(p_all, n_batch, seq_len, ctx_len, tables):
    cc, sc, _, _ = tables
    assert ctx_len == FOURIER_GROUP_DIM
    c = FOURIER_W
    wc = jnp.concatenate([cc, -sc], axis=0).astype(jnp.bfloat16)
    norm = 1.0 / np.sqrt(float(ctx_len) * FOURIER_GROUP_DIM)
    cs = (jnp.concatenate([cc, sc], axis=0) * norm).astype(jnp.bfloat16)
    blk0 = (n_batch * seq_len) // ctx_len
    return pl.pallas_call(
        _dft_ctx_kernel,
        out_shape=jax.ShapeDtypeStruct((n_batch * ctx_len, c), jnp.bfloat16),
        grid=(n_batch,),
        in_specs=[pl.BlockSpec((2 * ctx_len, ctx_len), lambda b: (0, 0)),
                  pl.BlockSpec((ctx_len, c), lambda b: (blk0 + b, P_FOUR // c)),
                  pl.BlockSpec((2 * FOURIER_GROUP_DIM, FOURIER_GROUP_DIM), lambda b: (0, 0))],
        out_specs=pl.BlockSpec((ctx_len, c), lambda b: (b, 0)),
        compiler_params=_cparams(1),
        name="dft_ctx",
    )(wc, p_all, cs)


def _merge_kernel(gate_ref, z_ref, xs_ref, yf_ref, yb_ref, f_ref, x_ref,
                  g1_ref, sc2_ref, sh2_ref, d_ref, nw_ref, lng_ref, lnb_ref,
                  wssm_ref, wfour_ref, wout_ref, xo_ref, u2_ref, u2b_ref,
                  *, tiles_per_seq, n_batch, alpha):
    bidx = jnp.minimum(pl.program_id(0) // tiles_per_seq, n_batch)
    y = (yf_ref[...].astype(jnp.float32) + yb_ref[...].astype(jnp.float32)
         + xs_ref[...].astype(jnp.float32) * d_ref[...])
    h = y * _silu(z_ref[...].astype(jnp.float32))
    h = h * lax.rsqrt(jnp.mean(h * h, axis=-1, keepdims=True) + LN_EPS) * nw_ref[...]
    ssm = jnp.dot(h.astype(jnp.bfloat16), wssm_ref[...], preferred_element_type=jnp.float32)
    four = jnp.dot(f_ref[...], wfour_ref[...], preferred_element_type=jnp.float32)
    gates = _sigmoid(gate_ref[...].astype(jnp.float32))
    merged = gates[:, :D_MODEL] * ssm + gates[:, D_MODEL:] * four
    out = jnp.dot(merged.astype(jnp.bfloat16), wout_ref[...], preferred_element_type=jnp.float32)
    r = alpha * x_ref[...] + g1_ref[pl.ds(bidx, 1), :] * out
    xn = _layer_norm_f32(r) * lng_ref[...] + lnb_ref[...]
    xo_ref[...] = xn
    u2 = (_layer_norm_f32(xn) * (1.0 + sc2_ref[pl.ds(bidx, 1), :])
          + sh2_ref[pl.ds(bidx, 1), :])
    u2_ref[...] = u2
    u2b_ref[...] = u2.astype(jnp.bfloat16)


def _merge(p_all, xbc, yf, yb, f_all, x_all, mod, d_cols, norm_w, ln_g, ln_b,
           w_ssm, w_four, w_out, n_rows, seq_len, n_batch, alpha):
    tm = ROW_TILE
    d = D_MODEL
    row = lambda i: (i, 0)
    const = lambda i: (0, 0)
    out_sds = jax.ShapeDtypeStruct((n_rows, d), jnp.float32)
    return pl.pallas_call(
        functools.partial(_merge_kernel, tiles_per_seq=seq_len // tm, n_batch=n_batch, alpha=alpha),
        out_shape=(out_sds, out_sds, jax.ShapeDtypeStruct((n_rows, d), jnp.bfloat16)),
        grid=(n_rows // tm,),
        in_specs=[
            pl.BlockSpec((tm, 2 * d), lambda i: (i, P_GATE // (2 * d))),
            pl.BlockSpec((tm, D_INNER), lambda i: (i, P_Z // D_INNER)),
            pl.BlockSpec((tm, D_INNER), row),
            pl.BlockSpec((tm, D_INNER), row),
            pl.BlockSpec((tm, D_INNER), row),
            pl.BlockSpec((tm, FOURIER_W), row),
            pl.BlockSpec((tm, d), row),
            pl.BlockSpec((8, d), lambda i: (0, 2)),
            pl.BlockSpec((8, d), lambda i: (0, 4)),
            pl.BlockSpec((8, d), lambda i: (0, 3)),
            pl.BlockSpec((1, D_INNER), const),
            pl.BlockSpec((1, D_INNER), const),
            pl.BlockSpec((1, d), const),
            pl.BlockSpec((1, d), const),
            pl.BlockSpec((D_INNER, d), const),
            pl.BlockSpec((FOURIER_W, d), const),
            pl.BlockSpec((d, d), const),
        ],
        out_specs=(pl.BlockSpec((tm, d), row), pl.BlockSpec((tm, d), row),
                   pl.BlockSpec((tm, d), row)),
        compiler_params=_cparams(1),
        name="merge_postnorm",
    )(p_all, p_all, xbc, yf, yb, f_all, x_all, mod, mod, mod, d_cols, norm_w,
      ln_g, ln_b, w_ssm, w_four, w_out)


def _first_index_of_max(v, iota, big):
    m = jnp.max(v, axis=0, keepdims=True)
    idx = jnp.min(jnp.where(v == m, iota, big), axis=0, keepdims=True)
    return m, idx


def _router_kernel(u_ref, wt_ref, bias_ref, tri_ref, row_ref, rowt_ref, gatet_ref, cnt_ref,
                   off_ref, basetab_ref, tot_ref, base_ref):
    @pl.when(pl.program_id(0) == 0)
    def _():
        base_ref[...] = jnp.zeros_like(base_ref)

    tm = u_ref.shape[0]
    ne, epg = N_EXPERTS, EXPERTS_PER_GROUP
    neg = -jnp.inf
    logits = lax.dot_general(wt_ref[...], u_ref[...], (((1,), (1,)), ((), ())),
                             preferred_element_type=jnp.float32,
                             precision=lax.Precision.HIGHEST)
    scores = _sigmoid(logits)
    sel = scores + bias_ref[...]
    iota_g = lax.broadcasted_iota(jnp.int32, (epg, tm), 0).astype(jnp.float32)
    grp_rows = []
    for g in range(N_EXPERT_GROUPS):
        v = sel[epg * g:epg * (g + 1), :]
        m1, i1 = _first_index_of_max(v, iota_g, epg)
        m2 = jnp.max(jnp.where(iota_g == i1, neg, v), axis=0, keepdims=True)
        grp_rows.append(m1 + m2)
    grp = jnp.concatenate(grp_rows, axis=0)
    iota_n = lax.broadcasted_iota(jnp.int32, (N_EXPERT_GROUPS, tm), 0).astype(jnp.float32)
    chosen = jnp.zeros((N_EXPERT_GROUPS, tm), jnp.float32)
    for _ in range(TOPK_GROUPS):
        _, gi = _first_index_of_max(grp, iota_n, N_EXPERT_GROUPS)
        hit = iota_n == gi
        chosen = jnp.where(hit, 1.0, chosen)
        grp = jnp.where(hit, neg, grp)
    masked = jnp.concatenate(
        [jnp.where(chosen[g:g + 1, :] > 0.0, sel[epg * g:epg * (g + 1), :], neg)
         for g in range(N_EXPERT_GROUPS)], axis=0)
    iota_e = lax.broadcasted_iota(jnp.int32, (ne, tm), 0).astype(jnp.float32)
    picked = jnp.zeros((ne, tm), jnp.float32)
    for _ in range(TOP_K):
        _, ei = _first_index_of_max(masked, iota_e, ne)
        hit = iota_e == ei
        picked = jnp.where(hit, 1.0, picked)
        masked = jnp.where(hit, neg, masked)
    chosen_scores = picked * scores
    gate = chosen_scores / jnp.sum(chosen_scores, axis=0, keepdims=True) * ROUTE_SCALE
    before = jnp.dot(picked.astype(jnp.bfloat16), tri_ref[...],
                     preferred_element_type=jnp.float32)
    cnt = jnp.sum(picked, axis=1, keepdims=True)
    cnt16 = jnp.ceil(cnt * (1.0 / ROW_ALIGN)) * ROW_ALIGN
    cnt16_b = jnp.broadcast_to(cnt16, (ne, LANES))
    e_row = lax.broadcasted_iota(jnp.int32, (ne, ne), 0)
    e_col = lax.broadcasted_iota(jnp.int32, (ne, ne), 1)
    off_b = jnp.dot((e_col < e_row).astype(jnp.float32), cnt16_b,
                    preferred_element_type=jnp.float32,
                    precision=lax.Precision.HIGHEST)
    row1 = picked * (before + off_b[:, 0:1] + 1.0)
    pad = jnp.zeros((EXPERT_PAD - ne, tm), jnp.float32)
    row1_p = jnp.concatenate([row1, pad], axis=0)
    gate_p = jnp.concatenate([gate, pad], axis=0)
    pad_b = jnp.zeros((EXPERT_PAD - ne, LANES), jnp.float32)
    row_ref[...] = row1_p
    rowt_ref[...] = row1_p.T
    gatet_ref[...] = gate_p.T.astype(jnp.bfloat16)
    cnt_ref[...] = jnp.concatenate([cnt16_b, pad_b], axis=0)
    off_ref[...] = jnp.concatenate([off_b, pad_b], axis=0)
    basetab_ref[...] = jnp.broadcast_to(base_ref[...], (ne, LANES))
    base_ref[...] = base_ref[...] + cnt16
    tot_ref[...] = base_ref[...]


def _router(u2, router_w, router_bias, n_tok):
    tm = MOE_TILE
    n_tiles = n_tok // tm
    tri = jnp.triu(jnp.ones((tm, tm), jnp.float32), k=1).astype(jnp.bfloat16)
    tab = lambda rows: (jax.ShapeDtypeStruct((n_tiles, rows, LANES), jnp.float32),
                        pl.BlockSpec((None, rows, LANES), lambda i: (i, 0, 0)))
    outs = [
        (jax.ShapeDtypeStruct((n_tiles, EXPERT_PAD, tm), jnp.float32),
         pl.BlockSpec((None, EXPERT_PAD, tm), lambda i: (i, 0, 0))),
        (jax.ShapeDtypeStruct((n_tok, EXPERT_PAD), jnp.float32),
         pl.BlockSpec((tm, EXPERT_PAD), lambda i: (i, 0))),
        (jax.ShapeDtypeStruct((n_tok, EXPERT_PAD), jnp.bfloat16),
         pl.BlockSpec((tm, EXPERT_PAD), lambda i: (i, 0))),
        tab(EXPERT_PAD), tab(EXPERT_PAD), tab(N_EXPERTS),
        (jax.ShapeDtypeStruct((N_EXPERTS, 1), jnp.float32),
         pl.BlockSpec((N_EXPERTS, 1), lambda i: (0, 0))),
    ]
    return pl.pallas_call(
        _router_kernel,
        out_shape=tuple(o[0] for o in outs),
        grid=(n_tiles,),
        in_specs=[pl.BlockSpec((tm, D_MODEL), lambda i: (i, 0)),
                  pl.BlockSpec((N_EXPERTS, D_MODEL), lambda i: (0, 0)),
                  pl.BlockSpec((N_EXPERTS, 1), lambda i: (0, 0)),
                  pl.BlockSpec((tm, tm), lambda i: (0, 0))],
        out_specs=tuple(o[1] for o in outs),
        scratch_shapes=[pltpu.VMEM((N_EXPERTS, 1), jnp.float32)],
        compiler_params=_cparams(1),
        name="moe_router",
    )(u2, router_w.T, router_bias.reshape(N_EXPERTS, 1), tri)


def _run_copies(cnt, make_copy, pieces):
    done = jnp.int32(0)
    for piece in pieces:
        hit = (cnt & piece) != 0

        @pl.when(hit)
        def _(done=done, piece=piece):
            make_copy(done, piece)

        done = done + (cnt & piece)


def _piece_lists(cnt, off, dst):
    counts, srcs, dsts = [], [], []
    for piece in RUN_PIECES:
        has = (cnt & piece) != 0
        above = cnt & ~(2 * piece - 1)
        order = jnp.argsort(jnp.logical_not(has), axis=-1, stable=True)
        counts.append(jnp.sum(has.astype(jnp.int32), axis=-1))
        srcs.append(jnp.take_along_axis(off + above, order, axis=-1))
        dsts.append(jnp.take_along_axis(dst + above, order, axis=-1))
    flat = lambda parts: jnp.stack(parts, axis=1).reshape(-1).astype(jnp.int32)
    return flat(counts), flat(srcs), flat(dsts)


def _for_each_piece(tile, pn_ref, psrc_ref, pdst_ref, make_copy, start):
    for p_idx, piece in enumerate(RUN_PIECES):
        seg = tile * len(RUN_PIECES) + p_idx
        base = seg * N_EXPERTS

        def body(k, carry, piece=piece, base=base):
            cp = make_copy(pl.multiple_of(psrc_ref[base + k], ROW_ALIGN),
                           pl.multiple_of(pdst_ref[base + k], ROW_ALIGN), piece)
            if start:
                cp.start()
            else:
                cp.wait()
            return carry

        lax.fori_loop(0, pn_ref[seg], body, 0)


def _split_rows(row1):
    hi = jnp.floor(row1 * (1.0 / ROW_SPLIT))
    lo = row1 - hi * ROW_SPLIT
    return hi.astype(jnp.bfloat16), lo.astype(jnp.bfloat16)


def _dispatch_kernel(pn_ref, psrc_ref, pdst_ref, tail_ref, nblk_ref, used_ref, u_ref, row_ref,
                     bounds_ref, xs_hbm, sorted_ref, zero_ref, sems):
    i = pl.program_id(0)
    last = pl.num_programs(0) - 1
    slot = i % 2
    tm = u_ref.shape[0]
    n_chunks = sorted_ref.shape[1] // tm
    digits = jnp.concatenate(_split_rows(row_ref[...]), axis=0)
    lower = bounds_ref[0:1, :]
    upper = bounds_ref[1:2, :]
    row_e = lax.broadcasted_iota(jnp.int32, (tm, EXPERT_PAD), 0).astype(jnp.float32)
    row_t = lax.broadcasted_iota(jnp.int32, (tm, tm), 0).astype(jnp.float32)

    def sort_chunk(j):
        r0 = float(j * tm)
        in_run = jnp.logical_and(row_e + r0 >= lower, row_e + r0 < upper)
        owner = jnp.concatenate([jnp.where(in_run, ROW_SPLIT, 0.0),
                                 jnp.where(in_run, 1.0, 0.0)], axis=1).astype(jnp.bfloat16)
        want = jnp.dot(owner, digits, preferred_element_type=jnp.float32)
        perm = jnp.where(want == row_t + (r0 + 1.0), 1.0, 0.0).astype(jnp.bfloat16)
        sorted_ref[slot, j * tm:(j + 1) * tm, :] = jnp.dot(
            perm, u_ref[...], preferred_element_type=jnp.float32).astype(jnp.bfloat16)

    for j in range(n_chunks):
        if (j + 1) * tm <= TOP_K * tm:
            sort_chunk(j)
        else:
            pl.when(used_ref[i] > j * tm)(functools.partial(sort_chunk, j))

    def run_copy(buf):
        def make(src, dst, piece):
            return pltpu.make_async_copy(sorted_ref.at[buf, pl.ds(src, piece), :],
                                         xs_hbm.at[pl.ds(dst, piece), :], sems.at[buf])
        return make

    _for_each_piece(i, pn_ref, psrc_ref, pdst_ref, run_copy(slot), start=True)

    @pl.when(i > 0)
    def _():
        _for_each_piece(i - 1, pn_ref, psrc_ref, pdst_ref, run_copy(1 - slot), start=False)

    @pl.when(i == last)
    def _():
        _for_each_piece(i, pn_ref, psrc_ref, pdst_ref, run_copy(slot), start=False)
        zero_ref[...] = jnp.zeros_like(zero_ref)
        blk = zero_ref.shape[0]
        n_blk_total = xs_hbm.shape[0] // blk

        def zero_copy(dst, piece):
            return pltpu.make_async_copy(zero_ref.at[pl.ds(0, piece), :],
                                         xs_hbm.at[pl.ds(dst, piece), :], sems.at[0])

        def tail_body(start):
            def body(e, carry):
                dst = pl.multiple_of(tail_ref[e], ROW_ALIGN)

                def piece_copy(done, piece):
                    cp = zero_copy(pl.multiple_of(dst + done, ROW_ALIGN), piece)
                    if start:
                        cp.start()
                    else:
                        cp.wait()

                _run_copies(tail_ref[N_EXPERTS + e], piece_copy, TAIL_PIECES)
                return carry
            return body

        def blk_body(start):
            def body(b, carry):
                cp = zero_copy(pl.multiple_of(b * blk, blk), blk)
                if start:
                    cp.start()
                else:
                    cp.wait()
                return carry
            return body

        lax.fori_loop(0, N_EXPERTS, tail_body(True), 0)
        lax.fori_loop(nblk_ref[0], n_blk_total, blk_body(True), 0)
        lax.fori_loop(0, N_EXPERTS, tail_body(False), 0)
        lax.fori_loop(nblk_ref[0], n_blk_total, blk_body(False), 0)


def _dispatch(u2b, row1, bounds, pieces, tail, n_used_blk, used, n_blk):
    tm = MOE_TILE
    d = D_MODEL
    n_tok = u2b.shape[0]
    grid_spec = pltpu.PrefetchScalarGridSpec(
        num_scalar_prefetch=6,
        grid=(n_tok // tm,),
        in_specs=[pl.BlockSpec((tm, d), lambda i, *_: (i, 0)),
                  pl.BlockSpec((None, EXPERT_PAD, tm), lambda i, *_: (i, 0, 0)),
                  pl.BlockSpec((None, 2, EXPERT_PAD), lambda i, *_: (i, 0, 0))],
        out_specs=pl.BlockSpec(memory_space=pl.ANY),
        scratch_shapes=[pltpu.VMEM((2, SORTED_ROWS, d), jnp.bfloat16),
                        pltpu.VMEM((MOE_BLK, d), jnp.bfloat16),
                        pltpu.SemaphoreType.DMA((2,))],
    )
    return pl.pallas_call(
        _dispatch_kernel,
        out_shape=jax.ShapeDtypeStruct((n_blk * MOE_BLK, d), jnp.bfloat16),
        grid_spec=grid_spec,
        compiler_params=_cparams(1),
        name="moe_dispatch",
    )(*pieces, tail, n_used_blk, used, u2b, row1, bounds)


def _expert_kernel(blk_e_ref, blk_valid_ref, x_ref, w13_ref, w2_ref, y_ref):
    del blk_e_ref
    i = pl.program_id(0)

    @pl.when(blk_valid_ref[i] > 0)
    def _():
        h = jnp.dot(x_ref[...], w13_ref[...], preferred_element_type=jnp.float32)
        hb = (_silu(h[:, :EXPERT_FF]) * h[:, EXPERT_FF:]).astype(jnp.bfloat16)
        y_ref[...] = jnp.dot(hb, w2_ref[...],
                             preferred_element_type=jnp.float32).astype(jnp.bfloat16)

    @pl.when(blk_valid_ref[i] == 0)
    def _():
        y_ref[...] = jnp.zeros_like(y_ref)


def _experts(xs, blk_e, blk_valid, w13, w2, n_blk):
    blk = MOE_BLK
    d = D_MODEL
    grid_spec = pltpu.PrefetchScalarGridSpec(
        num_scalar_prefetch=2,
        grid=(n_blk,),
        in_specs=[
            pl.BlockSpec((blk, d), lambda i, be, bv: (i, 0)),
            pl.BlockSpec((None, d, 2 * EXPERT_FF), lambda i, be, bv: (be[i], 0, 0)),
            pl.BlockSpec((None, EXPERT_FF, d), lambda i, be, bv: (be[i], 0, 0)),
        ],
        out_specs=pl.BlockSpec((blk, d), lambda i, be, bv: (i, 0)),
    )
    return pl.pallas_call(
        _expert_kernel,
        out_shape=jax.ShapeDtypeStruct((n_blk * blk, d), jnp.bfloat16),
        grid_spec=grid_spec,
        compiler_params=_cparams(1),
        name="moe_experts",
    )(blk_e, blk_valid, xs, w13, w2)


def _final_kernel(pn_ref, psrc_ref, pdst_ref, used_ref, rowt_ref, gatet_ref, cnt_tab_ref,
                  off_tab_ref, y_hbm, x_ref, u_ref, g2_ref, lng_ref, lnb_ref, ws1_ref, ws3_ref,
                  ws2_ref, o_ref, sorted_ref, acc_ref, sems, *, tiles_per_seq, n_batch, alpha):
    i = pl.program_id(0)
    last = pl.num_programs(0) - 1
    slot = i % 2
    bidx = jnp.minimum(i // tiles_per_seq, n_batch)
    tm = x_ref.shape[0]
    n_chunks = sorted_ref.shape[1] // tm

    def run_copy(buf):
        def make(src, dst, piece):
            return pltpu.make_async_copy(y_hbm.at[pl.ds(dst, piece), :],
                                         sorted_ref.at[buf, pl.ds(src, piece), :], sems.at[buf])
        return make

    @pl.when(i == 0)
    def _():
        sorted_ref[...] = jnp.zeros_like(sorted_ref)
        _for_each_piece(i, pn_ref, psrc_ref, pdst_ref, run_copy(slot), start=True)

    @pl.when(i < last)
    def _():
        _for_each_piece(i + 1, pn_ref, psrc_ref, pdst_ref, run_copy(1 - slot), start=True)

    _for_each_piece(i, pn_ref, psrc_ref, pdst_ref, run_copy(slot), start=False)

    digits = jnp.concatenate(_split_rows(rowt_ref[...]), axis=1)
    gate = gatet_ref[...]
    lower = off_tab_ref[:, 0:1]
    upper = lower + cnt_tab_ref[:, 0:1]
    lane_e = lax.broadcasted_iota(jnp.int32, (EXPERT_PAD, tm), 1).astype(jnp.float32)
    lane_t = lax.broadcasted_iota(jnp.int32, (tm, tm), 1).astype(jnp.float32)

    def chunk_sum(j):
        r0 = float(j * tm)
        in_run = jnp.logical_and(lane_e + r0 >= lower, lane_e + r0 < upper)
        owner = jnp.where(in_run, 1.0, 0.0).astype(jnp.bfloat16)
        owner_digits = jnp.concatenate([jnp.where(in_run, ROW_SPLIT, 0.0).astype(jnp.bfloat16),
                                        owner], axis=0)
        want = jnp.dot(digits, owner_digits, preferred_element_type=jnp.float32)
        weight = jnp.dot(gate, owner, preferred_element_type=jnp.float32)
        comb = jnp.where(want == lane_t + (r0 + 1.0), weight, 0.0).astype(jnp.bfloat16)
        return jnp.dot(comb, sorted_ref[slot, j * tm:(j + 1) * tm, :],
                       preferred_element_type=jnp.float32)

    always = TOP_K
    routed = chunk_sum(0)
    for j in range(1, always):
        routed = routed + chunk_sum(j)
    acc_ref[...] = routed
    for j in range(always, n_chunks):
        @pl.when(used_ref[i] > j * tm)
        def _(j=j):
            acc_ref[...] += chunk_sum(j)
    routed = acc_ref[...]
    u = u_ref[...]
    h1 = jnp.dot(u, ws1_ref[...], preferred_element_type=jnp.float32)
    h3 = jnp.dot(u, ws3_ref[...], preferred_element_type=jnp.float32)
    shared = jnp.dot((_silu(h1) * h3).astype(jnp.bfloat16), ws2_ref[...],
                     preferred_element_type=jnp.float32)
    r = alpha * x_ref[...] + g2_ref[pl.ds(bidx, 1), :] * (routed + shared)
    o_ref[...] = _layer_norm_f32(r) * lng_ref[...] + lnb_ref[...]


def _final(pieces, used, row1_t, gate_t, cnt_tab, off_tab, y_slots, x_mid, u2b, mod,
           ln_g, ln_b, ws1, ws3, ws2, n_rows, seq_len, n_batch, alpha):
    tm = MOE_TILE
    d = D_MODEL
    row = lambda i, *_: (i, 0)
    const = lambda i, *_: (0, 0)
    tab_spec = pl.BlockSpec((None, EXPERT_PAD, LANES), lambda i, *_: (i, 0, 0))
    grid_spec = pltpu.PrefetchScalarGridSpec(
        num_scalar_prefetch=4,
        grid=(n_rows // tm,),
        in_specs=[
            pl.BlockSpec((tm, EXPERT_PAD), row),
            pl.BlockSpec((tm, EXPERT_PAD), row),
            tab_spec,
            tab_spec,
            pl.BlockSpec(memory_space=pl.ANY),
            pl.BlockSpec((tm, d), row),
            pl.BlockSpec((tm, d), row),
            pl.BlockSpec((8, d), lambda i, *_: (0, 5)),
            pl.BlockSpec((1, d), const),
            pl.BlockSpec((1, d), const),
            pl.BlockSpec((d, SHARED_FF), const),
            pl.BlockSpec((d, SHARED_FF), const),
            pl.BlockSpec((SHARED_FF, d), const),
        ],
        out_specs=pl.BlockSpec((tm, d), row),
        scratch_shapes=[pltpu.VMEM((2, SORTED_ROWS, d), jnp.bfloat16),
                        pltpu.VMEM((tm, d), jnp.float32),
                        pltpu.SemaphoreType.DMA((2,))],
    )
    return pl.pallas_call(
        functools.partial(_final_kernel, tiles_per_seq=seq_len // tm, n_batch=n_batch, alpha=alpha),
        out_shape=jax.ShapeDtypeStruct((n_rows, d), jnp.float32),
        grid_spec=grid_spec,
        compiler_params=_cparams(1),
        name="moe_combine_postnorm",
    )(*pieces, used, row1_t, gate_t, cnt_tab, off_tab, y_slots, x_mid, u2b, mod,
      ln_g, ln_b, ws1, ws3, ws2)


def _moe_sublayer(x_mid, u2, u2b, mod, ln_g, ln_b, router_w, router_bias, w1, w3, w2,
                  ws1, ws3, ws2, n_rows, seq_len, n_batch, alpha):
    bf = jnp.bfloat16
    blk = MOE_BLK
    n_tiles = n_rows // MOE_TILE
    row1, row1_t, gate_t, cnt_tab, off_tab, base_tab, total = _router(
        u2, router_w, router_bias, n_rows)
    as_int = lambda t: t[:, :N_EXPERTS, 0].astype(jnp.int32)
    cnt, off, base = as_int(cnt_tab), as_int(off_tab), as_int(base_tab)
    used = off[:, -1] + cnt[:, -1]
    bounds = jnp.stack([off_tab[:, :, 0], off_tab[:, :, 0] + cnt_tab[:, :, 0]], axis=1)
    total = total.reshape(N_EXPERTS).astype(jnp.int32)
    padded = (total + blk - 1) // blk * blk
    pends = jnp.cumsum(padded)
    pstart = pends - padded
    dst = pstart[None, :] + base
    tail = jnp.concatenate([pstart + total, padded - total])
    n_used_blk = (pends[-1:] // blk).astype(jnp.int32)
    max_rows = n_rows * TOP_K + n_tiles * N_EXPERTS * (ROW_ALIGN - 1) + N_EXPERTS * (blk - 1)
    n_blk = -(-max_rows // blk)
    blk_start = jnp.arange(n_blk, dtype=jnp.int32) * blk
    blk_e = jnp.minimum(jnp.sum((blk_start[:, None] >= pends[None, :]).astype(jnp.int32), axis=1),
                        N_EXPERTS - 1)
    blk_valid = (blk_start < pends[-1]).astype(jnp.int32)
    pieces = _piece_lists(cnt, off, dst)
    xs = _dispatch(u2b, row1, bounds, pieces, tail, n_used_blk, used, n_blk)
    w13 = jnp.concatenate([w1.astype(bf), w3.astype(bf)], axis=-1)
    y_slots = _experts(xs, blk_e, blk_valid, w13, w2.astype(bf), n_blk)
    return _final(pieces, used, row1_t, gate_t, cnt_tab, off_tab, y_slots, x_mid, u2b, mod,
                  ln_g, ln_b, ws1.astype(bf), ws3.astype(bf), ws2.astype(bf),
                  n_rows, seq_len, n_batch, alpha)


def _pack_in_proj(w_in):
    d = w_in.shape[0]
    w_dt = jnp.zeros((d, P_FOUR - P_DT), w_in.dtype)
    for direction, lane0 in ((0, 0), (1, DT_BWD_LANE)):
        src = OFF_DT + direction * SSM_HEADS
        w_dt = w_dt.at[:, lane0:lane0 + SSM_HEADS].set(w_in[:, src:src + SSM_HEADS])
    return jnp.concatenate([
        w_in[:, OFF_XBC:OFF_DT], w_dt, w_in[:, OFF_FOUR:OFF_GATE],
        w_in[:, OFF_GATE:], w_in[:, :OFF_XBC]], axis=1).astype(jnp.bfloat16)


def _pack_head_rows(v):
    out = jnp.zeros((1, LANES), jnp.float32)
    v = v.reshape(2, SSM_HEADS).astype(jnp.float32)
    out = out.at[0, 0:SSM_HEADS].set(v[0])
    return out.at[0, DT_BWD_LANE:DT_BWD_LANE + SSM_HEADS].set(v[1])


def kernel(x, c, ctx, c_ctx, w_ada, b_ada, w_in, conv_w, conv_b, dt_bias, a_log, d_skip,
           ssm_norm_w, w_br_ssm, w_br_four, w_out, ln1_g, ln1_b, ln2_g, ln2_b,
           router_w, router_bias, w1, w3, w2, ws1, ws3, ws2):
    n_batch, seq_len, d = x.shape
    ctx_len = ctx.shape[1]
    depth = w_ada.shape[0]
    bf = jnp.bfloat16
    alpha = float((2 * depth) ** 0.25)
    n_lat = n_batch * seq_len
    assert d == D_MODEL and n_batch + 1 <= 8
    assert seq_len % K1_TM == 0 and (n_batch * ctx_len) % K1_TM == 0
    assert seq_len % (LANES * 8) == 0

    x_all = _assemble_stream(x, ctx)
    c_rows = jnp.zeros((8, d), jnp.float32).at[:n_batch].set(c).at[n_batch].set(c_ctx)
    mod_all = _ada_mod(c_rows, w_ada, b_ada)
    tables = _dft_tables(seq_len)

    for i in range(depth):
        last = i == depth - 1
        mod = mod_all[i]
        p_all = _in_proj(x_all, mod, _pack_in_proj(w_in[i]), seq_len, n_batch)
        a_row = _pack_head_rows(-jnp.exp(a_log[i].astype(jnp.float32)))
        xbc, dt_all, cum_all, cumt_all, bt_all = _conv(
            p_all, conv_w[i], conv_b[i], _pack_head_rows(dt_bias[i]), a_row,
            seq_len, n_lat, ctx_len)
        yf, yb = _ssd(xbc, dt_all, cum_all, cumt_all, bt_all, n_batch, seq_len, ctx_len)
        f_all = _fourier_latent(p_all, n_batch, seq_len, tables)
        n_rows = n_lat if last else x_all.shape[0]
        if not last:
            f_all = jnp.concatenate(
                [f_all, _fourier_ctx(p_all, n_batch, seq_len, ctx_len, tables)], axis=0)
        d_cols = jnp.repeat(d_skip[i].astype(jnp.float32), SSM_HEAD_DIM).reshape(1, D_INNER)
        x_mid, u2, u2b = _merge(p_all, xbc, yf, yb, f_all, x_all, mod, d_cols,
                           ssm_norm_w[i].reshape(1, D_INNER), ln1_g[i].reshape(1, d),
                           ln1_b[i].reshape(1, d), w_br_ssm[i].astype(bf),
                           w_br_four[i].astype(bf), w_out[i].astype(bf),
                           n_rows, seq_len, n_batch, alpha)
        x_all = _moe_sublayer(x_mid, u2, u2b, mod, ln2_g[i].reshape(1, d), ln2_b[i].reshape(1, d),
                              router_w[i], router_bias[i], w1[i], w3[i], w2[i],
                              ws1[i], ws3[i], ws2[i], n_rows, seq_len, n_batch, alpha)
    return x_all[:n_lat].reshape(n_batch, seq_len, d)
```

```python
import functools

import jax
import jax.numpy as jnp
import numpy as np
from jax import lax
from jax.experimental import pallas as pl
from jax.experimental.pallas import tpu as pltpu

D_MODEL = 1024
GRID_W = 64
POS_BASE = 10000.0
LN_EPS = 1e-6

SSM_HEADS = 24
SSM_HEAD_DIM = 64
D_INNER = SSM_HEADS * SSM_HEAD_DIM
SSM_GROUPS = 4
HEADS_PER_GROUP = SSM_HEADS // SSM_GROUPS
SSM_STATE = 128
CONV_W = 5
CONV_CH = D_INNER + 2 * SSM_GROUPS * SSM_STATE
CHUNK = 128

FOURIER_GROUPS = 4
FOURIER_GROUP_DIM = 256
FOURIER_W = FOURIER_GROUPS * FOURIER_GROUP_DIM

OFF_XBC = D_INNER
OFF_DT = OFF_XBC + CONV_CH
OFF_FOUR = OFF_DT + 2 * SSM_HEADS
OFF_GATE = OFF_FOUR + FOURIER_W

N_EXPERTS = 64
TOP_K = 8
N_EXPERT_GROUPS = 8
EXPERTS_PER_GROUP = N_EXPERTS // N_EXPERT_GROUPS
TOPK_GROUPS = 4
EXPERT_FF = 256
SHARED_FF = 256
ROUTE_SCALE = 2.5

LANES = 128
VMEM_LIMIT_BYTES = 56 * 1024 * 1024

P_XBC = 0
P_DT = CONV_CH
P_FOUR = P_DT + SSM_GROUPS * LANES
P_GATE = P_FOUR + FOURIER_W
P_Z = P_GATE + 2 * D_MODEL
P_WIDTH = P_Z + D_INNER
DT_BWD_LANE = 32
CUMT_ROWS = 64

DFT2_ROWS_PER_STEP = 4
K1_TM = 512
K1_TN = 1536
ROW_TILE = 256
MOE_TILE = 512
MOE_BLK = 512
ROW_ALIGN = 16
EXPERT_PAD = LANES
ROW_SPLIT = 64.0
RUN_PIECES = tuple(MOE_TILE >> s for s in range(6))
TAIL_PIECES = tuple(p for p in RUN_PIECES if p < MOE_BLK)
SORTED_ROWS = -(-(MOE_TILE * TOP_K + N_EXPERTS * (ROW_ALIGN - 1)) // MOE_TILE) * MOE_TILE


def _cparams(n_axes=1):
    return pltpu.CompilerParams(
        dimension_semantics=("arbitrary",) * n_axes,
        vmem_limit_bytes=VMEM_LIMIT_BYTES)


def _layer_norm_f32(x):
    mu = jnp.mean(x, axis=-1, keepdims=True)
    xc = x - mu
    var = jnp.mean(xc * xc, axis=-1, keepdims=True)
    return xc * lax.rsqrt(var + LN_EPS)


def _sigmoid(x):
    return 1.0 / (1.0 + jnp.exp(-x))


def _silu(x):
    return x * _sigmoid(x)


def _pos_kernel(x_ref, ctx_ref, er_ref, ec_ref, o_ref, *, n_lat):
    i = pl.program_id(0)
    half = D_MODEL // 2

    @pl.when(i < n_lat)
    def _():
        ec = ec_ref[...]
        for r in range(8):
            rows = slice(GRID_W * r, GRID_W * (r + 1))
            o_ref[rows, :half] = x_ref[rows, :half] + er_ref[r:r + 1, :]
            o_ref[rows, half:] = x_ref[rows, half:] + ec

    @pl.when(i >= n_lat)
    def _():
        o_ref[...] = ctx_ref[...]


def _assemble_stream(x, ctx):
    b, l, d = x.shape
    lc = ctx.shape[1]
    tile = 8 * GRID_W
    n_lat = (b * l) // tile
    n_ctx = (b * lc) // tile
    rows = l // GRID_W
    quarter = D_MODEL // 4
    omega = 1.0 / (POS_BASE ** (jnp.arange(quarter, dtype=jnp.float32) / quarter))
    ang_r = jnp.arange(rows, dtype=jnp.float32)[:, None] * omega
    ang_c = jnp.arange(GRID_W, dtype=jnp.float32)[:, None] * omega
    emb_r = jnp.concatenate([jnp.sin(ang_r), jnp.cos(ang_r)], -1)
    emb_c = jnp.concatenate([jnp.sin(ang_c), jnp.cos(ang_c)], -1)
    tiles_per_seq = l // tile
    return pl.pallas_call(
        functools.partial(_pos_kernel, n_lat=n_lat),
        out_shape=jax.ShapeDtypeStruct((b * l + b * lc, d), jnp.float32),
        grid=(n_lat + n_ctx,),
        in_specs=[
            pl.BlockSpec((tile, d), lambda i: (jnp.minimum(i, n_lat - 1), 0)),
            pl.BlockSpec((tile, d), lambda i: (jnp.maximum(i - n_lat, 0), 0)),
            pl.BlockSpec((8, d // 2), lambda i: (i % tiles_per_seq, 0)),
            pl.BlockSpec((GRID_W, d // 2), lambda i: (0, 0)),
        ],
        out_specs=pl.BlockSpec((tile, d), lambda i: (i, 0)),
        compiler_params=_cparams(1),
        name="assemble_stream",
    )(x.reshape(b * l, d), ctx.reshape(b * lc, d), emb_r, emb_c)


def _ada_kernel(c_ref, w_ref, b_ref, o_ref):
    c = c_ref[...]
    o_ref[...] = jnp.dot(_silu(c), w_ref[...], preferred_element_type=jnp.float32,
                         precision=lax.Precision.HIGHEST) + b_ref[...]


def _ada_mod(c_rows, w_ada, b_ada):
    depth, d, n6 = w_ada.shape
    tn = 1536
    return pl.pallas_call(
        _ada_kernel,
        out_shape=jax.ShapeDtypeStruct((depth, 8, n6), jnp.float32),
        grid=(depth, n6 // tn),
        in_specs=[
            pl.BlockSpec((8, d), lambda a, j: (0, 0)),
            pl.BlockSpec((None, d, tn), lambda a, j: (a, 0, j)),
            pl.BlockSpec((None, 1, tn), lambda a, j: (a, 0, j)),
        ],
        out_specs=pl.BlockSpec((None, 8, tn), lambda a, j: (a, 0, j)),
        compiler_params=_cparams(2),
        name="ada_mod",
    )(c_rows, w_ada, b_ada.reshape(depth, 1, n6))


def _k1_kernel(x_ref, sc_ref, sh_ref, w_ref, o_ref, u_ref, *, tiles_per_seq, n_batch):
    i = pl.program_id(0)
    bidx = jnp.minimum(i // tiles_per_seq, n_batch)
    xn = _layer_norm_f32(x_ref[...])
    u = xn * (1.0 + sc_ref[pl.ds(bidx, 1), :]) + sh_ref[pl.ds(bidx, 1), :]
    u_ref[...] = u.astype(jnp.bfloat16)
    for c in range(o_ref.shape[1] // K1_TN):
        cols = slice(c * K1_TN, (c + 1) * K1_TN)
        o_ref[:, cols] = jnp.dot(u_ref[...], w_ref[:, cols],
                                 preferred_element_type=jnp.float32).astype(jnp.bfloat16)


def _in_proj(x_all, mod, w_all, seq_len, n_batch):
    nt, d = x_all.shape
    tm = K1_TM
    return pl.pallas_call(
        functools.partial(_k1_kernel, tiles_per_seq=seq_len // tm, n_batch=n_batch),
        out_shape=jax.ShapeDtypeStruct((nt, P_WIDTH), jnp.bfloat16),
        grid=(nt // tm,),
        in_specs=[
            pl.BlockSpec((tm, d), lambda i: (i, 0)),
            pl.BlockSpec((8, d), lambda i: (0, 1)),
            pl.BlockSpec((8, d), lambda i: (0, 0)),
            pl.BlockSpec((d, P_WIDTH), lambda i: (0, 0), pipeline_mode=pl.Buffered(1)),
        ],
        out_specs=pl.BlockSpec((tm, P_WIDTH), lambda i: (i, 0)),
        scratch_shapes=[pltpu.VMEM((tm, d), jnp.bfloat16)],
        compiler_params=_cparams(1),
        name="in_proj",
    )(x_all, mod, mod, w_all)


def _softplus(x):
    return jnp.maximum(x, 0.0) + jnp.log1p(jnp.exp(-jnp.abs(x)))


def _conv_kernel(cur_ref, prev_ref, next_ref, w_ref, b_ref, dtraw_ref, bias_ref, a_ref,
                 o_ref, dt_ref, cum_ref, cumt_ref, bt_ref, ext_ref, *, tiles_per_seq, n_lat):
    i = pl.program_id(0)
    is_ctx = i >= n_lat
    is_start = jnp.logical_or(i % tiles_per_seq == 0, is_ctx)
    is_end = jnp.logical_or(i % tiles_per_seq == tiles_per_seq - 1, is_ctx)
    halo = prev_ref.shape[0]
    rows = cur_ref.shape[0]
    ext_ref[0:halo, :] = jnp.where(is_start, 0.0, prev_ref[...].astype(jnp.float32))
    ext_ref[halo:halo + rows, :] = cur_ref[...].astype(jnp.float32)
    ext_ref[halo + rows:, :] = jnp.where(is_end, 0.0, next_ref[...].astype(jnp.float32))

    def conv_lane_block(c, carry):
        lanes = pl.ds(pl.multiple_of(c * LANES, LANES), LANES)
        acc = b_ref[:, lanes] + w_ref[0:1, lanes] * ext_ref[pl.ds(halo - 2, rows), lanes]
        for k in range(1, CONV_W):
            acc = acc + w_ref[k:k + 1, lanes] * ext_ref[pl.ds(halo - 2 + k, rows), lanes]
        o_ref[:, lanes] = _silu(acc).astype(jnp.bfloat16)
        return carry

    lax.fori_loop(0, CONV_CH // LANES, conv_lane_block, 0)

    q = CHUNK
    dt = _softplus(dtraw_ref[...].astype(jnp.float32) + bias_ref[...])
    dt_ref[...] = dt
    adt = dt * a_ref[...]
    row_i = lax.broadcasted_iota(jnp.int32, (q, q), 0)
    col_i = lax.broadcasted_iota(jnp.int32, (q, q), 1)
    lower = (row_i >= col_i).astype(jnp.float32)
    upper = (row_i <= col_i).astype(jnp.float32)
    is_fwd = lax.broadcasted_iota(jnp.int32, (1, LANES), 1) < DT_BWD_LANE
    for ch in range(rows // q):
        rs = slice(q * ch, q * (ch + 1))
        cum_f = jnp.dot(lower, adt[rs, :], preferred_element_type=jnp.float32,
                        precision=lax.Precision.HIGHEST)
        cum_b = jnp.dot(upper, adt[rs, :], preferred_element_type=jnp.float32,
                        precision=lax.Precision.HIGHEST)
        cum = jnp.where(is_fwd, cum_f, cum_b)
        cum_ref[rs, :] = cum
        cumt_ref[ch] = cum.T[:CUMT_ROWS, :]
        for g in range(SSM_GROUPS):
            b_cols = slice(D_INNER + SSM_STATE * g, D_INNER + SSM_STATE * (g + 1))
            bt_ref[ch, SSM_STATE * g:SSM_STATE * (g + 1), :] = (
                o_ref[rs, b_cols].astype(jnp.float32).T.astype(jnp.bfloat16))


def _conv(p_all, conv_w, conv_b, dt_bias_row, a_row, seq_len, n_lat_rows, ctx_len):
    nt = p_all.shape[0]
    tl = ROW_TILE
    assert ctx_len == tl, "context sequences must span exactly one conv tile"
    halo = 16
    hb = tl // halo
    n_halo_blocks = nt // halo
    dtw = LANES
    cpt = tl // CHUNK
    w8 = jnp.zeros((8, CONV_CH), jnp.float32).at[:CONV_W].set(conv_w)
    return pl.pallas_call(
        functools.partial(_conv_kernel, tiles_per_seq=seq_len // tl, n_lat=n_lat_rows // tl),
        out_shape=(jax.ShapeDtypeStruct((nt, CONV_CH), jnp.bfloat16),
                   jax.ShapeDtypeStruct((nt, dtw), jnp.float32),
                   jax.ShapeDtypeStruct((nt, dtw), jnp.float32),
                   jax.ShapeDtypeStruct((nt // CHUNK, CUMT_ROWS, CHUNK), jnp.float32),
                   jax.ShapeDtypeStruct((nt // CHUNK, SSM_GROUPS * SSM_STATE, CHUNK), jnp.bfloat16)),
        grid=(nt // tl,),
        in_specs=[
            pl.BlockSpec((tl, CONV_CH), lambda i: (i, 0)),
            pl.BlockSpec((halo, CONV_CH), lambda i: (jnp.maximum(i * hb - 1, 0), 0)),
            pl.BlockSpec((halo, CONV_CH),
                         lambda i: (jnp.minimum((i + 1) * hb, n_halo_blocks - 1), 0)),
            pl.BlockSpec((8, CONV_CH), lambda i: (0, 0)),
            pl.BlockSpec((1, CONV_CH), lambda i: (0, 0)),
            pl.BlockSpec((tl, dtw), lambda i: (i, P_DT // dtw)),
            pl.BlockSpec((1, dtw), lambda i: (0, 0)),
            pl.BlockSpec((1, dtw), lambda i: (0, 0)),
        ],
        out_specs=(pl.BlockSpec((tl, CONV_CH), lambda i: (i, 0)),
                   pl.BlockSpec((tl, dtw), lambda i: (i, 0)),
                   pl.BlockSpec((tl, dtw), lambda i: (i, 0)),
                   pl.BlockSpec((cpt, CUMT_ROWS, CHUNK), lambda i: (i, 0, 0)),
                   pl.BlockSpec((cpt, SSM_GROUPS * SSM_STATE, CHUNK), lambda i: (i, 0, 0))),
        scratch_shapes=[pltpu.VMEM((tl + 2 * halo, CONV_CH), jnp.float32)],
        compiler_params=_cparams(1),
        name="dwconv_silu",
    )(p_all, p_all, p_all, w8, conv_b.reshape(1, CONV_CH), p_all, dt_bias_row, a_row)


def _ssd_direction(refs, s_ref, y_ref, gi, *, lane0, forward):
    x_ref, b_ref, c_ref, dt_ref, cum_ref, cumt_ref, bt_ref = refs
    q = CHUNK
    gw = HEADS_PER_GROUP * SSM_HEAD_DIM
    row_i = lax.broadcasted_iota(jnp.int32, (q, q), 0)
    col_i = lax.broadcasted_iota(jnp.int32, (q, q), 1)
    tri = (row_i >= col_i) if forward else (row_i <= col_i)
    lane_lo = lax.broadcasted_iota(jnp.int32, (q, LANES), 1) < SSM_HEAD_DIM
    lane_lo_row = lax.broadcasted_iota(jnp.int32, (1, LANES), 1) < SSM_HEAD_DIM

    g_lanes = slice(LANES * gi, LANES * (gi + 1))
    dt = dt_ref[...]
    cum = cum_ref[...]
    cum_t = cumt_ref[...]
    total = cum[q - 1:q, :] if forward else cum[0:1, :]

    cm = c_ref[:, g_lanes]
    bm = b_ref[:, g_lanes]
    scores = lax.dot_general(cm, bm, (((1,), (1,)), ((), ())),
                             preferred_element_type=jnp.float32)
    bm_t = bt_ref[SSM_STATE * gi:SSM_STATE * (gi + 1), :]

    for pr in range(HEADS_PER_GROUP // 2):
        r0 = lane0 + 2 * pr
        r1 = r0 + 1
        lanes = slice(gw * gi + LANES * pr, gw * gi + LANES * (pr + 1))
        xp = x_ref[:, lanes].astype(jnp.float32)
        dt_pair = jnp.where(lane_lo, dt[:, r0:r0 + 1], dt[:, r1:r1 + 1])
        cum_pair = jnp.where(lane_lo, cum[:, r0:r0 + 1], cum[:, r1:r1 + 1])
        tot_pair = jnp.where(lane_lo_row, total[:, r0:r0 + 1], total[:, r1:r1 + 1])
        xdt = xp * dt_pair
        l0 = jnp.exp(jnp.where(tri, cum[:, r0:r0 + 1] - cum_t[r0:r0 + 1, :], -jnp.inf))
        l1 = jnp.exp(jnp.where(tri, cum[:, r1:r1 + 1] - cum_t[r1:r1 + 1, :], -jnp.inf))
        w = jnp.concatenate([(scores * l0).astype(jnp.bfloat16),
                             (scores * l1).astype(jnp.bfloat16)], axis=1)
        xdt_b = xdt.astype(jnp.bfloat16)
        zero = jnp.zeros_like(xdt_b)
        rhs = jnp.concatenate([jnp.where(lane_lo, xdt_b, zero),
                               jnp.where(lane_lo, zero, xdt_b)], axis=0)
        y_diag = jnp.dot(w, rhs, preferred_element_type=jnp.float32)
        s_old = s_ref[:, lanes]
        y_off = jnp.dot(cm, s_old.astype(jnp.bfloat16),
                        preferred_element_type=jnp.float32) * jnp.exp(cum_pair)
        y_ref[:, lanes] = (y_diag + y_off).astype(jnp.bfloat16)
        decayed = (xdt * jnp.exp(tot_pair - cum_pair)).astype(jnp.bfloat16)
        s_ref[:, lanes] = jnp.exp(tot_pair) * s_old + jnp.dot(
            bm_t, decayed, preferred_element_type=jnp.float32)


def _ssd_kernel(*refs):
    n_in = 7
    fwd_refs, bwd_refs = refs[:n_in], refs[n_in:2 * n_in]
    yf_ref, yb_ref, sf_ref, sb_ref = refs[2 * n_in:]

    @pl.when(pl.program_id(2) == 0)
    def _():
        sf_ref[...] = jnp.zeros_like(sf_ref)
        sb_ref[...] = jnp.zeros_like(sb_ref)

    for gi in range(SSM_GROUPS):
        _ssd_direction(fwd_refs, sf_ref, yf_ref, gi, lane0=HEADS_PER_GROUP * gi, forward=True)
        _ssd_direction(bwd_refs, sb_ref, yb_ref, gi,
                       lane0=DT_BWD_LANE + HEADS_PER_GROUP * gi, forward=False)


def _ssd(xbc, dt_all, cum_all, cumt_all, bt_all, n_batch, seq_len, ctx_len):
    nt = xbc.shape[0]
    q = CHUNK
    nc_lat = seq_len // q
    nc_ctx = ctx_len // q
    ctx_blk0 = (n_batch * seq_len) // q
    n_steps = nc_ctx + nc_lat
    gw = D_INNER
    sw = SSM_GROUPS * SSM_STATE
    b_blk0 = D_INNER // sw
    c_blk0 = b_blk0 + 1

    def fwd_row(b, j):
        return jnp.where(j < nc_ctx, ctx_blk0 + nc_ctx * b + j, nc_lat * b + (j - nc_ctx))

    def bwd_row(b, j):
        return jnp.where(j < nc_ctx, ctx_blk0 + nc_ctx * b + (nc_ctx - 1 - j),
                         nc_lat * b + (n_steps - 1 - j))

    def specs(row):
        return [
            pl.BlockSpec((q, gw), lambda b, g, j: (row(b, j), g)),
            pl.BlockSpec((q, sw), lambda b, g, j: (row(b, j), b_blk0 + g)),
            pl.BlockSpec((q, sw), lambda b, g, j: (row(b, j), c_blk0 + g)),
            pl.BlockSpec((q, LANES), lambda b, g, j: (row(b, j), 0)),
            pl.BlockSpec((q, LANES), lambda b, g, j: (row(b, j), 0)),
            pl.BlockSpec((None, CUMT_ROWS, q), lambda b, g, j: (row(b, j), 0, 0)),
            pl.BlockSpec((None, sw, q), lambda b, g, j: (row(b, j), 0, 0)),
        ]

    out_sds = jax.ShapeDtypeStruct((nt, D_INNER), jnp.bfloat16)
    operands = (xbc, xbc, xbc, dt_all, cum_all, cumt_all, bt_all)
    return pl.pallas_call(
        _ssd_kernel,
        out_shape=(out_sds, out_sds),
        grid=(n_batch, 1, n_steps),
        in_specs=specs(fwd_row) + specs(bwd_row),
        out_specs=(pl.BlockSpec((q, gw), lambda b, g, j: (fwd_row(b, j), g)),
                   pl.BlockSpec((q, gw), lambda b, g, j: (bwd_row(b, j), g))),
        scratch_shapes=[pltpu.VMEM((SSM_STATE, gw), jnp.float32),
                        pltpu.VMEM((SSM_STATE, gw), jnp.float32)],
        compiler_params=_cparams(3),
        name="ssd_scan",
    )(*operands, *operands)


def _dft_tables(seq_len):
    l1n = seq_len // LANES
    two_pi = 2.0 * np.pi
    gd = FOURIER_GROUP_DIM
    jj = jnp.arange(gd, dtype=jnp.int32)
    ang_c = ((jj[:, None] * jj[None, :]) % gd).astype(jnp.float32) * (two_pi / gd)
    cc, sc = jnp.cos(ang_c), jnp.sin(ang_c)
    k1 = jnp.arange(l1n, dtype=jnp.int32)
    ang1 = ((k1[:, None] * k1[None, :]) % l1n).astype(jnp.float32) * (two_pi / l1n)
    w1 = jnp.concatenate([jnp.cos(ang1), -jnp.sin(ang1)], axis=0)
    k2 = jnp.arange(LANES, dtype=jnp.int32)
    kk = k1[:, None, None] + l1n * k2[None, :, None]
    ang2 = ((kk * k2[None, None, :]) % seq_len).astype(jnp.float32) * (two_pi / seq_len)
    er, ei = jnp.cos(ang2), -jnp.sin(ang2)
    e = jnp.concatenate([jnp.concatenate([er, -ei], axis=2),
                         jnp.concatenate([ei, er], axis=2)], axis=1)
    return cc, sc, w1, e


def _dft1_kernel(w_ref, x_ref, o_ref):
    o_ref[...] = jnp.dot(w_ref[...], x_ref[...],
                         preferred_element_type=jnp.float32).astype(jnp.bfloat16)


def _channel_mix(gr, gi, cs_ref, o_ref):
    gd = FOURIER_GROUP_DIM
    for g in range(FOURIER_GROUPS):
        cols = slice(gd * g, gd * (g + 1))
        lhs = jnp.concatenate([gr[:, cols], gi[:, cols]], axis=1).astype(jnp.bfloat16)
        o_ref[:, cols] = jnp.dot(lhs, cs_ref[...],
                                 preferred_element_type=jnp.float32).astype(jnp.bfloat16)


def _dft2_kernel(e_ref, ar_ref, ai_ref, cs_ref, o_ref):
    for k in range(e_ref.shape[0]):
        a = jnp.concatenate([ar_ref[k], ai_ref[k]], axis=0)
        g = jnp.dot(e_ref[k], a, preferred_element_type=jnp.float32)
        half = g.shape[0] // 2
        _channel_mix(g[:half], g[half:], cs_ref, o_ref.at[k])


def _dft_ctx_kernel(w_ref, x_ref, cs_ref, o_ref):
    g = jnp.dot(w_ref[...], x_ref[...], preferred_element_type=jnp.float32)
    half = g.shape[0] // 2
    _channel_mix(g[:half], g[half:], cs_ref, o_ref)


def _fourier_latent(p_all, n_batch, seq_len, tables):
    cc, sc, w1, e = tables
    c = FOURIER_W
    l1n = seq_len // LANES
    ncol = LANES * c
    four = p_all[:n_batch * seq_len, P_FOUR:P_FOUR + c].reshape(n_batch, l1n, ncol)
    tn = 4096
    a = pl.pallas_call(
        _dft1_kernel,
        out_shape=jax.ShapeDtypeStruct((n_batch, 2 * l1n, ncol), jnp.bfloat16),
        grid=(n_batch, ncol // tn),
        in_specs=[pl.BlockSpec((2 * l1n, l1n), lambda b, j: (0, 0)),
                  pl.BlockSpec((None, l1n, tn), lambda b, j: (b, 0, j))],
        out_specs=pl.BlockSpec((None, 2 * l1n, tn), lambda b, j: (b, 0, j)),
        compiler_params=_cparams(2),
        name="dft_stage1",
    )(w1.astype(jnp.bfloat16), four)
    a4 = a.reshape(n_batch, 2 * l1n, LANES, c)
    norm = 1.0 / np.sqrt(float(seq_len) * FOURIER_GROUP_DIM)
    cs = (jnp.concatenate([cc, sc], axis=0) * norm).astype(jnp.bfloat16)
    kp = DFT2_ROWS_PER_STEP
    o = pl.pallas_call(
        _dft2_kernel,
        out_shape=jax.ShapeDtypeStruct((n_batch, l1n, LANES, c), jnp.bfloat16),
        grid=(n_batch, l1n // kp),
        in_specs=[pl.BlockSpec((kp, 2 * LANES, 2 * LANES), lambda b, k: (k, 0, 0)),
                  pl.BlockSpec((None, kp, LANES, c), lambda b, k: (b, k, 0, 0)),
                  pl.BlockSpec((None, kp, LANES, c), lambda b, k: (b, l1n // kp + k, 0, 0)),
                  pl.BlockSpec((2 * FOURIER_GROUP_DIM, FOURIER_GROUP_DIM), lambda b, k: (0, 0))],
        out_specs=pl.BlockSpec((None, kp, LANES, c), lambda b, k: (b, k, 0, 0)),
        compiler_params=_cparams(2),
        name="dft_stage2",
    )(e.astype(jnp.bfloat16), a4, a4, cs)
    return o.transpose(0, 2, 1, 3).reshape(n_batch * seq_len, c)


def _fourier_ctx(p_all, n_batch, seq_len, ctx_len, tables):
    cc, sc, _, _ = tables
    assert ctx_len == FOURIER_GROUP_DIM
    c = FOURIER_W
    wc = jnp.concatenate([cc, -sc], axis=0).astype(jnp.bfloat16)
    norm = 1.0 / np.sqrt(float(ctx_len) * FOURIER_GROUP_DIM)
    cs = (jnp.concatenate([cc, sc], axis=0) * norm).astype(jnp.bfloat16)
    blk0 = (n_batch * seq_len) // ctx_len
    return pl.pallas_call(
        _dft_ctx_kernel,
        out_shape=jax.ShapeDtypeStruct((n_batch * ctx_len, c), jnp.bfloat16),
        grid=(n_batch,),
        in_specs=[pl.BlockSpec((2 * ctx_len, ctx_len), lambda b: (0, 0)),
                  pl.BlockSpec((ctx_len, c), lambda b: (blk0 + b, P_FOUR // c)),
                  pl.BlockSpec((2 * FOURIER_GROUP_DIM, FOURIER_GROUP_DIM), lambda b: (0, 0))],
        out_specs=pl.BlockSpec((ctx_len, c), lambda b: (b, 0)),
        compiler_params=_cparams(1),
        name="dft_ctx",
    )(wc, p_all, cs)


def _merge_kernel(gate_ref, z_ref, xs_ref, yf_ref, yb_ref, f_ref, x_ref,
                  g1_ref, sc2_ref, sh2_ref, d_ref, nw_ref, lng_ref, lnb_ref,
                  wssm_ref, wfour_ref, wout_ref, xo_ref, u2_ref, u2b_ref,
                  *, tiles_per_seq, n_batch, alpha):
    bidx = jnp.minimum(pl.program_id(0) // tiles_per_seq, n_batch)
    y = (yf_ref[...].astype(jnp.float32) + yb_ref[...].astype(jnp.float32)
         + xs_ref[...].astype(jnp.float32) * d_ref[...])
    h = y * _silu(z_ref[...].astype(jnp.float32))
    h = h * lax.rsqrt(jnp.mean(h * h, axis=-1, keepdims=True) + LN_EPS) * nw_ref[...]
    ssm = jnp.dot(h.astype(jnp.bfloat16), wssm_ref[...], preferred_element_type=jnp.float32)
    four = jnp.dot(f_ref[...], wfour_ref[...], preferred_element_type=jnp.float32)
    gates = _sigmoid(gate_ref[...].astype(jnp.float32))
    merged = gates[:, :D_MODEL] * ssm + gates[:, D_MODEL:] * four
    out = jnp.dot(merged.astype(jnp.bfloat16), wout_ref[...], preferred_element_type=jnp.float32)
    r = alpha * x_ref[...] + g1_ref[pl.ds(bidx, 1), :] * out
    xn = _layer_norm_f32(r) * lng_ref[...] + lnb_ref[...]
    xo_ref[...] = xn
    u2 = (_layer_norm_f32(xn) * (1.0 + sc2_ref[pl.ds(bidx, 1), :])
          + sh2_ref[pl.ds(bidx, 1), :])
    u2_ref[...] = u2
    u2b_ref[...] = u2.astype(jnp.bfloat16)


def _merge(p_all, xbc, yf, yb, f_all, x_all, mod, d_cols, norm_w, ln_g, ln_b,
           w_ssm, w_four, w_out, n_rows, seq_len, n_batch, alpha):
    tm = ROW_TILE
    d = D_MODEL
    row = lambda i: (i, 0)
    const = lambda i: (0, 0)
    out_sds = jax.ShapeDtypeStruct((n_rows, d), jnp.float32)
    return pl.pallas_call(
        functools.partial(_merge_kernel, tiles_per_seq=seq_len // tm, n_batch=n_batch, alpha=alpha),
        out_shape=(out_sds, out_sds, jax.ShapeDtypeStruct((n_rows, d), jnp.bfloat16)),
        grid=(n_rows // tm,),
        in_specs=[
            pl.BlockSpec((tm, 2 * d), lambda i: (i, P_GATE // (2 * d))),
            pl.BlockSpec((tm, D_INNER), lambda i: (i, P_Z // D_INNER)),
            pl.BlockSpec((tm, D_INNER), row),
            pl.BlockSpec((tm, D_INNER), row),
            pl.BlockSpec((tm, D_INNER), row),
            pl.BlockSpec((tm, FOURIER_W), row),
            pl.BlockSpec((tm, d), row),
            pl.BlockSpec((8, d), lambda i: (0, 2)),
            pl.BlockSpec((8, d), lambda i: (0, 4)),
            pl.BlockSpec((8, d), lambda i: (0, 3)),
            pl.BlockSpec((1, D_INNER), const),
            pl.BlockSpec((1, D_INNER), const),
            pl.BlockSpec((1, d), const),
            pl.BlockSpec((1, d), const),
            pl.BlockSpec((D_INNER, d), const),
            pl.BlockSpec((FOURIER_W, d), const),
            pl.BlockSpec((d, d), const),
        ],
        out_specs=(pl.BlockSpec((tm, d), row), pl.BlockSpec((tm, d), row),
                   pl.BlockSpec((tm, d), row)),
        compiler_params=_cparams(1),
        name="merge_postnorm",
    )(p_all, p_all, xbc, yf, yb, f_all, x_all, mod, mod, mod, d_cols, norm_w,
      ln_g, ln_b, w_ssm, w_four, w_out)


def _first_index_of_max(v, iota, big):
    m = jnp.max(v, axis=0, keepdims=True)
    idx = jnp.min(jnp.where(v == m, iota, big), axis=0, keepdims=True)
    return m, idx


def _router_kernel(u_ref, wt_ref, bias_ref, tri_ref, row_ref, rowt_ref, gatet_ref, cnt_ref,
                   off_ref, basetab_ref, tot_ref, base_ref):
    @pl.when(pl.program_id(0) == 0)
    def _():
        base_ref[...] = jnp.zeros_like(base_ref)

    tm = u_ref.shape[0]
    ne, epg = N_EXPERTS, EXPERTS_PER_GROUP
    neg = -jnp.inf
    logits = lax.dot_general(wt_ref[...], u_ref[...], (((1,), (1,)), ((), ())),
                             preferred_element_type=jnp.float32,
                             precision=lax.Precision.HIGHEST)
    scores = _sigmoid(logits)
    sel = scores + bias_ref[...]
    iota_g = lax.broadcasted_iota(jnp.int32, (epg, tm), 0).astype(jnp.float32)
    grp_rows = []
    for g in range(N_EXPERT_GROUPS):
        v = sel[epg * g:epg * (g + 1), :]
        m1, i1 = _first_index_of_max(v, iota_g, epg)
        m2 = jnp.max(jnp.where(iota_g == i1, neg, v), axis=0, keepdims=True)
        grp_rows.append(m1 + m2)
    grp = jnp.concatenate(grp_rows, axis=0)
    iota_n = lax.broadcasted_iota(jnp.int32, (N_EXPERT_GROUPS, tm), 0).astype(jnp.float32)
    chosen = jnp.zeros((N_EXPERT_GROUPS, tm), jnp.float32)
    for _ in range(TOPK_GROUPS):
        _, gi = _first_index_of_max(grp, iota_n, N_EXPERT_GROUPS)
        hit = iota_n == gi
        chosen = jnp.where(hit, 1.0, chosen)
        grp = jnp.where(hit, neg, grp)
    masked = jnp.concatenate(
        [jnp.where(chosen[g:g + 1, :] > 0.0, sel[epg * g:epg * (g + 1), :], neg)
         for g in range(N_EXPERT_GROUPS)], axis=0)
    iota_e = lax.broadcasted_iota(jnp.int32, (ne, tm), 0).astype(jnp.float32)
    picked = jnp.zeros((ne, tm), jnp.float32)
    for _ in range(TOP_K):
        _, ei = _first_index_of_max(masked, iota_e, ne)
        hit = iota_e == ei
        picked = jnp.where(hit, 1.0, picked)
        masked = jnp.where(hit, neg, masked)
    chosen_scores = picked * scores
    gate = chosen_scores / jnp.sum(chosen_scores, axis=0, keepdims=True) * ROUTE_SCALE
    before = jnp.dot(picked.astype(jnp.bfloat16), tri_ref[...],
                     preferred_element_type=jnp.float32)
    cnt = jnp.sum(picked, axis=1, keepdims=True)
    cnt16 = jnp.ceil(cnt * (1.0 / ROW_ALIGN)) * ROW_ALIGN
    cnt16_b = jnp.broadcast_to(cnt16, (ne, LANES))
    e_row = lax.broadcasted_iota(jnp.int32, (ne, ne), 0)
    e_col = lax.broadcasted_iota(jnp.int32, (ne, ne), 1)
    off_b = jnp.dot((e_col < e_row).astype(jnp.float32), cnt16_b,
                    preferred_element_type=jnp.float32,
                    precision=lax.Precision.HIGHEST)
    row1 = picked * (before + off_b[:, 0:1] + 1.0)
    pad = jnp.zeros((EXPERT_PAD - ne, tm), jnp.float32)
    row1_p = jnp.concatenate([row1, pad], axis=0)
    gate_p = jnp.concatenate([gate, pad], axis=0)
    pad_b = jnp.zeros((EXPERT_PAD - ne, LANES), jnp.float32)
    row_ref[...] = row1_p
    rowt_ref[...] = row1_p.T
    gatet_ref[...] = gate_p.T.astype(jnp.bfloat16)
    cnt_ref[...] = jnp.concatenate([cnt16_b, pad_b], axis=0)
    off_ref[...] = jnp.concatenate([off_b, pad_b], axis=0)
    basetab_ref[...] = jnp.broadcast_to(base_ref[...], (ne, LANES))
    base_ref[...] = base_ref[...] + cnt16
    tot_ref[...] = base_ref[...]


def _router(u2, router_w, router_bias, n_tok):
    tm = MOE_TILE
    n_tiles = n_tok // tm
    tri = jnp.triu(jnp.ones((tm, tm), jnp.float32), k=1).astype(jnp.bfloat16)
    tab = lambda rows: (jax.ShapeDtypeStruct((n_tiles, rows, LANES), jnp.float32),
                        pl.BlockSpec((None, rows, LANES), lambda i: (i, 0, 0)))
    outs = [
        (jax.ShapeDtypeStruct((n_tiles, EXPERT_PAD, tm), jnp.float32),
         pl.BlockSpec((None, EXPERT_PAD, tm), lambda i: (i, 0, 0))),
        (jax.ShapeDtypeStruct((n_tok, EXPERT_PAD), jnp.float32),
         pl.BlockSpec((tm, EXPERT_PAD), lambda i: (i, 0))),
        (jax.ShapeDtypeStruct((n_tok, EXPERT_PAD), jnp.bfloat16),
         pl.BlockSpec((tm, EXPERT_PAD), lambda i: (i, 0))),
        tab(EXPERT_PAD), tab(EXPERT_PAD), tab(N_EXPERTS),
        (jax.ShapeDtypeStruct((N_EXPERTS, 1), jnp.float32),
         pl.BlockSpec((N_EXPERTS, 1), lambda i: (0, 0))),
    ]
    return pl.pallas_call(
        _router_kernel,
        out_shape=tuple(o[0] for o in outs),
        grid=(n_tiles,),
        in_specs=[pl.BlockSpec((tm, D_MODEL), lambda i: (i, 0)),
                  pl.BlockSpec((N_EXPERTS, D_MODEL), lambda i: (0, 0)),
                  pl.BlockSpec((N_EXPERTS, 1), lambda i: (0, 0)),
                  pl.BlockSpec((tm, tm), lambda i: (0, 0))],
        out_specs=tuple(o[1] for o in outs),
        scratch_shapes=[pltpu.VMEM((N_EXPERTS, 1), jnp.float32)],
        compiler_params=_cparams(1),
        name="moe_router",
    )(u2, router_w.T, router_bias.reshape(N_EXPERTS, 1), tri)


def _run_copies(cnt, make_copy, pieces):
    done = jnp.int32(0)
    for piece in pieces:
        hit = (cnt & piece) != 0

        @pl.when(hit)
        def _(done=done, piece=piece):
            make_copy(done, piece)

        done = done + (cnt & piece)


def _piece_lists(cnt, off, dst):
    counts, srcs, dsts = [], [], []
    slot_ids = jnp.arange(N_EXPERTS, dtype=jnp.int32)
    for piece in RUN_PIECES:
        has = ((cnt & piece) != 0).astype(jnp.int32)
        above = cnt & ~(2 * piece - 1)
        rank = jnp.cumsum(has, axis=-1) - has
        place = (rank[:, None, :] == slot_ids[None, :, None]).astype(jnp.int32) * has[:, None, :]
        counts.append(jnp.sum(has, axis=-1))
        srcs.append(jnp.sum(place * (off + above)[:, None, :], axis=-1))
        dsts.append(jnp.sum(place * (dst + above)[:, None, :], axis=-1))
    flat = lambda parts: jnp.stack(parts, axis=1).reshape(-1).astype(jnp.int32)
    return flat(counts), flat(srcs), flat(dsts)


def _for_each_piece(tile, pn_ref, psrc_ref, pdst_ref, make_copy, start):
    for p_idx, piece in enumerate(RUN_PIECES):
        seg = tile * len(RUN_PIECES) + p_idx
        base = seg * N_EXPERTS

        def body(k, carry, piece=piece, base=base):
            cp = make_copy(pl.multiple_of(psrc_ref[base + k], ROW_ALIGN),
                           pl.multiple_of(pdst_ref[base + k], ROW_ALIGN), piece)
            if start:
                cp.start()
            else:
                cp.wait()
            return carry

        lax.fori_loop(0, pn_ref[seg], body, 0)


def _split_rows(row1):
    hi = jnp.floor(row1 * (1.0 / ROW_SPLIT))
    lo = row1 - hi * ROW_SPLIT
    return hi.astype(jnp.bfloat16), lo.astype(jnp.bfloat16)


def _dispatch_kernel(pn_ref, psrc_ref, pdst_ref, tail_ref, nblk_ref, used_ref, u_ref, row_ref,
                     bounds_ref, xs_hbm, sorted_ref, zero_ref, sems):
    i = pl.program_id(0)
    last = pl.num_programs(0) - 1
    slot = i % 2
    tm = u_ref.shape[0]
    n_chunks = sorted_ref.shape[1] // tm
    digits = jnp.concatenate(_split_rows(row_ref[...]), axis=0)
    lower = bounds_ref[0:1, :]
    upper = bounds_ref[1:2, :]
    row_e = lax.broadcasted_iota(jnp.int32, (tm, EXPERT_PAD), 0).astype(jnp.float32)
    row_t = lax.broadcasted_iota(jnp.int32, (tm, tm), 0).astype(jnp.float32)

    def sort_chunk(j):
        r0 = float(j * tm)
        in_run = jnp.logical_and(row_e + r0 >= lower, row_e + r0 < upper)
        owner = jnp.concatenate([jnp.where(in_run, ROW_SPLIT, 0.0),
                                 jnp.where(in_run, 1.0, 0.0)], axis=1).astype(jnp.bfloat16)
        want = jnp.dot(owner, digits, preferred_element_type=jnp.float32)
        perm = jnp.where(want == row_t + (r0 + 1.0), 1.0, 0.0).astype(jnp.bfloat16)
        sorted_ref[slot, j * tm:(j + 1) * tm, :] = jnp.dot(
            perm, u_ref[...], preferred_element_type=jnp.float32).astype(jnp.bfloat16)

    for j in range(n_chunks):
        if (j + 1) * tm <= TOP_K * tm:
            sort_chunk(j)
        else:
            pl.when(used_ref[i] > j * tm)(functools.partial(sort_chunk, j))

    def run_copy(buf):
        def make(src, dst, piece):
            return pltpu.make_async_copy(sorted_ref.at[buf, pl.ds(src, piece), :],
                                         xs_hbm.at[pl.ds(dst, piece), :], sems.at[buf])
        return make

    _for_each_piece(i, pn_ref, psrc_ref, pdst_ref, run_copy(slot), start=True)

    @pl.when(i > 0)
    def _():
        _for_each_piece(i - 1, pn_ref, psrc_ref, pdst_ref, run_copy(1 - slot), start=False)

    @pl.when(i == last)
    def _():
        _for_each_piece(i, pn_ref, psrc_ref, pdst_ref, run_copy(slot), start=False)
        zero_ref[...] = jnp.zeros_like(zero_ref)
        blk = zero_ref.shape[0]
        n_blk_total = xs_hbm.shape[0] // blk

        def zero_copy(dst, piece):
            return pltpu.make_async_copy(zero_ref.at[pl.ds(0, piece), :],
                                         xs_hbm.at[pl.ds(dst, piece), :], sems.at[0])

        def tail_body(start):
            def body(e, carry):
                dst = pl.multiple_of(tail_ref[e], ROW_ALIGN)

                def piece_copy(done, piece):
                    cp = zero_copy(pl.multiple_of(dst + done, ROW_ALIGN), piece)
                    if start:
                        cp.start()
                    else:
                        cp.wait()

                _run_copies(tail_ref[N_EXPERTS + e], piece_copy, TAIL_PIECES)
                return carry
            return body

        def blk_body(start):
            def body(b, carry):
                cp = zero_copy(pl.multiple_of(b * blk, blk), blk)
                if start:
                    cp.start()
                else:
                    cp.wait()
                return carry
            return body

        lax.fori_loop(0, N_EXPERTS, tail_body(True), 0)
        lax.fori_loop(nblk_ref[0], n_blk_total, blk_body(True), 0)
        lax.fori_loop(0, N_EXPERTS, tail_body(False), 0)
        lax.fori_loop(nblk_ref[0], n_blk_total, blk_body(False), 0)


def _dispatch(u2b, row1, bounds, pieces, tail, n_used_blk, used, n_blk):
    tm = MOE_TILE
    d = D_MODEL
    n_tok = u2b.shape[0]
    grid_spec = pltpu.PrefetchScalarGridSpec(
        num_scalar_prefetch=6,
        grid=(n_tok // tm,),
        in_specs=[pl.BlockSpec((tm, d), lambda i, *_: (i, 0)),
                  pl.BlockSpec((None, EXPERT_PAD, tm), lambda i, *_: (i, 0, 0)),
                  pl.BlockSpec((None, 2, EXPERT_PAD), lambda i, *_: (i, 0, 0))],
        out_specs=pl.BlockSpec(memory_space=pl.ANY),
        scratch_shapes=[pltpu.VMEM((2, SORTED_ROWS, d), jnp.bfloat16),
                        pltpu.VMEM((MOE_BLK, d), jnp.bfloat16),
                        pltpu.SemaphoreType.DMA((2,))],
    )
    return pl.pallas_call(
        _dispatch_kernel,
        out_shape=jax.ShapeDtypeStruct((n_blk * MOE_BLK, d), jnp.bfloat16),
        grid_spec=grid_spec,
        compiler_params=_cparams(1),
        name="moe_dispatch",
    )(*pieces, tail, n_used_blk, used, u2b, row1, bounds)


def _expert_kernel(blk_e_ref, blk_valid_ref, x_ref, w1_ref, w3_ref, w2_ref, y_ref,
                   w13_ref, w2b_ref):
    i = pl.program_id(0)

    @pl.when(blk_valid_ref[i] > 0)
    def _():
        @pl.when(jnp.logical_or(i == 0, blk_e_ref[i] != blk_e_ref[jnp.maximum(i - 1, 0)]))
        def _():
            w13_ref[:, :EXPERT_FF] = w1_ref[...].astype(jnp.bfloat16)
            w13_ref[:, EXPERT_FF:] = w3_ref[...].astype(jnp.bfloat16)
            w2b_ref[...] = w2_ref[...].astype(jnp.bfloat16)

        h = jnp.dot(x_ref[...], w13_ref[...], preferred_element_type=jnp.float32)
        hb = (_silu(h[:, :EXPERT_FF]) * h[:, EXPERT_FF:]).astype(jnp.bfloat16)
        y_ref[...] = jnp.dot(hb, w2b_ref[...],
                             preferred_element_type=jnp.float32).astype(jnp.bfloat16)

    @pl.when(blk_valid_ref[i] == 0)
    def _():
        y_ref[...] = jnp.zeros_like(y_ref)


def _experts(xs, blk_e, blk_valid, w1, w3, w2, n_blk):
    blk = MOE_BLK
    d = D_MODEL
    grid_spec = pltpu.PrefetchScalarGridSpec(
        num_scalar_prefetch=2,
        grid=(n_blk,),
        in_specs=[
            pl.BlockSpec((blk, d), lambda i, be, bv: (i, 0)),
            pl.BlockSpec((None, d, EXPERT_FF), lambda i, be, bv: (be[i], 0, 0)),
            pl.BlockSpec((None, d, EXPERT_FF), lambda i, be, bv: (be[i], 0, 0)),
            pl.BlockSpec((None, EXPERT_FF, d), lambda i, be, bv: (be[i], 0, 0)),
        ],
        out_specs=pl.BlockSpec((blk, d), lambda i, be, bv: (i, 0)),
        scratch_shapes=[pltpu.VMEM((d, 2 * EXPERT_FF), jnp.bfloat16),
                        pltpu.VMEM((EXPERT_FF, d), jnp.bfloat16)],
    )
    return pl.pallas_call(
        _expert_kernel,
        out_shape=jax.ShapeDtypeStruct((n_blk * blk, d), jnp.bfloat16),
        grid_spec=grid_spec,
        compiler_params=_cparams(1),
        name="moe_experts",
    )(blk_e, blk_valid, xs, w1, w3, w2)


def _final_kernel(pn_ref, psrc_ref, pdst_ref, used_ref, rowt_ref, gatet_ref, cnt_tab_ref,
                  off_tab_ref, y_hbm, x_ref, u_ref, g2_ref, lng_ref, lnb_ref, ws1_ref, ws3_ref,
                  ws2_ref, o_ref, sorted_ref, acc_ref, sems, *, tiles_per_seq, n_batch, alpha):
    i = pl.program_id(0)
    last = pl.num_programs(0) - 1
    slot = i % 2
    bidx = jnp.minimum(i // tiles_per_seq, n_batch)
    tm = x_ref.shape[0]
    n_chunks = sorted_ref.shape[1] // tm

    def run_copy(buf):
        def make(src, dst, piece):
            return pltpu.make_async_copy(y_hbm.at[pl.ds(dst, piece), :],
                                         sorted_ref.at[buf, pl.ds(src, piece), :], sems.at[buf])
        return make

    @pl.when(i == 0)
    def _():
        sorted_ref[...] = jnp.zeros_like(sorted_ref)
        _for_each_piece(i, pn_ref, psrc_ref, pdst_ref, run_copy(slot), start=True)

    @pl.when(i < last)
    def _():
        _for_each_piece(i + 1, pn_ref, psrc_ref, pdst_ref, run_copy(1 - slot), start=True)

    _for_each_piece(i, pn_ref, psrc_ref, pdst_ref, run_copy(slot), start=False)

    digits = jnp.concatenate(_split_rows(rowt_ref[...]), axis=1)
    gate = gatet_ref[...]
    lower = off_tab_ref[:, 0:1]
    upper = lower + cnt_tab_ref[:, 0:1]
    lane_e = lax.broadcasted_iota(jnp.int32, (EXPERT_PAD, tm), 1).astype(jnp.float32)
    lane_t = lax.broadcasted_iota(jnp.int32, (tm, tm), 1).astype(jnp.float32)

    def chunk_sum(j):
        r0 = float(j * tm)
        in_run = jnp.logical_and(lane_e + r0 >= lower, lane_e + r0 < upper)
        owner = jnp.where(in_run, 1.0, 0.0).astype(jnp.bfloat16)
        owner_digits = jnp.concatenate([jnp.where(in_run, ROW_SPLIT, 0.0).astype(jnp.bfloat16),
                                        owner], axis=0)
        want = jnp.dot(digits, owner_digits, preferred_element_type=jnp.float32)
        weight = jnp.dot(gate, owner, preferred_element_type=jnp.float32)
        comb = jnp.where(want == lane_t + (r0 + 1.0), weight, 0.0).astype(jnp.bfloat16)
        return jnp.dot(comb, sorted_ref[slot, j * tm:(j + 1) * tm, :],
                       preferred_element_type=jnp.float32)

    always = TOP_K
    routed = chunk_sum(0)
    for j in range(1, always):
        routed = routed + chunk_sum(j)
    acc_ref[...] = routed
    for j in range(always, n_chunks):
        @pl.when(used_ref[i] > j * tm)
        def _(j=j):
            acc_ref[...] += chunk_sum(j)
    routed = acc_ref[...]
    u = u_ref[...]
    h1 = jnp.dot(u, ws1_ref[...], preferred_element_type=jnp.float32)
    h3 = jnp.dot(u, ws3_ref[...], preferred_element_type=jnp.float32)
    shared = jnp.dot((_silu(h1) * h3).astype(jnp.bfloat16), ws2_ref[...],
                     preferred_element_type=jnp.float32)
    r = alpha * x_ref[...] + g2_ref[pl.ds(bidx, 1), :] * (routed + shared)
    o_ref[...] = _layer_norm_f32(r) * lng_ref[...] + lnb_ref[...]


def _final(pieces, used, row1_t, gate_t, cnt_tab, off_tab, y_slots, x_mid, u2b, mod,
           ln_g, ln_b, ws1, ws3, ws2, n_rows, seq_len, n_batch, alpha):
    tm = MOE_TILE
    d = D_MODEL
    row = lambda i, *_: (i, 0)
    const = lambda i, *_: (0, 0)
    tab_spec = pl.BlockSpec((None, EXPERT_PAD, LANES), lambda i, *_: (i, 0, 0))
    grid_spec = pltpu.PrefetchScalarGridSpec(
        num_scalar_prefetch=4,
        grid=(n_rows // tm,),
        in_specs=[
            pl.BlockSpec((tm, EXPERT_PAD), row),
            pl.BlockSpec((tm, EXPERT_PAD), row),
            tab_spec,
            tab_spec,
            pl.BlockSpec(memory_space=pl.ANY),
            pl.BlockSpec((tm, d), row),
            pl.BlockSpec((tm, d), row),
            pl.BlockSpec((8, d), lambda i, *_: (0, 5)),
            pl.BlockSpec((1, d), const),
            pl.BlockSpec((1, d), const),
            pl.BlockSpec((d, SHARED_FF), const),
            pl.BlockSpec((d, SHARED_FF), const),
            pl.BlockSpec((SHARED_FF, d), const),
        ],
        out_specs=pl.BlockSpec((tm, d), row),
        scratch_shapes=[pltpu.VMEM((2, SORTED_ROWS, d), jnp.bfloat16),
                        pltpu.VMEM((tm, d), jnp.float32),
                        pltpu.SemaphoreType.DMA((2,))],
    )
    return pl.pallas_call(
        functools.partial(_final_kernel, tiles_per_seq=seq_len // tm, n_batch=n_batch, alpha=alpha),
        out_shape=jax.ShapeDtypeStruct((n_rows, d), jnp.float32),
        grid_spec=grid_spec,
        compiler_params=_cparams(1),
        name="moe_combine_postnorm",
    )(*pieces, used, row1_t, gate_t, cnt_tab, off_tab, y_slots, x_mid, u2b, mod,
      ln_g, ln_b, ws1, ws3, ws2)


def _moe_sublayer(x_mid, u2, u2b, mod, ln_g, ln_b, router_w, router_bias, w1, w3, w2,
                  ws1, ws3, ws2, n_rows, seq_len, n_batch, alpha):
    bf = jnp.bfloat16
    blk = MOE_BLK
    n_tiles = n_rows // MOE_TILE
    row1, row1_t, gate_t, cnt_tab, off_tab, base_tab, total = _router(
        u2, router_w, router_bias, n_rows)
    as_int = lambda t: t[:, :N_EXPERTS, 0].astype(jnp.int32)
    cnt, off, base = as_int(cnt_tab), as_int(off_tab), as_int(base_tab)
    used = off[:, -1] + cnt[:, -1]
    bounds = jnp.stack([off_tab[:, :, 0], off_tab[:, :, 0] + cnt_tab[:, :, 0]], axis=1)
    total = total.reshape(N_EXPERTS).astype(jnp.int32)
    padded = (total + blk - 1) // blk * blk
    pends = jnp.cumsum(padded)
    pstart = pends - padded
    dst = pstart[None, :] + base
    tail = jnp.concatenate([pstart + total, padded - total])
    n_used_blk = (pends[-1:] // blk).astype(jnp.int32)
    max_rows = n_rows * TOP_K + n_tiles * N_EXPERTS * (ROW_ALIGN - 1) + N_EXPERTS * (blk - 1)
    n_blk = -(-max_rows // blk)
    blk_start = jnp.arange(n_blk, dtype=jnp.int32) * blk
    blk_e = jnp.minimum(jnp.sum((blk_start[:, None] >= pends[None, :]).astype(jnp.int32), axis=1),
                        N_EXPERTS - 1)
    blk_valid = (blk_start < pends[-1]).astype(jnp.int32)
    pieces = _piece_lists(cnt, off, dst)
    xs = _dispatch(u2b, row1, bounds, pieces, tail, n_used_blk, used, n_blk)
    y_slots = _experts(xs, blk_e, blk_valid, w1, w3, w2, n_blk)
    return _final(pieces, used, row1_t, gate_t, cnt_tab, off_tab, y_slots, x_mid, u2b, mod,
                  ln_g, ln_b, ws1.astype(bf), ws3.astype(bf), ws2.astype(bf),
                  n_rows, seq_len, n_batch, alpha)


def _pack_in_proj(w_in):
    d = w_in.shape[0]
    gap = jnp.zeros((d, DT_BWD_LANE - SSM_HEADS), w_in.dtype)
    rest = jnp.zeros((d, P_FOUR - P_DT - DT_BWD_LANE - SSM_HEADS), w_in.dtype)
    return jnp.concatenate([
        w_in[:, OFF_XBC:OFF_DT],
        w_in[:, OFF_DT:OFF_DT + SSM_HEADS], gap, w_in[:, OFF_DT + SSM_HEADS:OFF_FOUR], rest,
        w_in[:, OFF_FOUR:OFF_GATE], w_in[:, OFF_GATE:], w_in[:, :OFF_XBC]],
        axis=1).astype(jnp.bfloat16)


def _pack_head_rows(v):
    out = jnp.zeros((1, LANES), jnp.float32)
    v = v.reshape(2, SSM_HEADS).astype(jnp.float32)
    out = out.at[0, 0:SSM_HEADS].set(v[0])
    return out.at[0, DT_BWD_LANE:DT_BWD_LANE + SSM_HEADS].set(v[1])


def kernel(x, c, ctx, c_ctx, w_ada, b_ada, w_in, conv_w, conv_b, dt_bias, a_log, d_skip,
           ssm_norm_w, w_br_ssm, w_br_four, w_out, ln1_g, ln1_b, ln2_g, ln2_b,
           router_w, router_bias, w1, w3, w2, ws1, ws3, ws2):
    n_batch, seq_len, d = x.shape
    ctx_len = ctx.shape[1]
    depth = w_ada.shape[0]
    bf = jnp.bfloat16
    alpha = float((2 * depth) ** 0.25)
    n_lat = n_batch * seq_len
    assert d == D_MODEL and n_batch + 1 <= 8
    assert seq_len % K1_TM == 0 and (n_batch * ctx_len) % K1_TM == 0
    assert seq_len % (LANES * 8) == 0

    x_all = _assemble_stream(x, ctx)
    c_rows = jnp.zeros((8, d), jnp.float32).at[:n_batch].set(c).at[n_batch].set(c_ctx)
    mod_all = _ada_mod(c_rows, w_ada, b_ada)
    tables = _dft_tables(seq_len)

    for i in range(depth):
        last = i == depth - 1
        mod = mod_all[i]
        p_all = _in_proj(x_all, mod, _pack_in_proj(w_in[i]), seq_len, n_batch)
        a_row = _pack_head_rows(-jnp.exp(a_log[i].astype(jnp.float32)))
        xbc, dt_all, cum_all, cumt_all, bt_all = _conv(
            p_all, conv_w[i], conv_b[i], _pack_head_rows(dt_bias[i]), a_row,
            seq_len, n_lat, ctx_len)
        yf, yb = _ssd(xbc, dt_all, cum_all, cumt_all, bt_all, n_batch, seq_len, ctx_len)
        f_all = _fourier_latent(p_all, n_batch, seq_len, tables)
        n_rows = n_lat if last else x_all.shape[0]
        if not last:
            f_all = jnp.concatenate(
                [f_all, _fourier_ctx(p_all, n_batch, seq_len, ctx_len, tables)], axis=0)
        d_cols = jnp.repeat(d_skip[i].astype(jnp.float32), SSM_HEAD_DIM).reshape(1, D_INNER)
        x_mid, u2, u2b = _merge(p_all, xbc, yf, yb, f_all, x_all, mod, d_cols,
                           ssm_norm_w[i].reshape(1, D_INNER), ln1_g[i].reshape(1, d),
                           ln1_b[i].reshape(1, d), w_br_ssm[i].astype(bf),
                           w_br_four[i].astype(bf), w_out[i].astype(bf),
                           n_rows, seq_len, n_batch, alpha)
        x_all = _moe_sublayer(x_mid, u2, u2b, mod, ln2_g[i].reshape(1, d), ln2_b[i].reshape(1, d),
                              router_w[i], router_bias[i], w1[i], w3[i], w2[i],
                              ws1[i], ws3[i], ws2[i], n_rows, seq_len, n_batch, alpha)
    return x_all[:n_lat].reshape(n_batch, seq_len, d)
```

```python
import functools

import jax
import jax.numpy as jnp
import numpy as np
from jax import lax
from jax.experimental import pallas as pl
from jax.experimental.pallas import tpu as pltpu

D_MODEL = 1024
GRID_W = 64
POS_BASE = 10000.0
LN_EPS = 1e-6

SSM_HEADS = 24
SSM_HEAD_DIM = 64
D_INNER = SSM_HEADS * SSM_HEAD_DIM
SSM_GROUPS = 4
HEADS_PER_GROUP = SSM_HEADS // SSM_GROUPS
SSM_STATE = 128
CONV_W = 5
CONV_CH = D_INNER + 2 * SSM_GROUPS * SSM_STATE
CHUNK = 128

FOURIER_GROUPS = 4
FOURIER_GROUP_DIM = 256
FOURIER_W = FOURIER_GROUPS * FOURIER_GROUP_DIM

OFF_XBC = D_INNER
OFF_DT = OFF_XBC + CONV_CH
OFF_FOUR = OFF_DT + 2 * SSM_HEADS
OFF_GATE = OFF_FOUR + FOURIER_W

N_EXPERTS = 64
TOP_K = 8
N_EXPERT_GROUPS = 8
EXPERTS_PER_GROUP = N_EXPERTS // N_EXPERT_GROUPS
TOPK_GROUPS = 4
EXPERT_FF = 256
SHARED_FF = 256
ROUTE_SCALE = 2.5

LANES = 128
VMEM_LIMIT_BYTES = 56 * 1024 * 1024

P_XBC = 0
P_DT = CONV_CH
P_FOUR = P_DT + SSM_GROUPS * LANES
P_GATE = P_FOUR + FOURIER_W
P_Z = P_GATE + 2 * D_MODEL
P_WIDTH = P_Z + D_INNER
DT_BWD_LANE = 32
CUMT_ROWS = 64

DFT2_ROWS_PER_STEP = 4
K1_TM = 512
K1_TN = 1536
ROW_TILE = 256
MOE_TILE = 512
MOE_BLK = 1024
ROW_ALIGN = 16
EXPERT_PAD = LANES
ROW_SPLIT = 64.0
RUN_PIECES = tuple(MOE_TILE >> s for s in range(6))
TAIL_PIECES = tuple(p for p in RUN_PIECES if p < MOE_BLK)
SORTED_ROWS = -(-(MOE_TILE * TOP_K + N_EXPERTS * (ROW_ALIGN - 1)) // MOE_TILE) * MOE_TILE


def _cparams(n_axes=1):
    return pltpu.CompilerParams(
        dimension_semantics=("arbitrary",) * n_axes,
        vmem_limit_bytes=VMEM_LIMIT_BYTES)


def _layer_norm_f32(x):
    mu = jnp.mean(x, axis=-1, keepdims=True)
    xc = x - mu
    var = jnp.mean(xc * xc, axis=-1, keepdims=True)
    return xc * lax.rsqrt(var + LN_EPS)


def _sigmoid(x):
    return 1.0 / (1.0 + jnp.exp(-x))


def _silu(x):
    return x * _sigmoid(x)


def _pos_kernel(x_ref, ctx_ref, er_ref, ec_ref, o_ref, *, n_lat):
    i = pl.program_id(0)
    half = D_MODEL // 2

    @pl.when(i < n_lat)
    def _():
        ec = ec_ref[...]
        for r in range(8):
            rows = slice(GRID_W * r, GRID_W * (r + 1))
            o_ref[rows, :half] = x_ref[rows, :half] + er_ref[r:r + 1, :]
            o_ref[rows, half:] = x_ref[rows, half:] + ec

    @pl.when(i >= n_lat)
    def _():
        o_ref[...] = ctx_ref[...]


def _assemble_stream(x, ctx):
    b, l, d = x.shape
    lc = ctx.shape[1]
    tile = 8 * GRID_W
    n_lat = (b * l) // tile
    n_ctx = (b * lc) // tile
    rows = l // GRID_W
    quarter = D_MODEL // 4
    omega = 1.0 / (POS_BASE ** (jnp.arange(quarter, dtype=jnp.float32) / quarter))
    ang_r = jnp.arange(rows, dtype=jnp.float32)[:, None] * omega
    ang_c = jnp.arange(GRID_W, dtype=jnp.float32)[:, None] * omega
    emb_r = jnp.concatenate([jnp.sin(ang_r), jnp.cos(ang_r)], -1)
    emb_c = jnp.concatenate([jnp.sin(ang_c), jnp.cos(ang_c)], -1)
    tiles_per_seq = l // tile
    return pl.pallas_call(
        functools.partial(_pos_kernel, n_lat=n_lat),
        out_shape=jax.ShapeDtypeStruct((b * l + b * lc, d), jnp.float32),
        grid=(n_lat + n_ctx,),
        in_specs=[
            pl.BlockSpec((tile, d), lambda i: (jnp.minimum(i, n_lat - 1), 0)),
            pl.BlockSpec((tile, d), lambda i: (jnp.maximum(i - n_lat, 0), 0)),
            pl.BlockSpec((8, d // 2), lambda i: (i % tiles_per_seq, 0)),
            pl.BlockSpec((GRID_W, d // 2), lambda i: (0, 0)),
        ],
        out_specs=pl.BlockSpec((tile, d), lambda i: (i, 0)),
        compiler_params=_cparams(1),
        name="assemble_stream",
    )(x.reshape(b * l, d), ctx.reshape(b * lc, d), emb_r, emb_c)


def _ada_kernel(c_ref, w_ref, b_ref, o_ref):
    c = c_ref[...]
    o_ref[...] = jnp.dot(_silu(c), w_ref[...], preferred_element_type=jnp.float32,
                         precision=lax.Precision.HIGHEST) + b_ref[...]


def _ada_mod(c_rows, w_ada, b_ada):
    depth, d, n6 = w_ada.shape
    tn = 1536
    return pl.pallas_call(
        _ada_kernel,
        out_shape=jax.ShapeDtypeStruct((depth, 8, n6), jnp.float32),
        grid=(depth, n6 // tn),
        in_specs=[
            pl.BlockSpec((8, d), lambda a, j: (0, 0)),
            pl.BlockSpec((None, d, tn), lambda a, j: (a, 0, j)),
            pl.BlockSpec((None, 1, tn), lambda a, j: (a, 0, j)),
        ],
        out_specs=pl.BlockSpec((None, 8, tn), lambda a, j: (a, 0, j)),
        compiler_params=_cparams(2),
        name="ada_mod",
    )(c_rows, w_ada, b_ada.reshape(depth, 1, n6))


def _k1_kernel(x_ref, sc_ref, sh_ref, w_ref, o_ref, u_ref, *, tiles_per_seq, n_batch):
    i = pl.program_id(0)
    bidx = jnp.minimum(i // tiles_per_seq, n_batch)
    xn = _layer_norm_f32(x_ref[...])
    u = xn * (1.0 + sc_ref[pl.ds(bidx, 1), :]) + sh_ref[pl.ds(bidx, 1), :]
    u_ref[...] = u.astype(jnp.bfloat16)
    for c in range(o_ref.shape[1] // K1_TN):
        cols = slice(c * K1_TN, (c + 1) * K1_TN)
        o_ref[:, cols] = jnp.dot(u_ref[...], w_ref[:, cols],
                                 preferred_element_type=jnp.float32).astype(jnp.bfloat16)


def _in_proj(x_all, mod, w_all, seq_len, n_batch):
    nt, d = x_all.shape
    tm = K1_TM
    return pl.pallas_call(
        functools.partial(_k1_kernel, tiles_per_seq=seq_len // tm, n_batch=n_batch),
        out_shape=jax.ShapeDtypeStruct((nt, P_WIDTH), jnp.bfloat16),
        grid=(nt // tm,),
        in_specs=[
            pl.BlockSpec((tm, d), lambda i: (i, 0)),
            pl.BlockSpec((8, d), lambda i: (0, 1)),
            pl.BlockSpec((8, d), lambda i: (0, 0)),
            pl.BlockSpec((d, P_WIDTH), lambda i: (0, 0), pipeline_mode=pl.Buffered(1)),
        ],
        out_specs=pl.BlockSpec((tm, P_WIDTH), lambda i: (i, 0)),
        scratch_shapes=[pltpu.VMEM((tm, d), jnp.bfloat16)],
        compiler_params=_cparams(1),
        name="in_proj",
    )(x_all, mod, mod, w_all)


def _softplus(x):
    return jnp.maximum(x, 0.0) + jnp.log1p(jnp.exp(-jnp.abs(x)))


def _conv_kernel(cur_ref, prev_ref, next_ref, w_ref, b_ref, dtraw_ref, bias_ref, a_ref,
                 o_ref, dt_ref, cum_ref, cumt_ref, bt_ref, ext_ref, *, tiles_per_seq, n_lat):
    i = pl.program_id(0)
    is_ctx = i >= n_lat
    is_start = jnp.logical_or(i % tiles_per_seq == 0, is_ctx)
    is_end = jnp.logical_or(i % tiles_per_seq == tiles_per_seq - 1, is_ctx)
    halo = prev_ref.shape[0]
    rows = cur_ref.shape[0]
    ext_ref[0:halo, :] = jnp.where(is_start, 0.0, prev_ref[...].astype(jnp.float32))
    ext_ref[halo:halo + rows, :] = cur_ref[...].astype(jnp.float32)
    ext_ref[halo + rows:, :] = jnp.where(is_end, 0.0, next_ref[...].astype(jnp.float32))

    def conv_lane_block(c, carry):
        lanes = pl.ds(pl.multiple_of(c * LANES, LANES), LANES)
        acc = b_ref[:, lanes] + w_ref[0:1, lanes] * ext_ref[pl.ds(halo - 2, rows), lanes]
        for k in range(1, CONV_W):
            acc = acc + w_ref[k:k + 1, lanes] * ext_ref[pl.ds(halo - 2 + k, rows), lanes]
        o_ref[:, lanes] = _silu(acc).astype(jnp.bfloat16)
        return carry

    lax.fori_loop(0, CONV_CH // LANES, conv_lane_block, 0)

    q = CHUNK
    dt = _softplus(dtraw_ref[...].astype(jnp.float32) + bias_ref[...])
    dt_ref[...] = dt
    adt = dt * a_ref[...]
    row_i = lax.broadcasted_iota(jnp.int32, (q, q), 0)
    col_i = lax.broadcasted_iota(jnp.int32, (q, q), 1)
    lower = (row_i >= col_i).astype(jnp.float32)
    upper = (row_i <= col_i).astype(jnp.float32)
    is_fwd = lax.broadcasted_iota(jnp.int32, (1, LANES), 1) < DT_BWD_LANE
    for ch in range(rows // q):
        rs = slice(q * ch, q * (ch + 1))
        cum_f = jnp.dot(lower, adt[rs, :], preferred_element_type=jnp.float32,
                        precision=lax.Precision.HIGHEST)
        cum_b = jnp.dot(upper, adt[rs, :], preferred_element_type=jnp.float32,
                        precision=lax.Precision.HIGHEST)
        cum = jnp.where(is_fwd, cum_f, cum_b)
        cum_ref[rs, :] = cum
        cumt_ref[ch] = cum.T[:CUMT_ROWS, :]
        for g in range(SSM_GROUPS):
            b_cols = slice(D_INNER + SSM_STATE * g, D_INNER + SSM_STATE * (g + 1))
            bt_ref[ch, SSM_STATE * g:SSM_STATE * (g + 1), :] = (
                o_ref[rs, b_cols].astype(jnp.float32).T.astype(jnp.bfloat16))


def _conv(p_all, conv_w, conv_b, dt_bias_row, a_row, seq_len, n_lat_rows, ctx_len):
    nt = p_all.shape[0]
    tl = ROW_TILE
    assert ctx_len == tl, "context sequences must span exactly one conv tile"
    halo = 16
    hb = tl // halo
    n_halo_blocks = nt // halo
    dtw = LANES
    cpt = tl // CHUNK
    w8 = jnp.zeros((8, CONV_CH), jnp.float32).at[:CONV_W].set(conv_w)
    return pl.pallas_call(
        functools.partial(_conv_kernel, tiles_per_seq=seq_len // tl, n_lat=n_lat_rows // tl),
        out_shape=(jax.ShapeDtypeStruct((nt, CONV_CH), jnp.bfloat16),
                   jax.ShapeDtypeStruct((nt, dtw), jnp.float32),
                   jax.ShapeDtypeStruct((nt, dtw), jnp.float32),
                   jax.ShapeDtypeStruct((nt // CHUNK, CUMT_ROWS, CHUNK), jnp.float32),
                   jax.ShapeDtypeStruct((nt // CHUNK, SSM_GROUPS * SSM_STATE, CHUNK), jnp.bfloat16)),
        grid=(nt // tl,),
        in_specs=[
            pl.BlockSpec((tl, CONV_CH), lambda i: (i, 0)),
            pl.BlockSpec((halo, CONV_CH), lambda i: (jnp.maximum(i * hb - 1, 0), 0)),
            pl.BlockSpec((halo, CONV_CH),
                         lambda i: (jnp.minimum((i + 1) * hb, n_halo_blocks - 1), 0)),
            pl.BlockSpec((8, CONV_CH), lambda i: (0, 0)),
            pl.BlockSpec((1, CONV_CH), lambda i: (0, 0)),
            pl.BlockSpec((tl, dtw), lambda i: (i, P_DT // dtw)),
            pl.BlockSpec((1, dtw), lambda i: (0, 0)),
            pl.BlockSpec((1, dtw), lambda i: (0, 0)),
        ],
        out_specs=(pl.BlockSpec((tl, CONV_CH), lambda i: (i, 0)),
                   pl.BlockSpec((tl, dtw), lambda i: (i, 0)),
                   pl.BlockSpec((tl, dtw), lambda i: (i, 0)),
                   pl.BlockSpec((cpt, CUMT_ROWS, CHUNK), lambda i: (i, 0, 0)),
                   pl.BlockSpec((cpt, SSM_GROUPS * SSM_STATE, CHUNK), lambda i: (i, 0, 0))),
        scratch_shapes=[pltpu.VMEM((tl + 2 * halo, CONV_CH), jnp.float32)],
        compiler_params=_cparams(1),
        name="dwconv_silu",
    )(p_all, p_all, p_all, w8, conv_b.reshape(1, CONV_CH), p_all, dt_bias_row, a_row)


def _ssd_direction(refs, s_ref, y_ref, gi, *, lane0, forward):
    x_ref, b_ref, c_ref, dt_ref, cum_ref, cumt_ref, bt_ref = refs
    q = CHUNK
    gw = HEADS_PER_GROUP * SSM_HEAD_DIM
    row_i = lax.broadcasted_iota(jnp.int32, (q, q), 0)
    col_i = lax.broadcasted_iota(jnp.int32, (q, q), 1)
    tri = (row_i >= col_i) if forward else (row_i <= col_i)
    lane_lo = lax.broadcasted_iota(jnp.int32, (q, LANES), 1) < SSM_HEAD_DIM
    lane_lo_row = lax.broadcasted_iota(jnp.int32, (1, LANES), 1) < SSM_HEAD_DIM

    g_lanes = slice(LANES * gi, LANES * (gi + 1))
    dt = dt_ref[...]
    cum = cum_ref[...]
    cum_t = cumt_ref[...]
    total = cum[q - 1:q, :] if forward else cum[0:1, :]

    cm = c_ref[:, g_lanes]
    bm = b_ref[:, g_lanes]
    scores = lax.dot_general(cm, bm, (((1,), (1,)), ((), ())),
                             preferred_element_type=jnp.float32)
    bm_t = bt_ref[SSM_STATE * gi:SSM_STATE * (gi + 1), :]

    for pr in range(HEADS_PER_GROUP // 2):
        r0 = lane0 + 2 * pr
        r1 = r0 + 1
        lanes = slice(gw * gi + LANES * pr, gw * gi + LANES * (pr + 1))
        xp = x_ref[:, lanes].astype(jnp.float32)
        dt_pair = jnp.where(lane_lo, dt[:, r0:r0 + 1], dt[:, r1:r1 + 1])
        cum_pair = jnp.where(lane_lo, cum[:, r0:r0 + 1], cum[:, r1:r1 + 1])
        tot_pair = jnp.where(lane_lo_row, total[:, r0:r0 + 1], total[:, r1:r1 + 1])
        xdt = xp * dt_pair
        l0 = jnp.exp(jnp.where(tri, cum[:, r0:r0 + 1] - cum_t[r0:r0 + 1, :], -jnp.inf))
        l1 = jnp.exp(jnp.where(tri, cum[:, r1:r1 + 1] - cum_t[r1:r1 + 1, :], -jnp.inf))
        w = jnp.concatenate([(scores * l0).astype(jnp.bfloat16),
                             (scores * l1).astype(jnp.bfloat16)], axis=1)
        xdt_b = xdt.astype(jnp.bfloat16)
        zero = jnp.zeros_like(xdt_b)
        rhs = jnp.concatenate([jnp.where(lane_lo, xdt_b, zero),
                               jnp.where(lane_lo, zero, xdt_b)], axis=0)
        y_diag = jnp.dot(w, rhs, preferred_element_type=jnp.float32)
        s_old = s_ref[:, lanes]
        y_off = jnp.dot(cm, s_old.astype(jnp.bfloat16),
                        preferred_element_type=jnp.float32) * jnp.exp(cum_pair)
        y_ref[:, lanes] = (y_diag + y_off).astype(jnp.bfloat16)
        decayed = (xdt * jnp.exp(tot_pair - cum_pair)).astype(jnp.bfloat16)
        s_ref[:, lanes] = jnp.exp(tot_pair) * s_old + jnp.dot(
            bm_t, decayed, preferred_element_type=jnp.float32)


def _ssd_kernel(*refs):
    n_in = 7
    fwd_refs, bwd_refs = refs[:n_in], refs[n_in:2 * n_in]
    yf_ref, yb_ref, sf_ref, sb_ref = refs[2 * n_in:]

    @pl.when(pl.program_id(2) == 0)
    def _():
        sf_ref[...] = jnp.zeros_like(sf_ref)
        sb_ref[...] = jnp.zeros_like(sb_ref)

    for gi in range(SSM_GROUPS):
        _ssd_direction(fwd_refs, sf_ref, yf_ref, gi, lane0=HEADS_PER_GROUP * gi, forward=True)
        _ssd_direction(bwd_refs, sb_ref, yb_ref, gi,
                       lane0=DT_BWD_LANE + HEADS_PER_GROUP * gi, forward=False)


def _ssd(xbc, dt_all, cum_all, cumt_all, bt_all, n_batch, seq_len, ctx_len):
    nt = xbc.shape[0]
    q = CHUNK
    nc_lat = seq_len // q
    nc_ctx = ctx_len // q
    ctx_blk0 = (n_batch * seq_len) // q
    n_steps = nc_ctx + nc_lat
    gw = D_INNER
    sw = SSM_GROUPS * SSM_STATE
    b_blk0 = D_INNER // sw
    c_blk0 = b_blk0 + 1

    def fwd_row(b, j):
        return jnp.where(j < nc_ctx, ctx_blk0 + nc_ctx * b + j, nc_lat * b + (j - nc_ctx))

    def bwd_row(b, j):
        return jnp.where(j < nc_ctx, ctx_blk0 + nc_ctx * b + (nc_ctx - 1 - j),
                         nc_lat * b + (n_steps - 1 - j))

    def specs(row):
        return [
            pl.BlockSpec((q, gw), lambda b, g, j: (row(b, j), g)),
            pl.BlockSpec((q, sw), lambda b, g, j: (row(b, j), b_blk0 + g)),
            pl.BlockSpec((q, sw), lambda b, g, j: (row(b, j), c_blk0 + g)),
            pl.BlockSpec((q, LANES), lambda b, g, j: (row(b, j), 0)),
            pl.BlockSpec((q, LANES), lambda b, g, j: (row(b, j), 0)),
            pl.BlockSpec((None, CUMT_ROWS, q), lambda b, g, j: (row(b, j), 0, 0)),
            pl.BlockSpec((None, sw, q), lambda b, g, j: (row(b, j), 0, 0)),
        ]

    out_sds = jax.ShapeDtypeStruct((nt, D_INNER), jnp.bfloat16)
    operands = (xbc, xbc, xbc, dt_all, cum_all, cumt_all, bt_all)
    return pl.pallas_call(
        _ssd_kernel,
        out_shape=(out_sds, out_sds),
        grid=(n_batch, 1, n_steps),
        in_specs=specs(fwd_row) + specs(bwd_row),
        out_specs=(pl.BlockSpec((q, gw), lambda b, g, j: (fwd_row(b, j), g)),
                   pl.BlockSpec((q, gw), lambda b, g, j: (bwd_row(b, j), g))),
        scratch_shapes=[pltpu.VMEM((SSM_STATE, gw), jnp.float32),
                        pltpu.VMEM((SSM_STATE, gw), jnp.float32)],
        compiler_params=_cparams(3),
        name="ssd_scan",
    )(*operands, *operands)


def _dft_tables(seq_len):
    l1n = seq_len // LANES
    two_pi = 2.0 * np.pi
    gd = FOURIER_GROUP_DIM
    jj = jnp.arange(gd, dtype=jnp.int32)
    ang_c = ((jj[:, None] * jj[None, :]) % gd).astype(jnp.float32) * (two_pi / gd)
    cc, sc = jnp.cos(ang_c), jnp.sin(ang_c)
    k1 = jnp.arange(l1n, dtype=jnp.int32)
    ang1 = ((k1[:, None] * k1[None, :]) % l1n).astype(jnp.float32) * (two_pi / l1n)
    w1 = jnp.concatenate([jnp.cos(ang1), -jnp.sin(ang1)], axis=0)
    k2 = jnp.arange(LANES, dtype=jnp.int32)
    kk = k1[:, None, None] + l1n * k2[None, :, None]
    ang2 = ((kk * k2[None, None, :]) % seq_len).astype(jnp.float32) * (two_pi / seq_len)
    er, ei = jnp.cos(ang2), -jnp.sin(ang2)
    e = jnp.concatenate([jnp.concatenate([er, -ei], axis=2),
                         jnp.concatenate([ei, er], axis=2)], axis=1)
    return cc, sc, w1, e


def _dft1_kernel(w_ref, x_ref, o_ref):
    o_ref[...] = jnp.dot(w_ref[...], x_ref[...],
                         preferred_element_type=jnp.float32).astype(jnp.bfloat16)


def _channel_mix(gr, gi, cs_ref, o_ref):
    gd = FOURIER_GROUP_DIM
    for g in range(FOURIER_GROUPS):
        cols = slice(gd * g, gd * (g + 1))
        lhs = jnp.concatenate([gr[:, cols], gi[:, cols]], axis=1).astype(jnp.bfloat16)
        o_ref[:, cols] = jnp.dot(lhs, cs_ref[...],
                                 preferred_element_type=jnp.float32).astype(jnp.bfloat16)


def _dft2_kernel(e_ref, ar_ref, ai_ref, cs_ref, o_ref):
    for k in range(e_ref.shape[0]):
        a = jnp.concatenate([ar_ref[k], ai_ref[k]], axis=0)
        g = jnp.dot(e_ref[k], a, preferred_element_type=jnp.float32)
        half = g.shape[0] // 2
        _channel_mix(g[:half], g[half:], cs_ref, o_ref.at[k])


def _dft_ctx_kernel(w_ref, x_ref, cs_ref, o_ref):
    g = jnp.dot(w_ref[...], x_ref[...], preferred_element_type=jnp.float32)
    half = g.shape[0] // 2
    _channel_mix(g[:half], g[half:], cs_ref, o_ref)


def _fourier_latent(p_all, n_batch, seq_len, tables):
    cc, sc, w1, e = tables
    c = FOURIER_W
    l1n = seq_len // LANES
    ncol = LANES * c
    four = p_all[:n_batch * seq_len, P_FOUR:P_FOUR + c].reshape(n_batch, l1n, ncol)
    tn = 4096
    a = pl.pallas_call(
        _dft1_kernel,
        out_shape=jax.ShapeDtypeStruct((n_batch, 2 * l1n, ncol), jnp.bfloat16),
        grid=(n_batch, ncol // tn),
        in_specs=[pl.BlockSpec((2 * l1n, l1n), lambda b, j: (0, 0)),
                  pl.BlockSpec((None, l1n, tn), lambda b, j: (b, 0, j))],
        out_specs=pl.BlockSpec((None, 2 * l1n, tn), lambda b, j: (b, 0, j)),
        compiler_params=_cparams(2),
        name="dft_stage1",
    )(w1.astype(jnp.bfloat16), four)
    a4 = a.reshape(n_batch, 2 * l1n, LANES, c)
    norm = 1.0 / np.sqrt(float(seq_len) * FOURIER_GROUP_DIM)
    cs = (jnp.concatenate([cc, sc], axis=0) * norm).astype(jnp.bfloat16)
    kp = DFT2_ROWS_PER_STEP
    o = pl.pallas_call(
        _dft2_kernel,
        out_shape=jax.ShapeDtypeStruct((n_batch, l1n, LANES, c), jnp.bfloat16),
        grid=(n_batch, l1n // kp),
        in_specs=[pl.BlockSpec((kp, 2 * LANES, 2 * LANES), lambda b, k: (k, 0, 0)),
                  pl.BlockSpec((None, kp, LANES, c), lambda b, k: (b, k, 0, 0)),
                  pl.BlockSpec((None, kp, LANES, c), lambda b, k: (b, l1n // kp + k, 0, 0)),
                  pl.BlockSpec((2 * FOURIER_GROUP_DIM, FOURIER_GROUP_DIM), lambda b, k: (0, 0))],
        out_specs=pl.BlockSpec((None, kp, LANES, c), lambda b, k: (b, k, 0, 0)),
        compiler_params=_cparams(2),
        name="dft_stage2",
    )(e.astype(jnp.bfloat16), a4, a4, cs)
    return o.transpose(0, 2, 1, 3).reshape(n_batch * seq_len, c)


def _fourier_ctx(p_all, n_batch, seq_len, ctx_len, tables):
    cc, sc, _, _ = tables
    assert ctx_len == FOURIER_GROUP_DIM
    c = FOURIER_W
    wc = jnp.concatenate([cc, -sc], axis=0).astype(jnp.bfloat16)
    norm = 1.0 / np.sqrt(float(ctx_len) * FOURIER_GROUP_DIM)
    cs = (jnp.concatenate([cc, sc], axis=0) * norm).astype(jnp.bfloat16)
    blk0 = (n_batch * seq_len) // ctx_len
    return pl.pallas_call(
        _dft_ctx_kernel,
        out_shape=jax.ShapeDtypeStruct((n_batch * ctx_len, c), jnp.bfloat16),
        grid=(n_batch,),
        in_specs=[pl.BlockSpec((2 * ctx_len, ctx_len), lambda b: (0, 0)),
                  pl.BlockSpec((ctx_len, c), lambda b: (blk0 + b, P_FOUR // c)),
                  pl.BlockSpec((2 * FOURIER_GROUP_DIM, FOURIER_GROUP_DIM), lambda b: (0, 0))],
        out_specs=pl.BlockSpec((ctx_len, c), lambda b: (b, 0)),
        compiler_params=_cparams(1),
        name="dft_ctx",
    )(wc, p_all, cs)


def _merge_kernel(gate_ref, z_ref, xs_ref, yf_ref, yb_ref, f_ref, x_ref,
                  g1_ref, sc2_ref, sh2_ref, d_ref, nw_ref, lng_ref, lnb_ref,
                  wssm_ref, wfour_ref, wout_ref, xo_ref, u2_ref, u2b_ref,
                  *, tiles_per_seq, n_batch, alpha):
    bidx = jnp.minimum(pl.program_id(0) // tiles_per_seq, n_batch)
    y = (yf_ref[...].astype(jnp.float32) + yb_ref[...].astype(jnp.float32)
         + xs_ref[...].astype(jnp.float32) * d_ref[...])
    h = y * _silu(z_ref[...].astype(jnp.float32))
    h = h * lax.rsqrt(jnp.mean(h * h, axis=-1, keepdims=True) + LN_EPS) * nw_ref[...]
    ssm = jnp.dot(h.astype(jnp.bfloat16), wssm_ref[...], preferred_element_type=jnp.float32)
    four = jnp.dot(f_ref[...], wfour_ref[...], preferred_element_type=jnp.float32)
    gates = _sigmoid(gate_ref[...].astype(jnp.float32))
    merged = gates[:, :D_MODEL] * ssm + gates[:, D_MODEL:] * four
    out = jnp.dot(merged.astype(jnp.bfloat16), wout_ref[...], preferred_element_type=jnp.float32)
    r = alpha * x_ref[...] + g1_ref[pl.ds(bidx, 1), :] * out
    xn = _layer_norm_f32(r) * lng_ref[...] + lnb_ref[...]
    xo_ref[...] = xn
    u2 = (_layer_norm_f32(xn) * (1.0 + sc2_ref[pl.ds(bidx, 1), :])
          + sh2_ref[pl.ds(bidx, 1), :])
    u2_ref[...] = u2
    u2b_ref[...] = u2.astype(jnp.bfloat16)


def _merge(p_all, xbc, yf, yb, f_all, x_all, mod, d_cols, norm_w, ln_g, ln_b,
           w_ssm, w_four, w_out, n_rows, seq_len, n_batch, alpha):
    tm = ROW_TILE
    d = D_MODEL
    row = lambda i: (i, 0)
    const = lambda i: (0, 0)
    out_sds = jax.ShapeDtypeStruct((n_rows, d), jnp.float32)
    return pl.pallas_call(
        functools.partial(_merge_kernel, tiles_per_seq=seq_len // tm, n_batch=n_batch, alpha=alpha),
        out_shape=(out_sds, out_sds, jax.ShapeDtypeStruct((n_rows, d), jnp.bfloat16)),
        grid=(n_rows // tm,),
        in_specs=[
            pl.BlockSpec((tm, 2 * d), lambda i: (i, P_GATE // (2 * d))),
            pl.BlockSpec((tm, D_INNER), lambda i: (i, P_Z // D_INNER)),
            pl.BlockSpec((tm, D_INNER), row),
            pl.BlockSpec((tm, D_INNER), row),
            pl.BlockSpec((tm, D_INNER), row),
            pl.BlockSpec((tm, FOURIER_W), row),
            pl.BlockSpec((tm, d), row),
            pl.BlockSpec((8, d), lambda i: (0, 2)),
            pl.BlockSpec((8, d), lambda i: (0, 4)),
            pl.BlockSpec((8, d), lambda i: (0, 3)),
            pl.BlockSpec((1, D_INNER), const),
            pl.BlockSpec((1, D_INNER), const),
            pl.BlockSpec((1, d), const),
            pl.BlockSpec((1, d), const),
            pl.BlockSpec((D_INNER, d), const),
            pl.BlockSpec((FOURIER_W, d), const),
            pl.BlockSpec((d, d), const),
        ],
        out_specs=(pl.BlockSpec((tm, d), row), pl.BlockSpec((tm, d), row),
                   pl.BlockSpec((tm, d), row)),
        compiler_params=_cparams(1),
        name="merge_postnorm",
    )(p_all, p_all, xbc, yf, yb, f_all, x_all, mod, mod, mod, d_cols, norm_w,
      ln_g, ln_b, w_ssm, w_four, w_out)


def _first_index_of_max(v, iota, big):
    m = jnp.max(v, axis=0, keepdims=True)
    idx = jnp.min(jnp.where(v == m, iota, big), axis=0, keepdims=True)
    return m, idx


def _router_kernel(u_ref, wt_ref, bias_ref, tri_ref, row_ref, rowt_ref, gatet_ref, cnt_ref,
                   off_ref, basetab_ref, tot_ref, base_ref):
    @pl.when(pl.program_id(0) == 0)
    def _():
        base_ref[...] = jnp.zeros_like(base_ref)

    tm = u_ref.shape[0]
    ne, epg = N_EXPERTS, EXPERTS_PER_GROUP
    neg = -jnp.inf
    logits = lax.dot_general(wt_ref[...], u_ref[...], (((1,), (1,)), ((), ())),
                             preferred_element_type=jnp.float32,
                             precision=lax.Precision.HIGHEST)
    scores = _sigmoid(logits)
    sel = scores + bias_ref[...]
    iota_g = lax.broadcasted_iota(jnp.int32, (epg, tm), 0).astype(jnp.float32)
    grp_rows = []
    for g in range(N_EXPERT_GROUPS):
        v = sel[epg * g:epg * (g + 1), :]
        m1, i1 = _first_index_of_max(v, iota_g, epg)
        m2 = jnp.max(jnp.where(iota_g == i1, neg, v), axis=0, keepdims=True)
        grp_rows.append(m1 + m2)
    grp = jnp.concatenate(grp_rows, axis=0)
    iota_n = lax.broadcasted_iota(jnp.int32, (N_EXPERT_GROUPS, tm), 0).astype(jnp.float32)
    chosen = jnp.zeros((N_EXPERT_GROUPS, tm), jnp.float32)
    for _ in range(TOPK_GROUPS):
        _, gi = _first_index_of_max(grp, iota_n, N_EXPERT_GROUPS)
        hit = iota_n == gi
        chosen = jnp.where(hit, 1.0, chosen)
        grp = jnp.where(hit, neg, grp)
    masked = jnp.concatenate(
        [jnp.where(chosen[g:g + 1, :] > 0.0, sel[epg * g:epg * (g + 1), :], neg)
         for g in range(N_EXPERT_GROUPS)], axis=0)
    iota_e = lax.broadcasted_iota(jnp.int32, (ne, tm), 0).astype(jnp.float32)
    picked = jnp.zeros((ne, tm), jnp.float32)
    for _ in range(TOP_K):
        _, ei = _first_index_of_max(masked, iota_e, ne)
        hit = iota_e == ei
        picked = jnp.where(hit, 1.0, picked)
        masked = jnp.where(hit, neg, masked)
    chosen_scores = picked * scores
    gate = chosen_scores / jnp.sum(chosen_scores, axis=0, keepdims=True) * ROUTE_SCALE
    before = jnp.dot(picked.astype(jnp.bfloat16), tri_ref[...],
                     preferred_element_type=jnp.float32)
    cnt = jnp.sum(picked, axis=1, keepdims=True)
    cnt16 = jnp.ceil(cnt * (1.0 / ROW_ALIGN)) * ROW_ALIGN
    cnt16_b = jnp.broadcast_to(cnt16, (ne, LANES))
    e_row = lax.broadcasted_iota(jnp.int32, (ne, ne), 0)
    e_col = lax.broadcasted_iota(jnp.int32, (ne, ne), 1)
    off_b = jnp.dot((e_col < e_row).astype(jnp.float32), cnt16_b,
                    preferred_element_type=jnp.float32,
                    precision=lax.Precision.HIGHEST)
    row1 = picked * (before + off_b[:, 0:1] + 1.0)
    pad = jnp.zeros((EXPERT_PAD - ne, tm), jnp.float32)
    row1_p = jnp.concatenate([row1, pad], axis=0)
    gate_p = jnp.concatenate([gate, pad], axis=0)
    pad_b = jnp.zeros((EXPERT_PAD - ne, LANES), jnp.float32)
    row_ref[...] = row1_p
    rowt_ref[...] = row1_p.T
    gatet_ref[...] = gate_p.T.astype(jnp.bfloat16)
    cnt_ref[...] = jnp.concatenate([cnt16_b, pad_b], axis=0)
    off_ref[...] = jnp.concatenate([off_b, pad_b], axis=0)
    basetab_ref[...] = jnp.broadcast_to(base_ref[...], (ne, LANES))
    base_ref[...] = base_ref[...] + cnt16
    tot_ref[...] = base_ref[...]


def _router(u2, router_w, router_bias, n_tok):
    tm = MOE_TILE
    n_tiles = n_tok // tm
    tri = jnp.triu(jnp.ones((tm, tm), jnp.float32), k=1).astype(jnp.bfloat16)
    tab = lambda rows: (jax.ShapeDtypeStruct((n_tiles, rows, LANES), jnp.float32),
                        pl.BlockSpec((None, rows, LANES), lambda i: (i, 0, 0)))
    outs = [
        (jax.ShapeDtypeStruct((n_tiles, EXPERT_PAD, tm), jnp.float32),
         pl.BlockSpec((None, EXPERT_PAD, tm), lambda i: (i, 0, 0))),
        (jax.ShapeDtypeStruct((n_tok, EXPERT_PAD), jnp.float32),
         pl.BlockSpec((tm, EXPERT_PAD), lambda i: (i, 0))),
        (jax.ShapeDtypeStruct((n_tok, EXPERT_PAD), jnp.bfloat16),
         pl.BlockSpec((tm, EXPERT_PAD), lambda i: (i, 0))),
        tab(EXPERT_PAD), tab(EXPERT_PAD), tab(N_EXPERTS),
        (jax.ShapeDtypeStruct((N_EXPERTS, 1), jnp.float32),
         pl.BlockSpec((N_EXPERTS, 1), lambda i: (0, 0))),
    ]
    return pl.pallas_call(
        _router_kernel,
        out_shape=tuple(o[0] for o in outs),
        grid=(n_tiles,),
        in_specs=[pl.BlockSpec((tm, D_MODEL), lambda i: (i, 0)),
                  pl.BlockSpec((N_EXPERTS, D_MODEL), lambda i: (0, 0)),
                  pl.BlockSpec((N_EXPERTS, 1), lambda i: (0, 0)),
                  pl.BlockSpec((tm, tm), lambda i: (0, 0))],
        out_specs=tuple(o[1] for o in outs),
        scratch_shapes=[pltpu.VMEM((N_EXPERTS, 1), jnp.float32)],
        compiler_params=_cparams(1),
        name="moe_router",
    )(u2, router_w.T, router_bias.reshape(N_EXPERTS, 1), tri)


def _run_copies(cnt, make_copy, pieces):
    done = jnp.int32(0)
    for piece in pieces:
        hit = (cnt & piece) != 0

        @pl.when(hit)
        def _(done=done, piece=piece):
            make_copy(done, piece)

        done = done + (cnt & piece)


def _piece_lists(cnt, off, dst):
    counts, srcs, dsts = [], [], []
    slot_ids = jnp.arange(N_EXPERTS, dtype=jnp.int32)
    for piece in RUN_PIECES:
        has = ((cnt & piece) != 0).astype(jnp.int32)
        above = cnt & ~(2 * piece - 1)
        rank = jnp.cumsum(has, axis=-1) - has
        place = (rank[:, None, :] == slot_ids[None, :, None]).astype(jnp.int32) * has[:, None, :]
        counts.append(jnp.sum(has, axis=-1))
        srcs.append(jnp.sum(place * (off + above)[:, None, :], axis=-1))
        dsts.append(jnp.sum(place * (dst + above)[:, None, :], axis=-1))
    flat = lambda parts: jnp.stack(parts, axis=1).reshape(-1).astype(jnp.int32)
    return flat(counts), flat(srcs), flat(dsts)


def _for_each_piece(tile, pn_ref, psrc_ref, pdst_ref, make_copy, start):
    for p_idx, piece in enumerate(RUN_PIECES):
        seg = tile * len(RUN_PIECES) + p_idx
        base = seg * N_EXPERTS

        def body(k, carry, piece=piece, base=base):
            cp = make_copy(pl.multiple_of(psrc_ref[base + k], ROW_ALIGN),
                           pl.multiple_of(pdst_ref[base + k], ROW_ALIGN), piece)
            if start:
                cp.start()
            else:
                cp.wait()
            return carry

        lax.fori_loop(0, pn_ref[seg], body, 0)


def _split_rows(row1):
    hi = jnp.floor(row1 * (1.0 / ROW_SPLIT))
    lo = row1 - hi * ROW_SPLIT
    return hi.astype(jnp.bfloat16), lo.astype(jnp.bfloat16)


def _dispatch_kernel(pn_ref, psrc_ref, pdst_ref, tail_ref, nblk_ref, used_ref, u_ref, row_ref,
                     bounds_ref, xs_hbm, sorted_ref, zero_ref, sems):
    i = pl.program_id(0)
    last = pl.num_programs(0) - 1
    slot = i % 2
    tm = u_ref.shape[0]
    n_chunks = sorted_ref.shape[1] // tm
    digits = jnp.concatenate(_split_rows(row_ref[...]), axis=0)
    lower = bounds_ref[0:1, :]
    upper = bounds_ref[1:2, :]
    row_e = lax.broadcasted_iota(jnp.int32, (tm, EXPERT_PAD), 0).astype(jnp.float32)
    row_t = lax.broadcasted_iota(jnp.int32, (tm, tm), 0).astype(jnp.float32)

    def sort_chunk(j):
        r0 = float(j * tm)
        in_run = jnp.logical_and(row_e + r0 >= lower, row_e + r0 < upper)
        owner = jnp.concatenate([jnp.where(in_run, ROW_SPLIT, 0.0),
                                 jnp.where(in_run, 1.0, 0.0)], axis=1).astype(jnp.bfloat16)
        want = jnp.dot(owner, digits, preferred_element_type=jnp.float32)
        perm = jnp.where(want == row_t + (r0 + 1.0), 1.0, 0.0).astype(jnp.bfloat16)
        sorted_ref[slot, j * tm:(j + 1) * tm, :] = jnp.dot(
            perm, u_ref[...], preferred_element_type=jnp.float32).astype(jnp.bfloat16)

    for j in range(n_chunks):
        if (j + 1) * tm <= TOP_K * tm:
            sort_chunk(j)
        else:
            pl.when(used_ref[i] > j * tm)(functools.partial(sort_chunk, j))

    def run_copy(buf):
        def make(src, dst, piece):
            return pltpu.make_async_copy(sorted_ref.at[buf, pl.ds(src, piece), :],
                                         xs_hbm.at[pl.ds(dst, piece), :], sems.at[buf])
        return make

    _for_each_piece(i, pn_ref, psrc_ref, pdst_ref, run_copy(slot), start=True)

    @pl.when(i > 0)
    def _():
        _for_each_piece(i - 1, pn_ref, psrc_ref, pdst_ref, run_copy(1 - slot), start=False)

    @pl.when(i == last)
    def _():
        _for_each_piece(i, pn_ref, psrc_ref, pdst_ref, run_copy(slot), start=False)
        zero_ref[...] = jnp.zeros_like(zero_ref)
        blk = zero_ref.shape[0]
        n_blk_total = xs_hbm.shape[0] // blk

        def zero_copy(dst, piece):
            return pltpu.make_async_copy(zero_ref.at[pl.ds(0, piece), :],
                                         xs_hbm.at[pl.ds(dst, piece), :], sems.at[0])

        def tail_body(start):
            def body(e, carry):
                dst = pl.multiple_of(tail_ref[e], ROW_ALIGN)

                def piece_copy(done, piece):
                    cp = zero_copy(pl.multiple_of(dst + done, ROW_ALIGN), piece)
                    if start:
                        cp.start()
                    else:
                        cp.wait()

                _run_copies(tail_ref[N_EXPERTS + e], piece_copy, TAIL_PIECES)
                return carry
            return body

        def blk_body(start):
            def body(b, carry):
                cp = zero_copy(pl.multiple_of(b * blk, blk), blk)
                if start:
                    cp.start()
                else:
                    cp.wait()
                return carry
            return body

        lax.fori_loop(0, N_EXPERTS, tail_body(True), 0)
        lax.fori_loop(nblk_ref[0], n_blk_total, blk_body(True), 0)
        lax.fori_loop(0, N_EXPERTS, tail_body(False), 0)
        lax.fori_loop(nblk_ref[0], n_blk_total, blk_body(False), 0)


def _dispatch(u2b, row1, bounds, pieces, tail, n_used_blk, used, n_blk):
    tm = MOE_TILE
    d = D_MODEL
    n_tok = u2b.shape[0]
    grid_spec = pltpu.PrefetchScalarGridSpec(
        num_scalar_prefetch=6,
        grid=(n_tok // tm,),
        in_specs=[pl.BlockSpec((tm, d), lambda i, *_: (i, 0)),
                  pl.BlockSpec((None, EXPERT_PAD, tm), lambda i, *_: (i, 0, 0)),
                  pl.BlockSpec((None, 2, EXPERT_PAD), lambda i, *_: (i, 0, 0))],
        out_specs=pl.BlockSpec(memory_space=pl.ANY),
        scratch_shapes=[pltpu.VMEM((2, SORTED_ROWS, d), jnp.bfloat16),
                        pltpu.VMEM((MOE_BLK, d), jnp.bfloat16),
                        pltpu.SemaphoreType.DMA((2,))],
    )
    return pl.pallas_call(
        _dispatch_kernel,
        out_shape=jax.ShapeDtypeStruct((n_blk * MOE_BLK, d), jnp.bfloat16),
        grid_spec=grid_spec,
        compiler_params=_cparams(1),
        name="moe_dispatch",
    )(*pieces, tail, n_used_blk, used, u2b, row1, bounds)


def _expert_kernel(blk_e_ref, blk_valid_ref, x_ref, w1_ref, w3_ref, w2_ref, y_ref,
                   w13_ref, w2b_ref):
    i = pl.program_id(0)

    @pl.when(blk_valid_ref[i] > 0)
    def _():
        @pl.when(jnp.logical_or(i == 0, blk_e_ref[i] != blk_e_ref[jnp.maximum(i - 1, 0)]))
        def _():
            w13_ref[:, :EXPERT_FF] = w1_ref[...].astype(jnp.bfloat16)
            w13_ref[:, EXPERT_FF:] = w3_ref[...].astype(jnp.bfloat16)
            w2b_ref[...] = w2_ref[...].astype(jnp.bfloat16)

        h = jnp.dot(x_ref[...], w13_ref[...], preferred_element_type=jnp.float32)
        hb = (_silu(h[:, :EXPERT_FF]) * h[:, EXPERT_FF:]).astype(jnp.bfloat16)
        y_ref[...] = jnp.dot(hb, w2b_ref[...],
                             preferred_element_type=jnp.float32).astype(jnp.bfloat16)

    @pl.when(blk_valid_ref[i] == 0)
    def _():
        y_ref[...] = jnp.zeros_like(y_ref)


def _experts(xs, blk_e, blk_valid, w1, w3, w2, n_blk):
    blk = MOE_BLK
    d = D_MODEL
    grid_spec = pltpu.PrefetchScalarGridSpec(
        num_scalar_prefetch=2,
        grid=(n_blk,),
        in_specs=[
            pl.BlockSpec((blk, d), lambda i, be, bv: (i, 0)),
            pl.BlockSpec((None, d, EXPERT_FF), lambda i, be, bv: (be[i], 0, 0)),
            pl.BlockSpec((None, d, EXPERT_FF), lambda i, be, bv: (be[i], 0, 0)),
            pl.BlockSpec((None, EXPERT_FF, d), lambda i, be, bv: (be[i], 0, 0)),
        ],
        out_specs=pl.BlockSpec((blk, d), lambda i, be, bv: (i, 0)),
        scratch_shapes=[pltpu.VMEM((d, 2 * EXPERT_FF), jnp.bfloat16),
                        pltpu.VMEM((EXPERT_FF, d), jnp.bfloat16)],
    )
    return pl.pallas_call(
        _expert_kernel,
        out_shape=jax.ShapeDtypeStruct((n_blk * blk, d), jnp.bfloat16),
        grid_spec=grid_spec,
        compiler_params=_cparams(1),
        name="moe_experts",
    )(blk_e, blk_valid, xs, w1, w3, w2)


def _final_kernel(pn_ref, psrc_ref, pdst_ref, used_ref, rowt_ref, gatet_ref, cnt_tab_ref,
                  off_tab_ref, y_hbm, x_ref, u_ref, g2_ref, lng_ref, lnb_ref, ws1_ref, ws3_ref,
                  ws2_ref, o_ref, sorted_ref, acc_ref, sems, *, tiles_per_seq, n_batch, alpha):
    i = pl.program_id(0)
    last = pl.num_programs(0) - 1
    slot = i % 2
    bidx = jnp.minimum(i // tiles_per_seq, n_batch)
    tm = x_ref.shape[0]
    n_chunks = sorted_ref.shape[1] // tm

    def run_copy(buf):
        def make(src, dst, piece):
            return pltpu.make_async_copy(y_hbm.at[pl.ds(dst, piece), :],
                                         sorted_ref.at[buf, pl.ds(src, piece), :], sems.at[buf])
        return make

    @pl.when(i == 0)
    def _():
        sorted_ref[...] = jnp.zeros_like(sorted_ref)
        _for_each_piece(i, pn_ref, psrc_ref, pdst_ref, run_copy(slot), start=True)

    @pl.when(i < last)
    def _():
        _for_each_piece(i + 1, pn_ref, psrc_ref, pdst_ref, run_copy(1 - slot), start=True)

    _for_each_piece(i, pn_ref, psrc_ref, pdst_ref, run_copy(slot), start=False)

    digits = jnp.concatenate(_split_rows(rowt_ref[...]), axis=1)
    gate = gatet_ref[...]
    lower = off_tab_ref[:, 0:1]
    upper = lower + cnt_tab_ref[:, 0:1]
    lane_e = lax.broadcasted_iota(jnp.int32, (EXPERT_PAD, tm), 1).astype(jnp.float32)
    lane_t = lax.broadcasted_iota(jnp.int32, (tm, tm), 1).astype(jnp.float32)

    def chunk_sum(j):
        r0 = float(j * tm)
        in_run = jnp.logical_and(lane_e + r0 >= lower, lane_e + r0 < upper)
        owner = jnp.where(in_run, 1.0, 0.0).astype(jnp.bfloat16)
        owner_digits = jnp.concatenate([jnp.where(in_run, ROW_SPLIT, 0.0).astype(jnp.bfloat16),
                                        owner], axis=0)
        want = jnp.dot(digits, owner_digits, preferred_element_type=jnp.float32)
        weight = jnp.dot(gate, owner, preferred_element_type=jnp.float32)
        comb = jnp.where(want == lane_t + (r0 + 1.0), weight, 0.0).astype(jnp.bfloat16)
        return jnp.dot(comb, sorted_ref[slot, j * tm:(j + 1) * tm, :],
                       preferred_element_type=jnp.float32)

    always = TOP_K
    routed = chunk_sum(0)
    for j in range(1, always):
        routed = routed + chunk_sum(j)
    acc_ref[...] = routed
    for j in range(always, n_chunks):
        @pl.when(used_ref[i] > j * tm)
        def _(j=j):
            acc_ref[...] += chunk_sum(j)
    routed = acc_ref[...]
    u = u_ref[...]
    h1 = jnp.dot(u, ws1_ref[...], preferred_element_type=jnp.float32)
    h3 = jnp.dot(u, ws3_ref[...], preferred_element_type=jnp.float32)
    shared = jnp.dot((_silu(h1) * h3).astype(jnp.bfloat16), ws2_ref[...],
                     preferred_element_type=jnp.float32)
    r = alpha * x_ref[...] + g2_ref[pl.ds(bidx, 1), :] * (routed + shared)
    o_ref[...] = _layer_norm_f32(r) * lng_ref[...] + lnb_ref[...]


def _final(pieces, used, row1_t, gate_t, cnt_tab, off_tab, y_slots, x_mid, u2b, mod,
           ln_g, ln_b, ws1, ws3, ws2, n_rows, seq_len, n_batch, alpha):
    tm = MOE_TILE
    d = D_MODEL
    row = lambda i, *_: (i, 0)
    const = lambda i, *_: (0, 0)
    tab_spec = pl.BlockSpec((None, EXPERT_PAD, LANES), lambda i, *_: (i, 0, 0))
    grid_spec = pltpu.PrefetchScalarGridSpec(
        num_scalar_prefetch=4,
        grid=(n_rows // tm,),
        in_specs=[
            pl.BlockSpec((tm, EXPERT_PAD), row),
            pl.BlockSpec((tm, EXPERT_PAD), row),
            tab_spec,
            tab_spec,
            pl.BlockSpec(memory_space=pl.ANY),
            pl.BlockSpec((tm, d), row),
            pl.BlockSpec((tm, d), row),
            pl.BlockSpec((8, d), lambda i, *_: (0, 5)),
            pl.BlockSpec((1, d), const),
            pl.BlockSpec((1, d), const),
            pl.BlockSpec((d, SHARED_FF), const),
            pl.BlockSpec((d, SHARED_FF), const),
            pl.BlockSpec((SHARED_FF, d), const),
        ],
        out_specs=pl.BlockSpec((tm, d), row),
        scratch_shapes=[pltpu.VMEM((2, SORTED_ROWS, d), jnp.bfloat16),
                        pltpu.VMEM((tm, d), jnp.float32),
                        pltpu.SemaphoreType.DMA((2,))],
    )
    return pl.pallas_call(
        functools.partial(_final_kernel, tiles_per_seq=seq_len // tm, n_batch=n_batch, alpha=alpha),
        out_shape=jax.ShapeDtypeStruct((n_rows, d), jnp.float32),
        grid_spec=grid_spec,
        compiler_params=_cparams(1),
        name="moe_combine_postnorm",
    )(*pieces, used, row1_t, gate_t, cnt_tab, off_tab, y_slots, x_mid, u2b, mod,
      ln_g, ln_b, ws1, ws3, ws2)


def _moe_sublayer(x_mid, u2, u2b, mod, ln_g, ln_b, router_w, router_bias, w1, w3, w2,
                  ws1, ws3, ws2, n_rows, seq_len, n_batch, alpha):
    bf = jnp.bfloat16
    blk = MOE_BLK
    n_tiles = n_rows // MOE_TILE
    row1, row1_t, gate_t, cnt_tab, off_tab, base_tab, total = _router(
        u2, router_w, router_bias, n_rows)
    as_int = lambda t: t[:, :N_EXPERTS, 0].astype(jnp.int32)
    cnt, off, base = as_int(cnt_tab), as_int(off_tab), as_int(base_tab)
    used = off[:, -1] + cnt[:, -1]
    bounds = jnp.stack([off_tab[:, :, 0], off_tab[:, :, 0] + cnt_tab[:, :, 0]], axis=1)
    total = total.reshape(N_EXPERTS).astype(jnp.int32)
    padded = (total + blk - 1) // blk * blk
    pends = jnp.cumsum(padded)
    pstart = pends - padded
    dst = pstart[None, :] + base
    tail = jnp.concatenate([pstart + total, padded - total])
    n_used_blk = (pends[-1:] // blk).astype(jnp.int32)
    max_rows = n_rows * TOP_K + n_tiles * N_EXPERTS * (ROW_ALIGN - 1) + N_EXPERTS * (blk - 1)
    n_blk = -(-max_rows // blk)
    blk_start = jnp.arange(n_blk, dtype=jnp.int32) * blk
    blk_e = jnp.minimum(jnp.sum((blk_start[:, None] >= pends[None, :]).astype(jnp.int32), axis=1),
                        N_EXPERTS - 1)
    blk_valid = (blk_start < pends[-1]).astype(jnp.int32)
    pieces = _piece_lists(cnt, off, dst)
    xs = _dispatch(u2b, row1, bounds, pieces, tail, n_used_blk, used, n_blk)
    y_slots = _experts(xs, blk_e, blk_valid, w1, w3, w2, n_blk)
    return _final(pieces, used, row1_t, gate_t, cnt_tab, off_tab, y_slots, x_mid, u2b, mod,
                  ln_g, ln_b, ws1.astype(bf), ws3.astype(bf), ws2.astype(bf),
                  n_rows, seq_len, n_batch, alpha)


def _pack_in_proj(w_in):
    d = w_in.shape[0]
    gap = jnp.zeros((d, DT_BWD_LANE - SSM_HEADS), w_in.dtype)
    rest = jnp.zeros((d, P_FOUR - P_DT - DT_BWD_LANE - SSM_HEADS), w_in.dtype)
    return jnp.concatenate([
        w_in[:, OFF_XBC:OFF_DT],
        w_in[:, OFF_DT:OFF_DT + SSM_HEADS], gap, w_in[:, OFF_DT + SSM_HEADS:OFF_FOUR], rest,
        w_in[:, OFF_FOUR:OFF_GATE], w_in[:, OFF_GATE:], w_in[:, :OFF_XBC]],
        axis=1).astype(jnp.bfloat16)


def _pack_head_rows(v):
    out = jnp.zeros((1, LANES), jnp.float32)
    v = v.reshape(2, SSM_HEADS).astype(jnp.float32)
    out = out.at[0, 0:SSM_HEADS].set(v[0])
    return out.at[0, DT_BWD_LANE:DT_BWD_LANE + SSM_HEADS].set(v[1])


def kernel(x, c, ctx, c_ctx, w_ada, b_ada, w_in, conv_w, conv_b, dt_bias, a_log, d_skip,
           ssm_norm_w, w_br_ssm, w_br_four, w_out, ln1_g, ln1_b, ln2_g, ln2_b,
           router_w, router_bias, w1, w3, w2, ws1, ws3, ws2):
    n_batch, seq_len, d = x.shape
    ctx_len = ctx.shape[1]
    depth = w_ada.shape[0]
    bf = jnp.bfloat16
    alpha = float((2 * depth) ** 0.25)
    n_lat = n_batch * seq_len
    assert d == D_MODEL and n_batch + 1 <= 8
    assert seq_len % K1_TM == 0 and (n_batch * ctx_len) % K1_TM == 0
    assert seq_len % (LANES * 8) == 0

    x_all = _assemble_stream(x, ctx)
    c_rows = jnp.zeros((8, d), jnp.float32).at[:n_batch].set(c).at[n_batch].set(c_ctx)
    mod_all = _ada_mod(c_rows, w_ada, b_ada)
    tables = _dft_tables(seq_len)

    for i in range(depth):
        last = i == depth - 1
        mod = mod_all[i]
        p_all = _in_proj(x_all, mod, _pack_in_proj(w_in[i]), seq_len, n_batch)
        a_row = _pack_head_rows(-jnp.exp(a_log[i].astype(jnp.float32)))
        xbc, dt_all, cum_all, cumt_all, bt_all = _conv(
            p_all, conv_w[i], conv_b[i], _pack_head_rows(dt_bias[i]), a_row,
            seq_len, n_lat, ctx_len)
        yf, yb = _ssd(xbc, dt_all, cum_all, cumt_all, bt_all, n_batch, seq_len, ctx_len)
        f_all = _fourier_latent(p_all, n_batch, seq_len, tables)
        n_rows = n_lat if last else x_all.shape[0]
        if not last:
            f_all = jnp.concatenate(
                [f_all, _fourier_ctx(p_all, n_batch, seq_len, ctx_len, tables)], axis=0)
        d_cols = jnp.repeat(d_skip[i].astype(jnp.float32), SSM_HEAD_DIM).reshape(1, D_INNER)
        x_mid, u2, u2b = _merge(p_all, xbc, yf, yb, f_all, x_all, mod, d_cols,
                           ssm_norm_w[i].reshape(1, D_INNER), ln1_g[i].reshape(1, d),
                           ln1_b[i].reshape(1, d), w_br_ssm[i].astype(bf),
                           w_br_four[i].astype(bf), w_out[i].astype(bf),
                           n_rows, seq_len, n_batch, alpha)
        x_all = _moe_sublayer(x_mid, u2, u2b, mod, ln2_g[i].reshape(1, d), ln2_b[i].reshape(1, d),
                              router_w[i], router_bias[i], w1[i], w3[i], w2[i],
                              ws1[i], ws3[i], ws2[i], n_rows, seq_len, n_batch, alpha)
    return x_all[:n_lat].reshape(n_batch, seq_len, d)
```

```python
import functools

import jax
import jax.numpy as jnp
import numpy as np
from jax import lax
from jax.experimental import pallas as pl
from jax.experimental.pallas import tpu as pltpu

D_MODEL = 1024
GRID_W = 64
POS_BASE = 10000.0
LN_EPS = 1e-6

SSM_HEADS = 24
SSM_HEAD_DIM = 64
D_INNER = SSM_HEADS * SSM_HEAD_DIM
SSM_GROUPS = 4
HEADS_PER_GROUP = SSM_HEADS // SSM_GROUPS
SSM_STATE = 128
CONV_W = 5
CONV_CH = D_INNER + 2 * SSM_GROUPS * SSM_STATE
CHUNK = 128

FOURIER_GROUPS = 4
FOURIER_GROUP_DIM = 256
FOURIER_W = FOURIER_GROUPS * FOURIER_GROUP_DIM

OFF_XBC = D_INNER
OFF_DT = OFF_XBC + CONV_CH
OFF_FOUR = OFF_DT + 2 * SSM_HEADS
OFF_GATE = OFF_FOUR + FOURIER_W

N_EXPERTS = 64
TOP_K = 8
N_EXPERT_GROUPS = 8
EXPERTS_PER_GROUP = N_EXPERTS // N_EXPERT_GROUPS
TOPK_GROUPS = 4
EXPERT_FF = 256
SHARED_FF = 256
ROUTE_SCALE = 2.5

LANES = 128
VMEM_LIMIT_BYTES = 56 * 1024 * 1024

P_XBC = 0
P_DT = CONV_CH
P_FOUR = P_DT + SSM_GROUPS * LANES
P_GATE = P_FOUR + FOURIER_W
P_Z = P_GATE + 2 * D_MODEL
P_WIDTH = P_Z + D_INNER
DT_BWD_LANE = 32
CUMT_ROWS = 64

DFT2_ROWS_PER_STEP = 4
K1_TM = 512
K1_TN = 1536
ROW_TILE = 256
MOE_TILE = 512
MOE_BLK = 1024
ROW_ALIGN = 16
EXPERT_PAD = LANES
ROW_SPLIT = 64.0
RUN_PIECES = tuple(MOE_TILE >> s for s in range(6))
TAIL_PIECES = tuple(p for p in RUN_PIECES if p < MOE_BLK)
SORTED_ROWS = -(-(MOE_TILE * TOP_K + N_EXPERTS * (ROW_ALIGN - 1)) // MOE_TILE) * MOE_TILE


def _cparams(n_axes=1):
    return pltpu.CompilerParams(
        dimension_semantics=("arbitrary",) * n_axes,
        vmem_limit_bytes=VMEM_LIMIT_BYTES)


def _layer_norm_f32(x):
    mu = jnp.mean(x, axis=-1, keepdims=True)
    xc = x - mu
    var = jnp.mean(xc * xc, axis=-1, keepdims=True)
    return xc * lax.rsqrt(var + LN_EPS)


def _sigmoid(x):
    return 1.0 / (1.0 + jnp.exp(-x))


def _silu(x):
    return x * _sigmoid(x)


def _pos_kernel(x_ref, ctx_ref, er_ref, ec_ref, o_ref, *, n_lat):
    i = pl.program_id(0)
    half = D_MODEL // 2

    @pl.when(i < n_lat)
    def _():
        ec = ec_ref[...]
        for r in range(8):
            rows = slice(GRID_W * r, GRID_W * (r + 1))
            o_ref[rows, :half] = x_ref[rows, :half] + er_ref[r:r + 1, :]
            o_ref[rows, half:] = x_ref[rows, half:] + ec

    @pl.when(i >= n_lat)
    def _():
        o_ref[...] = ctx_ref[...]


def _assemble_stream(x, ctx):
    b, l, d = x.shape
    lc = ctx.shape[1]
    tile = 8 * GRID_W
    n_lat = (b * l) // tile
    n_ctx = (b * lc) // tile
    rows = l // GRID_W
    quarter = D_MODEL // 4
    omega = 1.0 / (POS_BASE ** (jnp.arange(quarter, dtype=jnp.float32) / quarter))
    ang_r = jnp.arange(rows, dtype=jnp.float32)[:, None] * omega
    ang_c = jnp.arange(GRID_W, dtype=jnp.float32)[:, None] * omega
    emb_r = jnp.concatenate([jnp.sin(ang_r), jnp.cos(ang_r)], -1)
    emb_c = jnp.concatenate([jnp.sin(ang_c), jnp.cos(ang_c)], -1)
    tiles_per_seq = l // tile
    return pl.pallas_call(
        functools.partial(_pos_kernel, n_lat=n_lat),
        out_shape=jax.ShapeDtypeStruct((b * l + b * lc, d), jnp.float32),
        grid=(n_lat + n_ctx,),
        in_specs=[
            pl.BlockSpec((tile, d), lambda i: (jnp.minimum(i, n_lat - 1), 0)),
            pl.BlockSpec((tile, d), lambda i: (jnp.maximum(i - n_lat, 0), 0)),
            pl.BlockSpec((8, d // 2), lambda i: (i % tiles_per_seq, 0)),
            pl.BlockSpec((GRID_W, d // 2), lambda i: (0, 0)),
        ],
        out_specs=pl.BlockSpec((tile, d), lambda i: (i, 0)),
        compiler_params=_cparams(1),
        name="assemble_stream",
    )(x.reshape(b * l, d), ctx.reshape(b * lc, d), emb_r, emb_c)


def _ada_kernel(c_ref, w_ref, b_ref, o_ref):
    c = c_ref[...]
    o_ref[...] = jnp.dot(_silu(c), w_ref[...], preferred_element_type=jnp.float32,
                         precision=lax.Precision.HIGHEST) + b_ref[...]


def _ada_mod(c_rows, w_ada, b_ada):
    depth, d, n6 = w_ada.shape
    tn = 1536
    return pl.pallas_call(
        _ada_kernel,
        out_shape=jax.ShapeDtypeStruct((depth, 8, n6), jnp.float32),
        grid=(depth, n6 // tn),
        in_specs=[
            pl.BlockSpec((8, d), lambda a, j: (0, 0)),
            pl.BlockSpec((None, d, tn), lambda a, j: (a, 0, j)),
            pl.BlockSpec((None, 1, tn), lambda a, j: (a, 0, j)),
        ],
        out_specs=pl.BlockSpec((None, 8, tn), lambda a, j: (a, 0, j)),
        compiler_params=_cparams(2),
        name="ada_mod",
    )(c_rows, w_ada, b_ada.reshape(depth, 1, n6))


def _k1_kernel(x_ref, sc_ref, sh_ref, w_ref, o_ref, u_ref, *, tiles_per_seq, n_batch):
    i = pl.program_id(0)
    bidx = jnp.minimum(i // tiles_per_seq, n_batch)
    xn = _layer_norm_f32(x_ref[...])
    u = xn * (1.0 + sc_ref[pl.ds(bidx, 1), :]) + sh_ref[pl.ds(bidx, 1), :]
    u_ref[...] = u.astype(jnp.bfloat16)
    for c in range(o_ref.shape[1] // K1_TN):
        cols = slice(c * K1_TN, (c + 1) * K1_TN)
        o_ref[:, cols] = jnp.dot(u_ref[...], w_ref[:, cols],
                                 preferred_element_type=jnp.float32).astype(jnp.bfloat16)


def _in_proj(x_all, mod, w_all, seq_len, n_batch):
    nt, d = x_all.shape
    tm = K1_TM
    return pl.pallas_call(
        functools.partial(_k1_kernel, tiles_per_seq=seq_len // tm, n_batch=n_batch),
        out_shape=jax.ShapeDtypeStruct((nt, P_WIDTH), jnp.bfloat16),
        grid=(nt // tm,),
        in_specs=[
            pl.BlockSpec((tm, d), lambda i: (i, 0)),
            pl.BlockSpec((8, d), lambda i: (0, 1)),
            pl.BlockSpec((8, d), lambda i: (0, 0)),
            pl.BlockSpec((d, P_WIDTH), lambda i: (0, 0), pipeline_mode=pl.Buffered(1)),
        ],
        out_specs=pl.BlockSpec((tm, P_WIDTH), lambda i: (i, 0)),
        scratch_shapes=[pltpu.VMEM((tm, d), jnp.bfloat16)],
        compiler_params=_cparams(1),
        name="in_proj",
    )(x_all, mod, mod, w_all)


def _softplus(x):
    return jnp.maximum(x, 0.0) + jnp.log1p(jnp.exp(-jnp.abs(x)))


def _conv_kernel(cur_ref, prev_ref, next_ref, w_ref, b_ref, dtraw_ref, bias_ref, a_ref,
                 o_ref, dt_ref, cum_ref, cumt_ref, bt_ref, ext_ref, *, tiles_per_seq, n_lat):
    i = pl.program_id(0)
    is_ctx = i >= n_lat
    is_start = jnp.logical_or(i % tiles_per_seq == 0, is_ctx)
    is_end = jnp.logical_or(i % tiles_per_seq == tiles_per_seq - 1, is_ctx)
    halo = prev_ref.shape[0]
    rows = cur_ref.shape[0]
    ext_ref[0:halo, :] = jnp.where(is_start, 0.0, prev_ref[...].astype(jnp.float32))
    ext_ref[halo:halo + rows, :] = cur_ref[...].astype(jnp.float32)
    ext_ref[halo + rows:, :] = jnp.where(is_end, 0.0, next_ref[...].astype(jnp.float32))

    def conv_lane_block(c, carry):
        lanes = pl.ds(pl.multiple_of(c * LANES, LANES), LANES)
        acc = b_ref[:, lanes] + w_ref[0:1, lanes] * ext_ref[pl.ds(halo - 2, rows), lanes]
        for k in range(1, CONV_W):
            acc = acc + w_ref[k:k + 1, lanes] * ext_ref[pl.ds(halo - 2 + k, rows), lanes]
        o_ref[:, lanes] = _silu(acc).astype(jnp.bfloat16)
        return carry

    lax.fori_loop(0, CONV_CH // LANES, conv_lane_block, 0)

    q = CHUNK
    dt = _softplus(dtraw_ref[...].astype(jnp.float32) + bias_ref[...])
    dt_ref[...] = dt
    adt = dt * a_ref[...]
    row_i = lax.broadcasted_iota(jnp.int32, (q, q), 0)
    col_i = lax.broadcasted_iota(jnp.int32, (q, q), 1)
    lower = (row_i >= col_i).astype(jnp.float32)
    upper = (row_i <= col_i).astype(jnp.float32)
    is_fwd = lax.broadcasted_iota(jnp.int32, (1, LANES), 1) < DT_BWD_LANE
    for ch in range(rows // q):
        rs = slice(q * ch, q * (ch + 1))
        cum_f = jnp.dot(lower, adt[rs, :], preferred_element_type=jnp.float32,
                        precision=lax.Precision.HIGHEST)
        cum_b = jnp.dot(upper, adt[rs, :], preferred_element_type=jnp.float32,
                        precision=lax.Precision.HIGHEST)
        cum = jnp.where(is_fwd, cum_f, cum_b)
        cum_ref[rs, :] = cum
        cumt_ref[ch] = cum.T[:CUMT_ROWS, :]
        for g in range(SSM_GROUPS):
            b_cols = slice(D_INNER + SSM_STATE * g, D_INNER + SSM_STATE * (g + 1))
            bt_ref[ch, SSM_STATE * g:SSM_STATE * (g + 1), :] = (
                o_ref[rs, b_cols].astype(jnp.float32).T.astype(jnp.bfloat16))


def _conv(p_all, conv_w, conv_b, dt_bias_row, a_row, seq_len, n_lat_rows, ctx_len):
    nt = p_all.shape[0]
    tl = ROW_TILE
    assert ctx_len == tl, "context sequences must span exactly one conv tile"
    halo = 16
    hb = tl // halo
    n_halo_blocks = nt // halo
    dtw = LANES
    cpt = tl // CHUNK
    w8 = jnp.zeros((8, CONV_CH), jnp.float32).at[:CONV_W].set(conv_w)
    return pl.pallas_call(
        functools.partial(_conv_kernel, tiles_per_seq=seq_len // tl, n_lat=n_lat_rows // tl),
        out_shape=(jax.ShapeDtypeStruct((nt, CONV_CH), jnp.bfloat16),
                   jax.ShapeDtypeStruct((nt, dtw), jnp.float32),
                   jax.ShapeDtypeStruct((nt, dtw), jnp.float32),
                   jax.ShapeDtypeStruct((nt // CHUNK, CUMT_ROWS, CHUNK), jnp.float32),
                   jax.ShapeDtypeStruct((nt // CHUNK, SSM_GROUPS * SSM_STATE, CHUNK), jnp.bfloat16)),
        grid=(nt // tl,),
        in_specs=[
            pl.BlockSpec((tl, CONV_CH), lambda i: (i, 0)),
            pl.BlockSpec((halo, CONV_CH), lambda i: (jnp.maximum(i * hb - 1, 0), 0)),
            pl.BlockSpec((halo, CONV_CH),
                         lambda i: (jnp.minimum((i + 1) * hb, n_halo_blocks - 1), 0)),
            pl.BlockSpec((8, CONV_CH), lambda i: (0, 0)),
            pl.BlockSpec((1, CONV_CH), lambda i: (0, 0)),
            pl.BlockSpec((tl, dtw), lambda i: (i, P_DT // dtw)),
            pl.BlockSpec((1, dtw), lambda i: (0, 0)),
            pl.BlockSpec((1, dtw), lambda i: (0, 0)),
        ],
        out_specs=(pl.BlockSpec((tl, CONV_CH), lambda i: (i, 0)),
                   pl.BlockSpec((tl, dtw), lambda i: (i, 0)),
                   pl.BlockSpec((tl, dtw), lambda i: (i, 0)),
                   pl.BlockSpec((cpt, CUMT_ROWS, CHUNK), lambda i: (i, 0, 0)),
                   pl.BlockSpec((cpt, SSM_GROUPS * SSM_STATE, CHUNK), lambda i: (i, 0, 0))),
        scratch_shapes=[pltpu.VMEM((tl + 2 * halo, CONV_CH), jnp.float32)],
        compiler_params=_cparams(1),
        name="dwconv_silu",
    )(p_all, p_all, p_all, w8, conv_b.reshape(1, CONV_CH), p_all, dt_bias_row, a_row)


def _ssd_direction(refs, s_ref, y_ref, gi, *, lane0, forward):
    x_ref, b_ref, c_ref, dt_ref, cum_ref, cumt_ref, bt_ref = refs
    q = CHUNK
    gw = HEADS_PER_GROUP * SSM_HEAD_DIM
    row_i = lax.broadcasted_iota(jnp.int32, (q, q), 0)
    col_i = lax.broadcasted_iota(jnp.int32, (q, q), 1)
    tri = (row_i >= col_i) if forward else (row_i <= col_i)
    lane_lo = lax.broadcasted_iota(jnp.int32, (q, LANES), 1) < SSM_HEAD_DIM
    lane_lo_row = lax.broadcasted_iota(jnp.int32, (1, LANES), 1) < SSM_HEAD_DIM

    g_lanes = slice(LANES * gi, LANES * (gi + 1))
    dt = dt_ref[...]
    cum = cum_ref[...]
    cum_t = cumt_ref[...]
    total = cum[q - 1:q, :] if forward else cum[0:1, :]

    cm = c_ref[:, g_lanes]
    bm = b_ref[:, g_lanes]
    scores = lax.dot_general(cm, bm, (((1,), (1,)), ((), ())),
                             preferred_element_type=jnp.float32)
    bm_t = bt_ref[SSM_STATE * gi:SSM_STATE * (gi + 1), :]

    for pr in range(HEADS_PER_GROUP // 2):
        r0 = lane0 + 2 * pr
        r1 = r0 + 1
        lanes = slice(gw * gi + LANES * pr, gw * gi + LANES * (pr + 1))
        xp = x_ref[:, lanes].astype(jnp.float32)
        dt_pair = jnp.where(lane_lo, dt[:, r0:r0 + 1], dt[:, r1:r1 + 1])
        cum_pair = jnp.where(lane_lo, cum[:, r0:r0 + 1], cum[:, r1:r1 + 1])
        tot_pair = jnp.where(lane_lo_row, total[:, r0:r0 + 1], total[:, r1:r1 + 1])
        xdt = xp * dt_pair
        l0 = jnp.exp(jnp.where(tri, cum[:, r0:r0 + 1] - cum_t[r0:r0 + 1, :], -jnp.inf))
        l1 = jnp.exp(jnp.where(tri, cum[:, r1:r1 + 1] - cum_t[r1:r1 + 1, :], -jnp.inf))
        w = jnp.concatenate([(scores * l0).astype(jnp.bfloat16),
                             (scores * l1).astype(jnp.bfloat16)], axis=1)
        xdt_b = xdt.astype(jnp.bfloat16)
        zero = jnp.zeros_like(xdt_b)
        rhs = jnp.concatenate([jnp.where(lane_lo, xdt_b, zero),
                               jnp.where(lane_lo, zero, xdt_b)], axis=0)
        y_diag = jnp.dot(w, rhs, preferred_element_type=jnp.float32)
        s_old = s_ref[:, lanes]
        y_off = jnp.dot(cm, s_old.astype(jnp.bfloat16),
                        preferred_element_type=jnp.float32) * jnp.exp(cum_pair)
        y_ref[:, lanes] = (y_diag + y_off).astype(jnp.bfloat16)
        decayed = (xdt * jnp.exp(tot_pair - cum_pair)).astype(jnp.bfloat16)
        s_ref[:, lanes] = jnp.exp(tot_pair) * s_old + jnp.dot(
            bm_t, decayed, preferred_element_type=jnp.float32)


def _ssd_kernel(*refs):
    n_in = 7
    fwd_refs, bwd_refs = refs[:n_in], refs[n_in:2 * n_in]
    yf_ref, yb_ref, sf_ref, sb_ref = refs[2 * n_in:]

    @pl.when(pl.program_id(2) == 0)
    def _():
        sf_ref[...] = jnp.zeros_like(sf_ref)
        sb_ref[...] = jnp.zeros_like(sb_ref)

    for gi in range(SSM_GROUPS):
        _ssd_direction(fwd_refs, sf_ref, yf_ref, gi, lane0=HEADS_PER_GROUP * gi, forward=True)
        _ssd_direction(bwd_refs, sb_ref, yb_ref, gi,
                       lane0=DT_BWD_LANE + HEADS_PER_GROUP * gi, forward=False)


def _ssd(xbc, dt_all, cum_all, cumt_all, bt_all, n_batch, seq_len, ctx_len):
    nt = xbc.shape[0]
    q = CHUNK
    nc_lat = seq_len // q
    nc_ctx = ctx_len // q
    ctx_blk0 = (n_batch * seq_len) // q
    n_steps = nc_ctx + nc_lat
    gw = D_INNER
    sw = SSM_GROUPS * SSM_STATE
    b_blk0 = D_INNER // sw
    c_blk0 = b_blk0 + 1

    def fwd_row(b, j):
        return jnp.where(j < nc_ctx, ctx_blk0 + nc_ctx * b + j, nc_lat * b + (j - nc_ctx))

    def bwd_row(b, j):
        return jnp.where(j < nc_ctx, ctx_blk0 + nc_ctx * b + (nc_ctx - 1 - j),
                         nc_lat * b + (n_steps - 1 - j))

    def specs(row):
        return [
            pl.BlockSpec((q, gw), lambda b, g, j: (row(b, j), g)),
            pl.BlockSpec((q, sw), lambda b, g, j: (row(b, j), b_blk0 + g)),
            pl.BlockSpec((q, sw), lambda b, g, j: (row(b, j), c_blk0 + g)),
            pl.BlockSpec((q, LANES), lambda b, g, j: (row(b, j), 0)),
            pl.BlockSpec((q, LANES), lambda b, g, j: (row(b, j), 0)),
            pl.BlockSpec((None, CUMT_ROWS, q), lambda b, g, j: (row(b, j), 0, 0)),
            pl.BlockSpec((None, sw, q), lambda b, g, j: (row(b, j), 0, 0)),
        ]

    out_sds = jax.ShapeDtypeStruct((nt, D_INNER), jnp.bfloat16)
    operands = (xbc, xbc, xbc, dt_all, cum_all, cumt_all, bt_all)
    return pl.pallas_call(
        _ssd_kernel,
        out_shape=(out_sds, out_sds),
        grid=(n_batch, 1, n_steps),
        in_specs=specs(fwd_row) + specs(bwd_row),
        out_specs=(pl.BlockSpec((q, gw), lambda b, g, j: (fwd_row(b, j), g)),
                   pl.BlockSpec((q, gw), lambda b, g, j: (bwd_row(b, j), g))),
        scratch_shapes=[pltpu.VMEM((SSM_STATE, gw), jnp.float32),
                        pltpu.VMEM((SSM_STATE, gw), jnp.float32)],
        compiler_params=_cparams(3),
        name="ssd_scan",
    )(*operands, *operands)


def _dft_tables(seq_len):
    l1n = seq_len // LANES
    two_pi = 2.0 * np.pi
    gd = FOURIER_GROUP_DIM
    jj = jnp.arange(gd, dtype=jnp.int32)
    ang_c = ((jj[:, None] * jj[None, :]) % gd).astype(jnp.float32) * (two_pi / gd)
    cc, sc = jnp.cos(ang_c), jnp.sin(ang_c)
    k1 = jnp.arange(l1n, dtype=jnp.int32)
    ang1 = ((k1[:, None] * k1[None, :]) % l1n).astype(jnp.float32) * (two_pi / l1n)
    w1 = jnp.concatenate([jnp.cos(ang1), -jnp.sin(ang1)], axis=0)
    k2 = jnp.arange(LANES, dtype=jnp.int32)
    kk = k1[:, None, None] + l1n * k2[None, :, None]
    ang2 = ((kk * k2[None, None, :]) % seq_len).astype(jnp.float32) * (two_pi / seq_len)
    er, ei = jnp.cos(ang2), -jnp.sin(ang2)
    e = jnp.concatenate([jnp.concatenate([er, -ei], axis=2),
                         jnp.concatenate([ei, er], axis=2)], axis=1)
    return cc, sc, w1, e


def _dft1_kernel(w_ref, x_ref, o_ref):
    o_ref[...] = jnp.dot(w_ref[...], x_ref[...],
                         preferred_element_type=jnp.float32).astype(jnp.bfloat16)


def _channel_mix(gr, gi, cs_ref, o_ref):
    gd = FOURIER_GROUP_DIM
    for g in range(FOURIER_GROUPS):
        cols = slice(gd * g, gd * (g + 1))
        lhs = jnp.concatenate([gr[:, cols], gi[:, cols]], axis=1).astype(jnp.bfloat16)
        o_ref[:, cols] = jnp.dot(lhs, cs_ref[...],
                                 preferred_element_type=jnp.float32).astype(jnp.bfloat16)


def _dft2_kernel(e_ref, ar_ref, ai_ref, cs_ref, o_ref):
    for k in range(e_ref.shape[0]):
        a = jnp.concatenate([ar_ref[k], ai_ref[k]], axis=0)
        g = jnp.dot(e_ref[k], a, preferred_element_type=jnp.float32)
        half = g.shape[0] // 2
        _channel_mix(g[:half], g[half:], cs_ref, o_ref.at[k])


def _dft_ctx_kernel(w_ref, x_ref, cs_ref, o_ref):
    g = jnp.dot(w_ref[...], x_ref[...], preferred_element_type=jnp.float32)
    half = g.shape[0] // 2
    _channel_mix(g[:half], g[half:], cs_ref, o_ref)


def _fourier_latent(p_all, n_batch, seq_len, tables):
    cc, sc, w1, e = tables
    c = FOURIER_W
    l1n = seq_len // LANES
    ncol = LANES * c
    four = p_all[:n_batch * seq_len, P_FOUR:P_FOUR + c].reshape(n_batch, l1n, ncol)
    tn = 4096
    a = pl.pallas_call(
        _dft1_kernel,
        out_shape=jax.ShapeDtypeStruct((n_batch, 2 * l1n, ncol), jnp.bfloat16),
        grid=(n_batch, ncol // tn),
        in_specs=[pl.BlockSpec((2 * l1n, l1n), lambda b, j: (0, 0)),
                  pl.BlockSpec((None, l1n, tn), lambda b, j: (b, 0, j))],
        out_specs=pl.BlockSpec((None, 2 * l1n, tn), lambda b, j: (b, 0, j)),
        compiler_params=_cparams(2),
        name="dft_stage1",
    )(w1.astype(jnp.bfloat16), four)
    a4 = a.reshape(n_batch, 2 * l1n, LANES, c)
    norm = 1.0 / np.sqrt(float(seq_len) * FOURIER_GROUP_DIM)
    cs = (jnp.concatenate([cc, sc], axis=0) * norm).astype(jnp.bfloat16)
    kp = DFT2_ROWS_PER_STEP
    o = pl.pallas_call(
        _dft2_kernel,
        out_shape=jax.ShapeDtypeStruct((n_batch, l1n, LANES, c), jnp.bfloat16),
        grid=(n_batch, l1n // kp),
        in_specs=[pl.BlockSpec((kp, 2 * LANES, 2 * LANES), lambda b, k: (k, 0, 0)),
                  pl.BlockSpec((None, kp, LANES, c), lambda b, k: (b, k, 0, 0)),
                  pl.BlockSpec((None, kp, LANES, c), lambda b, k: (b, l1n // kp + k, 0, 0)),
                  pl.BlockSpec((2 * FOURIER_GROUP_DIM, FOURIER_GROUP_DIM), lambda b, k: (0, 0))],
        out_specs=pl.BlockSpec((None, kp, LANES, c), lambda b, k: (b, k, 0, 0)),
        compiler_params=_cparams(2),
        name="dft_stage2",
    )(e.astype(jnp.bfloat16), a4, a4, cs)
    return o.transpose(0, 2, 1, 3).reshape(n_batch * seq_len, c)


def _fourier_ctx(p_all, n_batch, seq_len, ctx_len, tables):
    cc, sc, _, _ = tables
    assert ctx_len == FOURIER_GROUP_DIM
    c = FOURIER_W
    wc = jnp.concatenate([cc, -sc], axis=0).astype(jnp.bfloat16)
    norm = 1.0 / np.sqrt(float(ctx_len) * FOURIER_GROUP_DIM)
    cs = (jnp.concatenate([cc, sc], axis=0) * norm).astype(jnp.bfloat16)
    blk0 = (n_batch * seq_len) // ctx_len
    return pl.pallas_call(
        _dft_ctx_kernel,
        out_shape=jax.ShapeDtypeStruct((n_batch * ctx_len, c), jnp.bfloat16),
        grid=(n_batch,),
        in_specs=[pl.BlockSpec((2 * ctx_len, ctx_len), lambda b: (0, 0)),
                  pl.BlockSpec((ctx_len, c), lambda b: (blk0 + b, P_FOUR // c)),
                  pl.BlockSpec((2 * FOURIER_GROUP_DIM, FOURIER_GROUP_DIM), lambda b: (0, 0))],
        out_specs=pl.BlockSpec((ctx_len, c), lambda b: (b, 0)),
        compiler_params=_cparams(1),
        name="dft_ctx",
    )(wc, p_all, cs)


def _merge_kernel(gate_ref, z_ref, xs_ref, yf_ref, yb_ref, f_ref, x_ref,
                  g1_ref, sc2_ref, sh2_ref, d_ref, nw_ref, lng_ref, lnb_ref,
                  wssm_ref, wfour_ref, wout_ref, xo_ref, u2_ref, u2b_ref,
                  *, tiles_per_seq, n_batch, alpha):
    bidx = jnp.minimum(pl.program_id(0) // tiles_per_seq, n_batch)
    y = (yf_ref[...].astype(jnp.float32) + yb_ref[...].astype(jnp.float32)
         + xs_ref[...].astype(jnp.float32) * d_ref[...])
    h = y * _silu(z_ref[...].astype(jnp.float32))
    h = h * lax.rsqrt(jnp.mean(h * h, axis=-1, keepdims=True) + LN_EPS) * nw_ref[...]
    ssm = jnp.dot(h.astype(jnp.bfloat16), wssm_ref[...], preferred_element_type=jnp.float32)
    four = jnp.dot(f_ref[...], wfour_ref[...], preferred_element_type=jnp.float32)
    gates = _sigmoid(gate_ref[...].astype(jnp.float32))
    merged = gates[:, :D_MODEL] * ssm + gates[:, D_MODEL:] * four
    out = jnp.dot(merged.astype(jnp.bfloat16), wout_ref[...], preferred_element_type=jnp.float32)
    r = alpha * x_ref[...] + g1_ref[pl.ds(bidx, 1), :] * out
    xn = _layer_norm_f32(r) * lng_ref[...] + lnb_ref[...]
    xo_ref[...] = xn
    u2 = (_layer_norm_f32(xn) * (1.0 + sc2_ref[pl.ds(bidx, 1), :])
          + sh2_ref[pl.ds(bidx, 1), :])
    u2_ref[...] = u2
    u2b_ref[...] = u2.astype(jnp.bfloat16)


def _merge(p_all, xbc, yf, yb, f_all, x_all, mod, d_cols, norm_w, ln_g, ln_b,
           w_ssm, w_four, w_out, n_rows, seq_len, n_batch, alpha):
    tm = ROW_TILE
    d = D_MODEL
    row = lambda i: (i, 0)
    const = lambda i: (0, 0)
    out_sds = jax.ShapeDtypeStruct((n_rows, d), jnp.float32)
    return pl.pallas_call(
        functools.partial(_merge_kernel, tiles_per_seq=seq_len // tm, n_batch=n_batch, alpha=alpha),
        out_shape=(out_sds, out_sds, jax.ShapeDtypeStruct((n_rows, d), jnp.bfloat16)),
        grid=(n_rows // tm,),
        in_specs=[
            pl.BlockSpec((tm, 2 * d), lambda i: (i, P_GATE // (2 * d))),
            pl.BlockSpec((tm, D_INNER), lambda i: (i, P_Z // D_INNER)),
            pl.BlockSpec((tm, D_INNER), row),
            pl.BlockSpec((tm, D_INNER), row),
            pl.BlockSpec((tm, D_INNER), row),
            pl.BlockSpec((tm, FOURIER_W), row),
            pl.BlockSpec((tm, d), row),
            pl.BlockSpec((8, d), lambda i: (0, 2)),
            pl.BlockSpec((8, d), lambda i: (0, 4)),
            pl.BlockSpec((8, d), lambda i: (0, 3)),
            pl.BlockSpec((1, D_INNER), const),
            pl.BlockSpec((1, D_INNER), const),
            pl.BlockSpec((1, d), const),
            pl.BlockSpec((1, d), const),
            pl.BlockSpec((D_INNER, d), const),
            pl.BlockSpec((FOURIER_W, d), const),
            pl.BlockSpec((d, d), const),
        ],
        out_specs=(pl.BlockSpec((tm, d), row), pl.BlockSpec((tm, d), row),
                   pl.BlockSpec((tm, d), row)),
        compiler_params=_cparams(1),
        name="merge_postnorm",
    )(p_all, p_all, xbc, yf, yb, f_all, x_all, mod, mod, mod, d_cols, norm_w,
      ln_g, ln_b, w_ssm, w_four, w_out)


def _first_index_of_max(v, iota, big):
    m = jnp.max(v, axis=0, keepdims=True)
    idx = jnp.min(jnp.where(v == m, iota, big), axis=0, keepdims=True)
    return m, idx


def _router_kernel(u_ref, wt_ref, bias_ref, tri_ref, row_ref, rowt_ref, gatet_ref, cnt_ref,
                   off_ref, basetab_ref, tot_ref, base_ref):
    @pl.when(pl.program_id(0) == 0)
    def _():
        base_ref[...] = jnp.zeros_like(base_ref)

    tm = u_ref.shape[0]
    ne, epg = N_EXPERTS, EXPERTS_PER_GROUP
    neg = -jnp.inf
    logits = lax.dot_general(wt_ref[...], u_ref[...], (((1,), (1,)), ((), ())),
                             preferred_element_type=jnp.float32,
                             precision=lax.Precision.HIGHEST)
    scores = _sigmoid(logits)
    sel = scores + bias_ref[...]
    iota_g = lax.broadcasted_iota(jnp.int32, (epg, tm), 0).astype(jnp.float32)
    grp_rows = []
    for g in range(N_EXPERT_GROUPS):
        v = sel[epg * g:epg * (g + 1), :]
        m1, i1 = _first_index_of_max(v, iota_g, epg)
        m2 = jnp.max(jnp.where(iota_g == i1, neg, v), axis=0, keepdims=True)
        grp_rows.append(m1 + m2)
    grp = jnp.concatenate(grp_rows, axis=0)
    iota_n = lax.broadcasted_iota(jnp.int32, (N_EXPERT_GROUPS, tm), 0).astype(jnp.float32)
    chosen = jnp.zeros((N_EXPERT_GROUPS, tm), jnp.float32)
    for _ in range(TOPK_GROUPS):
        _, gi = _first_index_of_max(grp, iota_n, N_EXPERT_GROUPS)
        hit = iota_n == gi
        chosen = jnp.where(hit, 1.0, chosen)
        grp = jnp.where(hit, neg, grp)
    masked = jnp.concatenate(
        [jnp.where(chosen[g:g + 1, :] > 0.0, sel[epg * g:epg * (g + 1), :], neg)
         for g in range(N_EXPERT_GROUPS)], axis=0)
    iota_e = lax.broadcasted_iota(jnp.int32, (ne, tm), 0).astype(jnp.float32)
    picked = jnp.zeros((ne, tm), jnp.float32)
    for _ in range(TOP_K):
        _, ei = _first_index_of_max(masked, iota_e, ne)
        hit = iota_e == ei
        picked = jnp.where(hit, 1.0, picked)
        masked = jnp.where(hit, neg, masked)
    chosen_scores = picked * scores
    gate = chosen_scores / jnp.sum(chosen_scores, axis=0, keepdims=True) * ROUTE_SCALE
    before = jnp.dot(picked.astype(jnp.bfloat16), tri_ref[...],
                     preferred_element_type=jnp.float32)
    cnt = jnp.sum(picked, axis=1, keepdims=True)
    cnt16 = jnp.ceil(cnt * (1.0 / ROW_ALIGN)) * ROW_ALIGN
    cnt16_b = jnp.broadcast_to(cnt16, (ne, LANES))
    e_row = lax.broadcasted_iota(jnp.int32, (ne, ne), 0)
    e_col = lax.broadcasted_iota(jnp.int32, (ne, ne), 1)
    off_b = jnp.dot((e_col < e_row).astype(jnp.float32), cnt16_b,
                    preferred_element_type=jnp.float32,
                    precision=lax.Precision.HIGHEST)
    row1 = picked * (before + off_b[:, 0:1] + 1.0)
    pad = jnp.zeros((EXPERT_PAD - ne, tm), jnp.float32)
    row1_p = jnp.concatenate([row1, pad], axis=0)
    gate_p = jnp.concatenate([gate, pad], axis=0)
    pad_b = jnp.zeros((EXPERT_PAD - ne, LANES), jnp.float32)
    row_ref[...] = row1_p
    rowt_ref[...] = row1_p.T
    gatet_ref[...] = gate_p.T.astype(jnp.bfloat16)
    cnt_ref[...] = jnp.concatenate([cnt16_b, pad_b], axis=0)
    off_ref[...] = jnp.concatenate([off_b, pad_b], axis=0)
    basetab_ref[...] = jnp.broadcast_to(base_ref[...], (ne, LANES))
    base_ref[...] = base_ref[...] + cnt16
    tot_ref[...] = base_ref[...]


def _router(u2, router_w, router_bias, n_tok):
    tm = MOE_TILE
    n_tiles = n_tok // tm
    tri = jnp.triu(jnp.ones((tm, tm), jnp.float32), k=1).astype(jnp.bfloat16)
    tab = lambda rows: (jax.ShapeDtypeStruct((n_tiles, rows, LANES), jnp.float32),
                        pl.BlockSpec((None, rows, LANES), lambda i: (i, 0, 0)))
    outs = [
        (jax.ShapeDtypeStruct((n_tiles, EXPERT_PAD, tm), jnp.float32),
         pl.BlockSpec((None, EXPERT_PAD, tm), lambda i: (i, 0, 0))),
        (jax.ShapeDtypeStruct((n_tok, EXPERT_PAD), jnp.float32),
         pl.BlockSpec((tm, EXPERT_PAD), lambda i: (i, 0))),
        (jax.ShapeDtypeStruct((n_tok, EXPERT_PAD), jnp.bfloat16),
         pl.BlockSpec((tm, EXPERT_PAD), lambda i: (i, 0))),
        tab(EXPERT_PAD), tab(EXPERT_PAD), tab(N_EXPERTS),
        (jax.ShapeDtypeStruct((N_EXPERTS, 1), jnp.float32),
         pl.BlockSpec((N_EXPERTS, 1), lambda i: (0, 0))),
    ]
    return pl.pallas_call(
        _router_kernel,
        out_shape=tuple(o[0] for o in outs),
        grid=(n_tiles,),
        in_specs=[pl.BlockSpec((tm, D_MODEL), lambda i: (i, 0)),
                  pl.BlockSpec((N_EXPERTS, D_MODEL), lambda i: (0, 0)),
                  pl.BlockSpec((N_EXPERTS, 1), lambda i: (0, 0)),
                  pl.BlockSpec((tm, tm), lambda i: (0, 0))],
        out_specs=tuple(o[1] for o in outs),
        scratch_shapes=[pltpu.VMEM((N_EXPERTS, 1), jnp.float32)],
        compiler_params=_cparams(1),
        name="moe_router",
    )(u2, router_w.T, router_bias.reshape(N_EXPERTS, 1), tri)


def _run_copies(cnt, make_copy, pieces):
    done = jnp.int32(0)
    for piece in pieces:
        hit = (cnt & piece) != 0

        @pl.when(hit)
        def _(done=done, piece=piece):
            make_copy(done, piece)

        done = done + (cnt & piece)


def _piece_lists(cnt, off, dst):
    counts, srcs, dsts = [], [], []
    slot_ids = jnp.arange(N_EXPERTS, dtype=jnp.int32)
    for piece in RUN_PIECES:
        has = ((cnt & piece) != 0).astype(jnp.int32)
        above = cnt & ~(2 * piece - 1)
        rank = jnp.cumsum(has, axis=-1) - has
        place = (rank[:, None, :] == slot_ids[None, :, None]).astype(jnp.int32) * has[:, None, :]
        counts.append(jnp.sum(has, axis=-1))
        srcs.append(jnp.sum(place * (off + above)[:, None, :], axis=-1))
        dsts.append(jnp.sum(place * (dst + above)[:, None, :], axis=-1))
    flat = lambda parts: jnp.stack(parts, axis=1).reshape(-1).astype(jnp.int32)
    return flat(counts), flat(srcs), flat(dsts)


def _for_each_piece(tile, pn_ref, psrc_ref, pdst_ref, make_copy, start):
    for p_idx, piece in enumerate(RUN_PIECES):
        seg = tile * len(RUN_PIECES) + p_idx
        base = seg * N_EXPERTS

        def body(k, carry, piece=piece, base=base):
            cp = make_copy(pl.multiple_of(psrc_ref[base + k], ROW_ALIGN),
                           pl.multiple_of(pdst_ref[base + k], ROW_ALIGN), piece)
            if start:
                cp.start()
            else:
                cp.wait()
            return carry

        lax.fori_loop(0, pn_ref[seg], body, 0)


def _split_rows(row1):
    hi = jnp.floor(row1 * (1.0 / ROW_SPLIT))
    lo = row1 - hi * ROW_SPLIT
    return hi.astype(jnp.bfloat16), lo.astype(jnp.bfloat16)


def _dispatch_kernel(pn_ref, psrc_ref, pdst_ref, tail_ref, nblk_ref, used_ref, u_ref, row_ref,
                     bounds_ref, xs_hbm, sorted_ref, zero_ref, sems):
    i = pl.program_id(0)
    last = pl.num_programs(0) - 1
    slot = i % 2
    tm = u_ref.shape[0]
    n_chunks = sorted_ref.shape[1] // tm
    digits = jnp.concatenate(_split_rows(row_ref[...]), axis=0)
    lower = bounds_ref[0:1, :]
    upper = bounds_ref[1:2, :]
    row_e = lax.broadcasted_iota(jnp.int32, (tm, EXPERT_PAD), 0).astype(jnp.float32)
    row_t = lax.broadcasted_iota(jnp.int32, (tm, tm), 0).astype(jnp.float32)

    def sort_chunk(j):
        r0 = float(j * tm)
        in_run = jnp.logical_and(row_e + r0 >= lower, row_e + r0 < upper)
        owner = jnp.concatenate([jnp.where(in_run, ROW_SPLIT, 0.0),
                                 jnp.where(in_run, 1.0, 0.0)], axis=1).astype(jnp.bfloat16)
        want = jnp.dot(owner, digits, preferred_element_type=jnp.float32)
        perm = jnp.where(want == row_t + (r0 + 1.0), 1.0, 0.0).astype(jnp.bfloat16)
        sorted_ref[slot, j * tm:(j + 1) * tm, :] = jnp.dot(
            perm, u_ref[...], preferred_element_type=jnp.float32).astype(jnp.bfloat16)

    for j in range(n_chunks):
        if (j + 1) * tm <= TOP_K * tm:
            sort_chunk(j)
        else:
            pl.when(used_ref[i] > j * tm)(functools.partial(sort_chunk, j))

    def run_copy(buf):
        def make(src, dst, piece):
            return pltpu.make_async_copy(sorted_ref.at[buf, pl.ds(src, piece), :],
                                         xs_hbm.at[pl.ds(dst, piece), :], sems.at[buf])
        return make

    _for_each_piece(i, pn_ref, psrc_ref, pdst_ref, run_copy(slot), start=True)

    @pl.when(i > 0)
    def _():
        _for_each_piece(i - 1, pn_ref, psrc_ref, pdst_ref, run_copy(1 - slot), start=False)

    @pl.when(i == last)
    def _():
        _for_each_piece(i, pn_ref, psrc_ref, pdst_ref, run_copy(slot), start=False)
        zero_ref[...] = jnp.zeros_like(zero_ref)
        blk = zero_ref.shape[0]
        n_blk_total = xs_hbm.shape[0] // blk

        def zero_copy(dst, piece):
            return pltpu.make_async_copy(zero_ref.at[pl.ds(0, piece), :],
                                         xs_hbm.at[pl.ds(dst, piece), :], sems.at[0])

        def tail_body(start):
            def body(e, carry):
                dst = pl.multiple_of(tail_ref[e], ROW_ALIGN)

                def piece_copy(done, piece):
                    cp = zero_copy(pl.multiple_of(dst + done, ROW_ALIGN), piece)
                    if start:
                        cp.start()
                    else:
                        cp.wait()

                _run_copies(tail_ref[N_EXPERTS + e], piece_copy, TAIL_PIECES)
                return carry
            return body

        def blk_body(start):
            def body(b, carry):
                cp = zero_copy(pl.multiple_of(b * blk, blk), blk)
                if start:
                    cp.start()
                else:
                    cp.wait()
                return carry
            return body

        lax.fori_loop(0, N_EXPERTS, tail_body(True), 0)
        lax.fori_loop(nblk_ref[0], n_blk_total, blk_body(True), 0)
        lax.fori_loop(0, N_EXPERTS, tail_body(False), 0)
        lax.fori_loop(nblk_ref[0], n_blk_total, blk_body(False), 0)


def _dispatch(u2b, row1, bounds, pieces, tail, n_used_blk, used, n_blk):
    tm = MOE_TILE
    d = D_MODEL
    n_tok = u2b.shape[0]
    grid_spec = pltpu.PrefetchScalarGridSpec(
        num_scalar_prefetch=6,
        grid=(n_tok // tm,),
        in_specs=[pl.BlockSpec((tm, d), lambda i, *_: (i, 0)),
                  pl.BlockSpec((None, EXPERT_PAD, tm), lambda i, *_: (i, 0, 0)),
                  pl.BlockSpec((None, 2, EXPERT_PAD), lambda i, *_: (i, 0, 0))],
        out_specs=pl.BlockSpec(memory_space=pl.ANY),
        scratch_shapes=[pltpu.VMEM((2, SORTED_ROWS, d), jnp.bfloat16),
                        pltpu.VMEM((MOE_BLK, d), jnp.bfloat16),
                        pltpu.SemaphoreType.DMA((2,))],
    )
    return pl.pallas_call(
        _dispatch_kernel,
        out_shape=jax.ShapeDtypeStruct((n_blk * MOE_BLK, d), jnp.bfloat16),
        grid_spec=grid_spec,
        compiler_params=_cparams(1),
        name="moe_dispatch",
    )(*pieces, tail, n_used_blk, used, u2b, row1, bounds)


def _expert_kernel(blk_e_ref, blk_valid_ref, x_ref, w1_ref, w3_ref, w2_ref, y_ref,
                   w13_ref, w2b_ref):
    i = pl.program_id(0)

    @pl.when(blk_valid_ref[i] > 0)
    def _():
        @pl.when(jnp.logical_or(i == 0, blk_e_ref[i] != blk_e_ref[jnp.maximum(i - 1, 0)]))
        def _():
            w13_ref[:, :EXPERT_FF] = w1_ref[...].astype(jnp.bfloat16)
            w13_ref[:, EXPERT_FF:] = w3_ref[...].astype(jnp.bfloat16)
            w2b_ref[...] = w2_ref[...].astype(jnp.bfloat16)

        h = jnp.dot(x_ref[...], w13_ref[...], preferred_element_type=jnp.float32)
        hb = (_silu(h[:, :EXPERT_FF]) * h[:, EXPERT_FF:]).astype(jnp.bfloat16)
        y_ref[...] = jnp.dot(hb, w2b_ref[...],
                             preferred_element_type=jnp.float32).astype(jnp.bfloat16)

    @pl.when(blk_valid_ref[i] == 0)
    def _():
        y_ref[...] = jnp.zeros_like(y_ref)


def _experts(xs, blk_e, blk_valid, w1, w3, w2, layer, n_blk):
    blk = MOE_BLK
    d = D_MODEL
    grid_spec = pltpu.PrefetchScalarGridSpec(
        num_scalar_prefetch=2,
        grid=(n_blk,),
        in_specs=[
            pl.BlockSpec((blk, d), lambda i, be, bv: (i, 0)),
            pl.BlockSpec((None, None, d, EXPERT_FF), lambda i, be, bv: (layer, be[i], 0, 0)),
            pl.BlockSpec((None, None, d, EXPERT_FF), lambda i, be, bv: (layer, be[i], 0, 0)),
            pl.BlockSpec((None, None, EXPERT_FF, d), lambda i, be, bv: (layer, be[i], 0, 0)),
        ],
        out_specs=pl.BlockSpec((blk, d), lambda i, be, bv: (i, 0)),
        scratch_shapes=[pltpu.VMEM((d, 2 * EXPERT_FF), jnp.bfloat16),
                        pltpu.VMEM((EXPERT_FF, d), jnp.bfloat16)],
    )
    return pl.pallas_call(
        _expert_kernel,
        out_shape=jax.ShapeDtypeStruct((n_blk * blk, d), jnp.bfloat16),
        grid_spec=grid_spec,
        compiler_params=_cparams(1),
        name="moe_experts",
    )(blk_e, blk_valid, xs, w1, w3, w2)


def _final_kernel(pn_ref, psrc_ref, pdst_ref, used_ref, rowt_ref, gatet_ref, cnt_tab_ref,
                  off_tab_ref, y_hbm, x_ref, u_ref, g2_ref, lng_ref, lnb_ref, ws1_ref, ws3_ref,
                  ws2_ref, o_ref, sorted_ref, acc_ref, sems, *, tiles_per_seq, n_batch, alpha):
    i = pl.program_id(0)
    last = pl.num_programs(0) - 1
    slot = i % 2
    bidx = jnp.minimum(i // tiles_per_seq, n_batch)
    tm = x_ref.shape[0]
    n_chunks = sorted_ref.shape[1] // tm

    def run_copy(buf):
        def make(src, dst, piece):
            return pltpu.make_async_copy(y_hbm.at[pl.ds(dst, piece), :],
                                         sorted_ref.at[buf, pl.ds(src, piece), :], sems.at[buf])
        return make

    @pl.when(i == 0)
    def _():
        sorted_ref[...] = jnp.zeros_like(sorted_ref)
        _for_each_piece(i, pn_ref, psrc_ref, pdst_ref, run_copy(slot), start=True)

    @pl.when(i < last)
    def _():
        _for_each_piece(i + 1, pn_ref, psrc_ref, pdst_ref, run_copy(1 - slot), start=True)

    _for_each_piece(i, pn_ref, psrc_ref, pdst_ref, run_copy(slot), start=False)

    digits = jnp.concatenate(_split_rows(rowt_ref[...]), axis=1)
    gate = gatet_ref[...]
    lower = off_tab_ref[:, 0:1]
    upper = lower + cnt_tab_ref[:, 0:1]
    lane_e = lax.broadcasted_iota(jnp.int32, (EXPERT_PAD, tm), 1).astype(jnp.float32)
    lane_t = lax.broadcasted_iota(jnp.int32, (tm, tm), 1).astype(jnp.float32)

    def chunk_sum(j):
        r0 = float(j * tm)
        in_run = jnp.logical_and(lane_e + r0 >= lower, lane_e + r0 < upper)
        owner = jnp.where(in_run, 1.0, 0.0).astype(jnp.bfloat16)
        owner_digits = jnp.concatenate([jnp.where(in_run, ROW_SPLIT, 0.0).astype(jnp.bfloat16),
                                        owner], axis=0)
        want = jnp.dot(digits, owner_digits, preferred_element_type=jnp.float32)
        weight = jnp.dot(gate, owner, preferred_element_type=jnp.float32)
        comb = jnp.where(want == lane_t + (r0 + 1.0), weight, 0.0).astype(jnp.bfloat16)
        return jnp.dot(comb, sorted_ref[slot, j * tm:(j + 1) * tm, :],
                       preferred_element_type=jnp.float32)

    always = TOP_K
    routed = chunk_sum(0)
    for j in range(1, always):
        routed = routed + chunk_sum(j)
    acc_ref[...] = routed
    for j in range(always, n_chunks):
        @pl.when(used_ref[i] > j * tm)
        def _(j=j):
            acc_ref[...] += chunk_sum(j)
    routed = acc_ref[...]
    u = u_ref[...]
    h1 = jnp.dot(u, ws1_ref[...], preferred_element_type=jnp.float32)
    h3 = jnp.dot(u, ws3_ref[...], preferred_element_type=jnp.float32)
    shared = jnp.dot((_silu(h1) * h3).astype(jnp.bfloat16), ws2_ref[...],
                     preferred_element_type=jnp.float32)
    r = alpha * x_ref[...] + g2_ref[pl.ds(bidx, 1), :] * (routed + shared)
    o_ref[...] = _layer_norm_f32(r) * lng_ref[...] + lnb_ref[...]


def _final(pieces, used, row1_t, gate_t, cnt_tab, off_tab, y_slots, x_mid, u2b, mod,
           ln_g, ln_b, ws1, ws3, ws2, n_rows, seq_len, n_batch, alpha):
    tm = MOE_TILE
    d = D_MODEL
    row = lambda i, *_: (i, 0)
    const = lambda i, *_: (0, 0)
    tab_spec = pl.BlockSpec((None, EXPERT_PAD, LANES), lambda i, *_: (i, 0, 0))
    grid_spec = pltpu.PrefetchScalarGridSpec(
        num_scalar_prefetch=4,
        grid=(n_rows // tm,),
        in_specs=[
            pl.BlockSpec((tm, EXPERT_PAD), row),
            pl.BlockSpec((tm, EXPERT_PAD), row),
            tab_spec,
            tab_spec,
            pl.BlockSpec(memory_space=pl.ANY),
            pl.BlockSpec((tm, d), row),
            pl.BlockSpec((tm, d), row),
            pl.BlockSpec((8, d), lambda i, *_: (0, 5)),
            pl.BlockSpec((1, d), const),
            pl.BlockSpec((1, d), const),
            pl.BlockSpec((d, SHARED_FF), const),
            pl.BlockSpec((d, SHARED_FF), const),
            pl.BlockSpec((SHARED_FF, d), const),
        ],
        out_specs=pl.BlockSpec((tm, d), row),
        scratch_shapes=[pltpu.VMEM((2, SORTED_ROWS, d), jnp.bfloat16),
                        pltpu.VMEM((tm, d), jnp.float32),
                        pltpu.SemaphoreType.DMA((2,))],
    )
    return pl.pallas_call(
        functools.partial(_final_kernel, tiles_per_seq=seq_len // tm, n_batch=n_batch, alpha=alpha),
        out_shape=jax.ShapeDtypeStruct((n_rows, d), jnp.float32),
        grid_spec=grid_spec,
        compiler_params=_cparams(1),
        name="moe_combine_postnorm",
    )(*pieces, used, row1_t, gate_t, cnt_tab, off_tab, y_slots, x_mid, u2b, mod,
      ln_g, ln_b, ws1, ws3, ws2)


def _moe_sublayer(x_mid, u2, u2b, mod, ln_g, ln_b, router_w, router_bias, w1, w3, w2, layer,
                  ws1, ws3, ws2, n_rows, seq_len, n_batch, alpha):
    bf = jnp.bfloat16
    blk = MOE_BLK
    n_tiles = n_rows // MOE_TILE
    row1, row1_t, gate_t, cnt_tab, off_tab, base_tab, total = _router(
        u2, router_w, router_bias, n_rows)
    as_int = lambda t: t[:, :N_EXPERTS, 0].astype(jnp.int32)
    cnt, off, base = as_int(cnt_tab), as_int(off_tab), as_int(base_tab)
    used = off[:, -1] + cnt[:, -1]
    bounds = jnp.stack([off_tab[:, :, 0], off_tab[:, :, 0] + cnt_tab[:, :, 0]], axis=1)
    total = total.reshape(N_EXPERTS).astype(jnp.int32)
    padded = (total + blk - 1) // blk * blk
    pends = jnp.cumsum(padded)
    pstart = pends - padded
    dst = pstart[None, :] + base
    tail = jnp.concatenate([pstart + total, padded - total])
    n_used_blk = (pends[-1:] // blk).astype(jnp.int32)
    max_rows = n_rows * TOP_K + n_tiles * N_EXPERTS * (ROW_ALIGN - 1) + N_EXPERTS * (blk - 1)
    n_blk = -(-max_rows // blk)
    blk_start = jnp.arange(n_blk, dtype=jnp.int32) * blk
    blk_e = jnp.minimum(jnp.sum((blk_start[:, None] >= pends[None, :]).astype(jnp.int32), axis=1),
                        N_EXPERTS - 1)
    blk_valid = (blk_start < pends[-1]).astype(jnp.int32)
    pieces = _piece_lists(cnt, off, dst)
    xs = _dispatch(u2b, row1, bounds, pieces, tail, n_used_blk, used, n_blk)
    y_slots = _experts(xs, blk_e, blk_valid, w1, w3, w2, layer, n_blk)
    return _final(pieces, used, row1_t, gate_t, cnt_tab, off_tab, y_slots, x_mid, u2b, mod,
                  ln_g, ln_b, ws1.astype(bf), ws3.astype(bf), ws2.astype(bf),
                  n_rows, seq_len, n_batch, alpha)


def _pack_in_proj(w_in):
    d = w_in.shape[0]
    gap = jnp.zeros((d, DT_BWD_LANE - SSM_HEADS), w_in.dtype)
    rest = jnp.zeros((d, P_FOUR - P_DT - DT_BWD_LANE - SSM_HEADS), w_in.dtype)
    return jnp.concatenate([
        w_in[:, OFF_XBC:OFF_DT],
        w_in[:, OFF_DT:OFF_DT + SSM_HEADS], gap, w_in[:, OFF_DT + SSM_HEADS:OFF_FOUR], rest,
        w_in[:, OFF_FOUR:OFF_GATE], w_in[:, OFF_GATE:], w_in[:, :OFF_XBC]],
        axis=1).astype(jnp.bfloat16)


def _pack_head_rows(v):
    out = jnp.zeros((1, LANES), jnp.float32)
    v = v.reshape(2, SSM_HEADS).astype(jnp.float32)
    out = out.at[0, 0:SSM_HEADS].set(v[0])
    return out.at[0, DT_BWD_LANE:DT_BWD_LANE + SSM_HEADS].set(v[1])


def kernel(x, c, ctx, c_ctx, w_ada, b_ada, w_in, conv_w, conv_b, dt_bias, a_log, d_skip,
           ssm_norm_w, w_br_ssm, w_br_four, w_out, ln1_g, ln1_b, ln2_g, ln2_b,
           router_w, router_bias, w1, w3, w2, ws1, ws3, ws2):
    n_batch, seq_len, d = x.shape
    ctx_len = ctx.shape[1]
    depth = w_ada.shape[0]
    bf = jnp.bfloat16
    alpha = float((2 * depth) ** 0.25)
    n_lat = n_batch * seq_len
    assert d == D_MODEL and n_batch + 1 <= 8
    assert seq_len % K1_TM == 0 and (n_batch * ctx_len) % K1_TM == 0
    assert seq_len % (LANES * 8) == 0

    x_all = _assemble_stream(x, ctx)
    c_rows = jnp.zeros((8, d), jnp.float32).at[:n_batch].set(c).at[n_batch].set(c_ctx)
    mod_all = _ada_mod(c_rows, w_ada, b_ada)
    tables = _dft_tables(seq_len)

    for i in range(depth):
        last = i == depth - 1
        mod = mod_all[i]
        p_all = _in_proj(x_all, mod, _pack_in_proj(w_in[i]), seq_len, n_batch)
        a_row = _pack_head_rows(-jnp.exp(a_log[i].astype(jnp.float32)))
        xbc, dt_all, cum_all, cumt_all, bt_all = _conv(
            p_all, conv_w[i], conv_b[i], _pack_head_rows(dt_bias[i]), a_row,
            seq_len, n_lat, ctx_len)
        yf, yb = _ssd(xbc, dt_all, cum_all, cumt_all, bt_all, n_batch, seq_len, ctx_len)
        f_all = _fourier_latent(p_all, n_batch, seq_len, tables)
        n_rows = n_lat if last else x_all.shape[0]
        if not last:
            f_all = jnp.concatenate(
                [f_all, _fourier_ctx(p_all, n_batch, seq_len, ctx_len, tables)], axis=0)
        d_cols = jnp.repeat(d_skip[i].astype(jnp.float32), SSM_HEAD_DIM).reshape(1, D_INNER)
        x_mid, u2, u2b = _merge(p_all, xbc, yf, yb, f_all, x_all, mod, d_cols,
                           ssm_norm_w[i].reshape(1, D_INNER), ln1_g[i].reshape(1, d),
                           ln1_b[i].reshape(1, d), w_br_ssm[i].astype(bf),
                           w_br_four[i].astype(bf), w_out[i].astype(bf),
                           n_rows, seq_len, n_batch, alpha)
        x_all = _moe_sublayer(x_mid, u2, u2b, mod, ln2_g[i].reshape(1, d), ln2_b[i].reshape(1, d),
                              router_w[i], router_bias[i], w1, w3, w2, i,
                              ws1[i], ws3[i], ws2[i], n_rows, seq_len, n_batch, alpha)
    return x_all[:n_lat].reshape(n_batch, seq_len, d)
```

```python
import functools

import jax
import jax.numpy as jnp
import numpy as np
from jax import lax
from jax.experimental import pallas as pl
from jax.experimental.pallas import tpu as pltpu

D_MODEL = 1024
GRID_W = 64
POS_BASE = 10000.0
LN_EPS = 1e-6

SSM_HEADS = 24
SSM_HEAD_DIM = 64
D_INNER = SSM_HEADS * SSM_HEAD_DIM
SSM_GROUPS = 4
HEADS_PER_GROUP = SSM_HEADS // SSM_GROUPS
SSM_STATE = 128
CONV_W = 5
CONV_CH = D_INNER + 2 * SSM_GROUPS * SSM_STATE
CHUNK = 128

FOURIER_GROUPS = 4
FOURIER_GROUP_DIM = 256
FOURIER_W = FOURIER_GROUPS * FOURIER_GROUP_DIM

OFF_XBC = D_INNER
OFF_DT = OFF_XBC + CONV_CH
OFF_FOUR = OFF_DT + 2 * SSM_HEADS
OFF_GATE = OFF_FOUR + FOURIER_W

N_EXPERTS = 64
TOP_K = 8
N_EXPERT_GROUPS = 8
EXPERTS_PER_GROUP = N_EXPERTS // N_EXPERT_GROUPS
TOPK_GROUPS = 4
EXPERT_FF = 256
SHARED_FF = 256
ROUTE_SCALE = 2.5

LANES = 128
VMEM_LIMIT_BYTES = 56 * 1024 * 1024

P_XBC = 0
P_DT = CONV_CH
P_FOUR = P_DT + SSM_GROUPS * LANES
P_GATE = P_FOUR + FOURIER_W
P_Z = P_GATE + 2 * D_MODEL
P_WIDTH = P_Z + D_INNER
DT_BWD_LANE = 32
CUMT_ROWS = 64

DFT2_ROWS_PER_STEP = 4
K1_TM = 512
K1_TN = 1536
ROW_TILE = 256
MOE_TILE = 512
MOE_BLK = 1024
ROW_ALIGN = 16
EXPERT_PAD = LANES
ROW_SPLIT = 64.0
RUN_PIECES = tuple(MOE_TILE >> s for s in range(6))
TAIL_PIECES = tuple(p for p in RUN_PIECES if p < MOE_BLK)
SORTED_ROWS = -(-(MOE_TILE * TOP_K + N_EXPERTS * (ROW_ALIGN - 1)) // MOE_TILE) * MOE_TILE


def _cparams(n_axes=1):
    return pltpu.CompilerParams(
        dimension_semantics=("arbitrary",) * n_axes,
        vmem_limit_bytes=VMEM_LIMIT_BYTES)


def _layer_norm_f32(x):
    mu = jnp.mean(x, axis=-1, keepdims=True)
    xc = x - mu
    var = jnp.mean(xc * xc, axis=-1, keepdims=True)
    return xc * lax.rsqrt(var + LN_EPS)


def _sigmoid(x):
    return 1.0 / (1.0 + jnp.exp(-x))


def _silu(x):
    return x * _sigmoid(x)


def _pos_kernel(x_ref, ctx_ref, er_ref, ec_ref, o_ref, *, n_lat):
    i = pl.program_id(0)
    half = D_MODEL // 2

    @pl.when(i < n_lat)
    def _():
        ec = ec_ref[...]
        for r in range(8):
            rows = slice(GRID_W * r, GRID_W * (r + 1))
            o_ref[rows, :half] = x_ref[rows, :half] + er_ref[r:r + 1, :]
            o_ref[rows, half:] = x_ref[rows, half:] + ec

    @pl.when(i >= n_lat)
    def _():
        o_ref[...] = ctx_ref[...]


def _assemble_stream(x, ctx):
    b, l, d = x.shape
    lc = ctx.shape[1]
    tile = 8 * GRID_W
    n_lat = (b * l) // tile
    n_ctx = (b * lc) // tile
    rows = l // GRID_W
    quarter = D_MODEL // 4
    omega = 1.0 / (POS_BASE ** (jnp.arange(quarter, dtype=jnp.float32) / quarter))
    ang_r = jnp.arange(rows, dtype=jnp.float32)[:, None] * omega
    ang_c = jnp.arange(GRID_W, dtype=jnp.float32)[:, None] * omega
    emb_r = jnp.concatenate([jnp.sin(ang_r), jnp.cos(ang_r)], -1)
    emb_c = jnp.concatenate([jnp.sin(ang_c), jnp.cos(ang_c)], -1)
    tiles_per_seq = l // tile
    return pl.pallas_call(
        functools.partial(_pos_kernel, n_lat=n_lat),
        out_shape=jax.ShapeDtypeStruct((b * l + b * lc, d), jnp.float32),
        grid=(n_lat + n_ctx,),
        in_specs=[
            pl.BlockSpec((tile, d), lambda i: (jnp.minimum(i, n_lat - 1), 0)),
            pl.BlockSpec((tile, d), lambda i: (jnp.maximum(i - n_lat, 0), 0)),
            pl.BlockSpec((8, d // 2), lambda i: (i % tiles_per_seq, 0)),
            pl.BlockSpec((GRID_W, d // 2), lambda i: (0, 0)),
        ],
        out_specs=pl.BlockSpec((tile, d), lambda i: (i, 0)),
        compiler_params=_cparams(1),
        name="assemble_stream",
    )(x.reshape(b * l, d), ctx.reshape(b * lc, d), emb_r, emb_c)


def _ada_kernel(c_ref, w_ref, b_ref, o_ref):
    c = c_ref[...]
    o_ref[...] = jnp.dot(_silu(c), w_ref[...], preferred_element_type=jnp.float32,
                         precision=lax.Precision.HIGHEST) + b_ref[...]


def _ada_mod(c_rows, w_ada, b_ada):
    depth, d, n6 = w_ada.shape
    tn = 1536
    return pl.pallas_call(
        _ada_kernel,
        out_shape=jax.ShapeDtypeStruct((depth, 8, n6), jnp.float32),
        grid=(depth, n6 // tn),
        in_specs=[
            pl.BlockSpec((8, d), lambda a, j: (0, 0)),
            pl.BlockSpec((None, d, tn), lambda a, j: (a, 0, j)),
            pl.BlockSpec((None, 1, tn), lambda a, j: (a, 0, j)),
        ],
        out_specs=pl.BlockSpec((None, 8, tn), lambda a, j: (a, 0, j)),
        compiler_params=_cparams(2),
        name="ada_mod",
    )(c_rows, w_ada, b_ada.reshape(depth, 1, n6))


def _k1_kernel(x_ref, sc_ref, sh_ref, w_ref, o_ref, four_ref, u_ref, *, tiles_per_seq, n_batch):
    i = pl.program_id(0)
    bidx = jnp.minimum(i // tiles_per_seq, n_batch)
    xn = _layer_norm_f32(x_ref[...])
    u = xn * (1.0 + sc_ref[pl.ds(bidx, 1), :]) + sh_ref[pl.ds(bidx, 1), :]
    u_ref[...] = u.astype(jnp.bfloat16)
    for c in range(o_ref.shape[1] // K1_TN):
        cols = slice(c * K1_TN, (c + 1) * K1_TN)
        res = jnp.dot(u_ref[...], w_ref[:, cols],
                      preferred_element_type=jnp.float32).astype(jnp.bfloat16)
        o_ref[:, cols] = res
        if c * K1_TN == P_FOUR:
            four_ref[...] = res[:, :FOURIER_W]


def _in_proj(x_all, mod, w_all, seq_len, n_batch):
    nt, d = x_all.shape
    tm = K1_TM
    assert P_FOUR % K1_TN == 0 and FOURIER_W <= K1_TN
    return pl.pallas_call(
        functools.partial(_k1_kernel, tiles_per_seq=seq_len // tm, n_batch=n_batch),
        out_shape=(jax.ShapeDtypeStruct((nt, P_WIDTH), jnp.bfloat16),
                   jax.ShapeDtypeStruct((nt, FOURIER_W), jnp.bfloat16)),
        grid=(nt // tm,),
        in_specs=[
            pl.BlockSpec((tm, d), lambda i: (i, 0)),
            pl.BlockSpec((8, d), lambda i: (0, 1)),
            pl.BlockSpec((8, d), lambda i: (0, 0)),
            pl.BlockSpec((d, P_WIDTH), lambda i: (0, 0), pipeline_mode=pl.Buffered(1)),
        ],
        out_specs=(pl.BlockSpec((tm, P_WIDTH), lambda i: (i, 0)),
                   pl.BlockSpec((tm, FOURIER_W), lambda i: (i, 0))),
        scratch_shapes=[pltpu.VMEM((tm, d), jnp.bfloat16)],
        compiler_params=_cparams(1),
        name="in_proj",
    )(x_all, mod, mod, w_all)


def _softplus(x):
    return jnp.maximum(x, 0.0) + jnp.log1p(jnp.exp(-jnp.abs(x)))


def _conv_kernel(cur_ref, prev_ref, next_ref, w_ref, b_ref, dtraw_ref, bias_ref, a_ref,
                 o_ref, dt_ref, cum_ref, cumt_ref, bt_ref, ext_ref, *, tiles_per_seq, n_lat):
    i = pl.program_id(0)
    is_ctx = i >= n_lat
    is_start = jnp.logical_or(i % tiles_per_seq == 0, is_ctx)
    is_end = jnp.logical_or(i % tiles_per_seq == tiles_per_seq - 1, is_ctx)
    halo = prev_ref.shape[0]
    rows = cur_ref.shape[0]
    ext_ref[0:halo, :] = jnp.where(is_start, 0.0, prev_ref[...].astype(jnp.float32))
    ext_ref[halo:halo + rows, :] = cur_ref[...].astype(jnp.float32)
    ext_ref[halo + rows:, :] = jnp.where(is_end, 0.0, next_ref[...].astype(jnp.float32))

    def conv_lane_block(c, carry):
        lanes = pl.ds(pl.multiple_of(c * LANES, LANES), LANES)
        acc = b_ref[:, lanes] + w_ref[0:1, lanes] * ext_ref[pl.ds(halo - 2, rows), lanes]
        for k in range(1, CONV_W):
            acc = acc + w_ref[k:k + 1, lanes] * ext_ref[pl.ds(halo - 2 + k, rows), lanes]
        o_ref[:, lanes] = _silu(acc).astype(jnp.bfloat16)
        return carry

    lax.fori_loop(0, CONV_CH // LANES, conv_lane_block, 0)

    q = CHUNK
    dt = _softplus(dtraw_ref[...].astype(jnp.float32) + bias_ref[...])
    dt_ref[...] = dt
    adt = dt * a_ref[...]
    row_i = lax.broadcasted_iota(jnp.int32, (q, q), 0)
    col_i = lax.broadcasted_iota(jnp.int32, (q, q), 1)
    lower = (row_i >= col_i).astype(jnp.float32)
    upper = (row_i <= col_i).astype(jnp.float32)
    is_fwd = lax.broadcasted_iota(jnp.int32, (1, LANES), 1) < DT_BWD_LANE
    for ch in range(rows // q):
        rs = slice(q * ch, q * (ch + 1))
        cum_f = jnp.dot(lower, adt[rs, :], preferred_element_type=jnp.float32,
                        precision=lax.Precision.HIGHEST)
        cum_b = jnp.dot(upper, adt[rs, :], preferred_element_type=jnp.float32,
                        precision=lax.Precision.HIGHEST)
        cum = jnp.where(is_fwd, cum_f, cum_b)
        cum_ref[rs, :] = cum
        cumt_ref[ch] = cum.T[:CUMT_ROWS, :]
        for g in range(SSM_GROUPS):
            b_cols = slice(D_INNER + SSM_STATE * g, D_INNER + SSM_STATE * (g + 1))
            bt_ref[ch, SSM_STATE * g:SSM_STATE * (g + 1), :] = (
                o_ref[rs, b_cols].astype(jnp.float32).T.astype(jnp.bfloat16))


def _conv(p_all, conv_w, conv_b, dt_bias_row, a_row, seq_len, n_lat_rows, ctx_len):
    nt = p_all.shape[0]
    tl = ROW_TILE
    assert ctx_len == tl, "context sequences must span exactly one conv tile"
    halo = 16
    hb = tl // halo
    n_halo_blocks = nt // halo
    dtw = LANES
    cpt = tl // CHUNK
    w8 = jnp.zeros((8, CONV_CH), jnp.float32).at[:CONV_W].set(conv_w)
    return pl.pallas_call(
        functools.partial(_conv_kernel, tiles_per_seq=seq_len // tl, n_lat=n_lat_rows // tl),
        out_shape=(jax.ShapeDtypeStruct((nt, CONV_CH), jnp.bfloat16),
                   jax.ShapeDtypeStruct((nt, dtw), jnp.float32),
                   jax.ShapeDtypeStruct((nt, dtw), jnp.float32),
                   jax.ShapeDtypeStruct((nt // CHUNK, CUMT_ROWS, CHUNK), jnp.float32),
                   jax.ShapeDtypeStruct((nt // CHUNK, SSM_GROUPS * SSM_STATE, CHUNK), jnp.bfloat16)),
        grid=(nt // tl,),
        in_specs=[
            pl.BlockSpec((tl, CONV_CH), lambda i: (i, 0)),
            pl.BlockSpec((halo, CONV_CH), lambda i: (jnp.maximum(i * hb - 1, 0), 0)),
            pl.BlockSpec((halo, CONV_CH),
                         lambda i: (jnp.minimum((i + 1) * hb, n_halo_blocks - 1), 0)),
            pl.BlockSpec((8, CONV_CH), lambda i: (0, 0)),
            pl.BlockSpec((1, CONV_CH), lambda i: (0, 0)),
            pl.BlockSpec((tl, dtw), lambda i: (i, P_DT // dtw)),
            pl.BlockSpec((1, dtw), lambda i: (0, 0)),
            pl.BlockSpec((1, dtw), lambda i: (0, 0)),
        ],
        out_specs=(pl.BlockSpec((tl, CONV_CH), lambda i: (i, 0)),
                   pl.BlockSpec((tl, dtw), lambda i: (i, 0)),
                   pl.BlockSpec((tl, dtw), lambda i: (i, 0)),
                   pl.BlockSpec((cpt, CUMT_ROWS, CHUNK), lambda i: (i, 0, 0)),
                   pl.BlockSpec((cpt, SSM_GROUPS * SSM_STATE, CHUNK), lambda i: (i, 0, 0))),
        scratch_shapes=[pltpu.VMEM((tl + 2 * halo, CONV_CH), jnp.float32)],
        compiler_params=_cparams(1),
        name="dwconv_silu",
    )(p_all, p_all, p_all, w8, conv_b.reshape(1, CONV_CH), p_all, dt_bias_row, a_row)


def _ssd_direction(refs, s_ref, y_ref, gi, *, lane0, forward):
    x_ref, b_ref, c_ref, dt_ref, cum_ref, cumt_ref, bt_ref = refs
    q = CHUNK
    gw = HEADS_PER_GROUP * SSM_HEAD_DIM
    row_i = lax.broadcasted_iota(jnp.int32, (q, q), 0)
    col_i = lax.broadcasted_iota(jnp.int32, (q, q), 1)
    tri = (row_i >= col_i) if forward else (row_i <= col_i)
    lane_lo = lax.broadcasted_iota(jnp.int32, (q, LANES), 1) < SSM_HEAD_DIM
    lane_lo_row = lax.broadcasted_iota(jnp.int32, (1, LANES), 1) < SSM_HEAD_DIM

    g_lanes = slice(LANES * gi, LANES * (gi + 1))
    dt = dt_ref[...]
    cum = cum_ref[...]
    cum_t = cumt_ref[...]
    total = cum[q - 1:q, :] if forward else cum[0:1, :]

    cm = c_ref[:, g_lanes]
    bm = b_ref[:, g_lanes]
    scores = lax.dot_general(cm, bm, (((1,), (1,)), ((), ())),
                             preferred_element_type=jnp.float32)
    bm_t = bt_ref[SSM_STATE * gi:SSM_STATE * (gi + 1), :]

    for pr in range(HEADS_PER_GROUP // 2):
        r0 = lane0 + 2 * pr
        r1 = r0 + 1
        lanes = slice(gw * gi + LANES * pr, gw * gi + LANES * (pr + 1))
        xp = x_ref[:, lanes].astype(jnp.float32)
        dt_pair = jnp.where(lane_lo, dt[:, r0:r0 + 1], dt[:, r1:r1 + 1])
        cum_pair = jnp.where(lane_lo, cum[:, r0:r0 + 1], cum[:, r1:r1 + 1])
        tot_pair = jnp.where(lane_lo_row, total[:, r0:r0 + 1], total[:, r1:r1 + 1])
        xdt = xp * dt_pair
        l0 = jnp.exp(jnp.where(tri, cum[:, r0:r0 + 1] - cum_t[r0:r0 + 1, :], -jnp.inf))
        l1 = jnp.exp(jnp.where(tri, cum[:, r1:r1 + 1] - cum_t[r1:r1 + 1, :], -jnp.inf))
        w = jnp.concatenate([(scores * l0).astype(jnp.bfloat16),
                             (scores * l1).astype(jnp.bfloat16)], axis=1)
        xdt_b = xdt.astype(jnp.bfloat16)
        zero = jnp.zeros_like(xdt_b)
        rhs = jnp.concatenate([jnp.where(lane_lo, xdt_b, zero),
                               jnp.where(lane_lo, zero, xdt_b)], axis=0)
        y_diag = jnp.dot(w, rhs, preferred_element_type=jnp.float32)
        s_old = s_ref[:, lanes]
        y_off = jnp.dot(cm, s_old.astype(jnp.bfloat16),
                        preferred_element_type=jnp.float32) * jnp.exp(cum_pair)
        y_ref[:, lanes] = (y_diag + y_off).astype(jnp.bfloat16)
        decayed = (xdt * jnp.exp(tot_pair - cum_pair)).astype(jnp.bfloat16)
        s_ref[:, lanes] = jnp.exp(tot_pair) * s_old + jnp.dot(
            bm_t, decayed, preferred_element_type=jnp.float32)


def _ssd_kernel(*refs):
    n_in = 7
    fwd_refs, bwd_refs = refs[:n_in], refs[n_in:2 * n_in]
    yf_ref, yb_ref, sf_ref, sb_ref = refs[2 * n_in:]

    @pl.when(pl.program_id(2) == 0)
    def _():
        sf_ref[...] = jnp.zeros_like(sf_ref)
        sb_ref[...] = jnp.zeros_like(sb_ref)

    for gi in range(SSM_GROUPS):
        _ssd_direction(fwd_refs, sf_ref, yf_ref, gi, lane0=HEADS_PER_GROUP * gi, forward=True)
        _ssd_direction(bwd_refs, sb_ref, yb_ref, gi,
                       lane0=DT_BWD_LANE + HEADS_PER_GROUP * gi, forward=False)


def _ssd(xbc, dt_all, cum_all, cumt_all, bt_all, n_batch, seq_len, ctx_len):
    nt = xbc.shape[0]
    q = CHUNK
    nc_lat = seq_len // q
    nc_ctx = ctx_len // q
    ctx_blk0 = (n_batch * seq_len) // q
    n_steps = nc_ctx + nc_lat
    gw = D_INNER
    sw = SSM_GROUPS * SSM_STATE
    b_blk0 = D_INNER // sw
    c_blk0 = b_blk0 + 1

    def fwd_row(b, j):
        return jnp.where(j < nc_ctx, ctx_blk0 + nc_ctx * b + j, nc_lat * b + (j - nc_ctx))

    def bwd_row(b, j):
        return jnp.where(j < nc_ctx, ctx_blk0 + nc_ctx * b + (nc_ctx - 1 - j),
                         nc_lat * b + (n_steps - 1 - j))

    def specs(row):
        return [
            pl.BlockSpec((q, gw), lambda b, g, j: (row(b, j), g)),
            pl.BlockSpec((q, sw), lambda b, g, j: (row(b, j), b_blk0 + g)),
            pl.BlockSpec((q, sw), lambda b, g, j: (row(b, j), c_blk0 + g)),
            pl.BlockSpec((q, LANES), lambda b, g, j: (row(b, j), 0)),
            pl.BlockSpec((q, LANES), lambda b, g, j: (row(b, j), 0)),
            pl.BlockSpec((None, CUMT_ROWS, q), lambda b, g, j: (row(b, j), 0, 0)),
            pl.BlockSpec((None, sw, q), lambda b, g, j: (row(b, j), 0, 0)),
        ]

    out_sds = jax.ShapeDtypeStruct((nt, D_INNER), jnp.bfloat16)
    operands = (xbc, xbc, xbc, dt_all, cum_all, cumt_all, bt_all)
    return pl.pallas_call(
        _ssd_kernel,
        out_shape=(out_sds, out_sds),
        grid=(n_batch, 1, n_steps),
        in_specs=specs(fwd_row) + specs(bwd_row),
        out_specs=(pl.BlockSpec((q, gw), lambda b, g, j: (fwd_row(b, j), g)),
                   pl.BlockSpec((q, gw), lambda b, g, j: (bwd_row(b, j), g))),
        scratch_shapes=[pltpu.VMEM((SSM_STATE, gw), jnp.float32),
                        pltpu.VMEM((SSM_STATE, gw), jnp.float32)],
        compiler_params=_cparams(3),
        name="ssd_scan",
    )(*operands, *operands)


def _dft_tables(seq_len):
    l1n = seq_len // LANES
    two_pi = 2.0 * np.pi
    gd = FOURIER_GROUP_DIM
    jj = jnp.arange(gd, dtype=jnp.int32)
    ang_c = ((jj[:, None] * jj[None, :]) % gd).astype(jnp.float32) * (two_pi / gd)
    cc, sc = jnp.cos(ang_c), jnp.sin(ang_c)
    k1 = jnp.arange(l1n, dtype=jnp.int32)
    ang1 = ((k1[:, None] * k1[None, :]) % l1n).astype(jnp.float32) * (two_pi / l1n)
    w1 = jnp.concatenate([jnp.cos(ang1), -jnp.sin(ang1)], axis=0)
    k2 = jnp.arange(LANES, dtype=jnp.int32)
    kk = k1[:, None, None] + l1n * k2[None, :, None]
    ang2 = ((kk * k2[None, None, :]) % seq_len).astype(jnp.float32) * (two_pi / seq_len)
    er, ei = jnp.cos(ang2), -jnp.sin(ang2)
    e = jnp.concatenate([jnp.concatenate([er, -ei], axis=2),
                         jnp.concatenate([ei, er], axis=2)], axis=1)
    return cc, sc, w1, e


def _dft1_kernel(w_ref, x_ref, o_ref):
    o_ref[...] = jnp.dot(w_ref[...], x_ref[...],
                         preferred_element_type=jnp.float32).astype(jnp.bfloat16)


def _channel_mix(gr, gi, cs_ref, o_ref):
    gd = FOURIER_GROUP_DIM
    for g in range(FOURIER_GROUPS):
        cols = slice(gd * g, gd * (g + 1))
        lhs = jnp.concatenate([gr[:, cols], gi[:, cols]], axis=1).astype(jnp.bfloat16)
        o_ref[:, cols] = jnp.dot(lhs, cs_ref[...],
                                 preferred_element_type=jnp.float32).astype(jnp.bfloat16)


def _dft2_kernel(e_ref, ar_ref, ai_ref, cs_ref, o_ref):
    for k in range(e_ref.shape[0]):
        a = jnp.concatenate([ar_ref[k], ai_ref[k]], axis=0)
        g = jnp.dot(e_ref[k], a, preferred_element_type=jnp.float32)
        half = g.shape[0] // 2
        _channel_mix(g[:half], g[half:], cs_ref, o_ref.at[k])


def _dft_ctx_kernel(w_ref, x_ref, cs_ref, o_ref):
    g = jnp.dot(w_ref[...], x_ref[...], preferred_element_type=jnp.float32)
    half = g.shape[0] // 2
    _channel_mix(g[:half], g[half:], cs_ref, o_ref)


def _fourier_latent(four_all, n_batch, seq_len, tables):
    cc, sc, w1, e = tables
    c = FOURIER_W
    l1n = seq_len // LANES
    ncol = LANES * c
    four = four_all[:n_batch * seq_len].reshape(n_batch, l1n, ncol)
    tn = 4096
    a = pl.pallas_call(
        _dft1_kernel,
        out_shape=jax.ShapeDtypeStruct((n_batch, 2 * l1n, ncol), jnp.bfloat16),
        grid=(n_batch, ncol // tn),
        in_specs=[pl.BlockSpec((2 * l1n, l1n), lambda b, j: (0, 0)),
                  pl.BlockSpec((None, l1n, tn), lambda b, j: (b, 0, j))],
        out_specs=pl.BlockSpec((None, 2 * l1n, tn), lambda b, j: (b, 0, j)),
        compiler_params=_cparams(2),
        name="dft_stage1",
    )(w1.astype(jnp.bfloat16), four)
    a4 = a.reshape(n_batch, 2 * l1n, LANES, c)
    norm = 1.0 / np.sqrt(float(seq_len) * FOURIER_GROUP_DIM)
    cs = (jnp.concatenate([cc, sc], axis=0) * norm).astype(jnp.bfloat16)
    kp = DFT2_ROWS_PER_STEP
    o = pl.pallas_call(
        _dft2_kernel,
        out_shape=jax.ShapeDtypeStruct((n_batch, l1n, LANES, c), jnp.bfloat16),
        grid=(n_batch, l1n // kp),
        in_specs=[pl.BlockSpec((kp, 2 * LANES, 2 * LANES), lambda b, k: (k, 0, 0)),
                  pl.BlockSpec((None, kp, LANES, c), lambda b, k: (b, k, 0, 0)),
                  pl.BlockSpec((None, kp, LANES, c), lambda b, k: (b, l1n // kp + k, 0, 0)),
                  pl.BlockSpec((2 * FOURIER_GROUP_DIM, FOURIER_GROUP_DIM), lambda b, k: (0, 0))],
        out_specs=pl.BlockSpec((None, kp, LANES, c), lambda b, k: (b, k, 0, 0)),
        compiler_params=_cparams(2),
        name="dft_stage2",
    )(e.astype(jnp.bfloat16), a4, a4, cs)
    return o.transpose(0, 2, 1, 3).reshape(n_batch * seq_len, c)


def _fourier_ctx(p_all, n_batch, seq_len, ctx_len, tables):
    cc, sc, _, _ = tables
    assert ctx_len == FOURIER_GROUP_DIM
    c = FOURIER_W
    wc = jnp.concatenate([cc, -sc], axis=0).astype(jnp.bfloat16)
    norm = 1.0 / np.sqrt(float(ctx_len) * FOURIER_GROUP_DIM)
    cs = (jnp.concatenate([cc, sc], axis=0) * norm).astype(jnp.bfloat16)
    blk0 = (n_batch * seq_len) // ctx_len
    return pl.pallas_call(
        _dft_ctx_kernel,
        out_shape=jax.ShapeDtypeStruct((n_batch * ctx_len, c), jnp.bfloat16),
        grid=(n_batch,),
        in_specs=[pl.BlockSpec((2 * ctx_len, ctx_len), lambda b: (0, 0)),
                  pl.BlockSpec((ctx_len, c), lambda b: (blk0 + b, P_FOUR // c)),
                  pl.BlockSpec((2 * FOURIER_GROUP_DIM, FOURIER_GROUP_DIM), lambda b: (0, 0))],
        out_specs=pl.BlockSpec((ctx_len, c), lambda b: (b, 0)),
        compiler_params=_cparams(1),
        name="dft_ctx",
    )(wc, p_all, cs)


def _merge_kernel(gate_ref, z_ref, xs_ref, yf_ref, yb_ref, flat_ref, fctx_ref, x_ref,
                  g1_ref, sc2_ref, sh2_ref, d_ref, nw_ref, lng_ref, lnb_ref,
                  wssm_ref, wfour_ref, wout_ref, xo_ref, u2_ref, u2b_ref,
                  *, tiles_per_seq, n_batch, alpha):
    bidx = jnp.minimum(pl.program_id(0) // tiles_per_seq, n_batch)
    f_tile = jnp.where(bidx >= n_batch, fctx_ref[...], flat_ref[...])
    y = (yf_ref[...].astype(jnp.float32) + yb_ref[...].astype(jnp.float32)
         + xs_ref[...].astype(jnp.float32) * d_ref[...])
    h = y * _silu(z_ref[...].astype(jnp.float32))
    h = h * lax.rsqrt(jnp.mean(h * h, axis=-1, keepdims=True) + LN_EPS) * nw_ref[...]
    ssm = jnp.dot(h.astype(jnp.bfloat16), wssm_ref[...], preferred_element_type=jnp.float32)
    four = jnp.dot(f_tile, wfour_ref[...], preferred_element_type=jnp.float32)
    gates = _sigmoid(gate_ref[...].astype(jnp.float32))
    merged = gates[:, :D_MODEL] * ssm + gates[:, D_MODEL:] * four
    out = jnp.dot(merged.astype(jnp.bfloat16), wout_ref[...], preferred_element_type=jnp.float32)
    r = alpha * x_ref[...] + g1_ref[pl.ds(bidx, 1), :] * out
    xn = _layer_norm_f32(r) * lng_ref[...] + lnb_ref[...]
    xo_ref[...] = xn
    u2 = (_layer_norm_f32(xn) * (1.0 + sc2_ref[pl.ds(bidx, 1), :])
          + sh2_ref[pl.ds(bidx, 1), :])
    u2_ref[...] = u2
    u2b_ref[...] = u2.astype(jnp.bfloat16)


def _merge(p_all, xbc, yf, yb, f_lat, f_ctx, x_all, mod, d_cols, norm_w, ln_g, ln_b,
           w_ssm, w_four, w_out, n_rows, seq_len, n_batch, alpha):
    tm = ROW_TILE
    d = D_MODEL
    row = lambda i: (i, 0)
    const = lambda i: (0, 0)
    n_lat_tiles = f_lat.shape[0] // tm
    out_sds = jax.ShapeDtypeStruct((n_rows, d), jnp.float32)
    return pl.pallas_call(
        functools.partial(_merge_kernel, tiles_per_seq=seq_len // tm, n_batch=n_batch, alpha=alpha),
        out_shape=(out_sds, out_sds, jax.ShapeDtypeStruct((n_rows, d), jnp.bfloat16)),
        grid=(n_rows // tm,),
        in_specs=[
            pl.BlockSpec((tm, 2 * d), lambda i: (i, P_GATE // (2 * d))),
            pl.BlockSpec((tm, D_INNER), lambda i: (i, P_Z // D_INNER)),
            pl.BlockSpec((tm, D_INNER), row),
            pl.BlockSpec((tm, D_INNER), row),
            pl.BlockSpec((tm, D_INNER), row),
            pl.BlockSpec((tm, FOURIER_W), lambda i: (jnp.minimum(i, n_lat_tiles - 1), 0)),
            pl.BlockSpec((tm, FOURIER_W), lambda i: (jnp.maximum(i - n_lat_tiles, 0), 0)),
            pl.BlockSpec((tm, d), row),
            pl.BlockSpec((8, d), lambda i: (0, 2)),
            pl.BlockSpec((8, d), lambda i: (0, 4)),
            pl.BlockSpec((8, d), lambda i: (0, 3)),
            pl.BlockSpec((1, D_INNER), const),
            pl.BlockSpec((1, D_INNER), const),
            pl.BlockSpec((1, d), const),
            pl.BlockSpec((1, d), const),
            pl.BlockSpec((D_INNER, d), const),
            pl.BlockSpec((FOURIER_W, d), const),
            pl.BlockSpec((d, d), const),
        ],
        out_specs=(pl.BlockSpec((tm, d), row), pl.BlockSpec((tm, d), row),
                   pl.BlockSpec((tm, d), row)),
        compiler_params=_cparams(1),
        name="merge_postnorm",
    )(p_all, p_all, xbc, yf, yb, f_lat, f_ctx, x_all, mod, mod, mod, d_cols, norm_w,
      ln_g, ln_b, w_ssm, w_four, w_out)


def _first_index_of_max(v, iota, big):
    m = jnp.max(v, axis=0, keepdims=True)
    idx = jnp.min(jnp.where(v == m, iota, big), axis=0, keepdims=True)
    return m, idx


def _router_kernel(u_ref, wt_ref, bias_ref, tri_ref, row_ref, rowt_ref, gatet_ref, cnt_ref,
                   off_ref, basetab_ref, tot_ref, base_ref):
    @pl.when(pl.program_id(0) == 0)
    def _():
        base_ref[...] = jnp.zeros_like(base_ref)

    tm = u_ref.shape[0]
    ne, epg = N_EXPERTS, EXPERTS_PER_GROUP
    neg = -jnp.inf
    logits = lax.dot_general(wt_ref[...], u_ref[...], (((1,), (1,)), ((), ())),
                             preferred_element_type=jnp.float32,
                             precision=lax.Precision.HIGHEST)
    scores = _sigmoid(logits)
    sel = scores + bias_ref[...]
    iota_g = lax.broadcasted_iota(jnp.int32, (epg, tm), 0).astype(jnp.float32)
    grp_rows = []
    for g in range(N_EXPERT_GROUPS):
        v = sel[epg * g:epg * (g + 1), :]
        m1, i1 = _first_index_of_max(v, iota_g, epg)
        m2 = jnp.max(jnp.where(iota_g == i1, neg, v), axis=0, keepdims=True)
        grp_rows.append(m1 + m2)
    grp = jnp.concatenate(grp_rows, axis=0)
    iota_n = lax.broadcasted_iota(jnp.int32, (N_EXPERT_GROUPS, tm), 0).astype(jnp.float32)
    chosen = jnp.zeros((N_EXPERT_GROUPS, tm), jnp.float32)
    for _ in range(TOPK_GROUPS):
        _, gi = _first_index_of_max(grp, iota_n, N_EXPERT_GROUPS)
        hit = iota_n == gi
        chosen = jnp.where(hit, 1.0, chosen)
        grp = jnp.where(hit, neg, grp)
    masked = jnp.concatenate(
        [jnp.where(chosen[g:g + 1, :] > 0.0, sel[epg * g:epg * (g + 1), :], neg)
         for g in range(N_EXPERT_GROUPS)], axis=0)
    iota_e = lax.broadcasted_iota(jnp.int32, (ne, tm), 0).astype(jnp.float32)
    picked = jnp.zeros((ne, tm), jnp.float32)
    for _ in range(TOP_K):
        _, ei = _first_index_of_max(masked, iota_e, ne)
        hit = iota_e == ei
        picked = jnp.where(hit, 1.0, picked)
        masked = jnp.where(hit, neg, masked)
    chosen_scores = picked * scores
    gate = chosen_scores / jnp.sum(chosen_scores, axis=0, keepdims=True) * ROUTE_SCALE
    before = jnp.dot(picked.astype(jnp.bfloat16), tri_ref[...],
                     preferred_element_type=jnp.float32)
    cnt = jnp.sum(picked, axis=1, keepdims=True)
    cnt16 = jnp.ceil(cnt * (1.0 / ROW_ALIGN)) * ROW_ALIGN
    cnt16_b = jnp.broadcast_to(cnt16, (ne, LANES))
    e_row = lax.broadcasted_iota(jnp.int32, (ne, ne), 0)
    e_col = lax.broadcasted_iota(jnp.int32, (ne, ne), 1)
    off_b = jnp.dot((e_col < e_row).astype(jnp.float32), cnt16_b,
                    preferred_element_type=jnp.float32,
                    precision=lax.Precision.HIGHEST)
    row1 = picked * (before + off_b[:, 0:1] + 1.0)
    pad = jnp.zeros((EXPERT_PAD - ne, tm), jnp.float32)
    row1_p = jnp.concatenate([row1, pad], axis=0)
    gate_p = jnp.concatenate([gate, pad], axis=0)
    pad_b = jnp.zeros((EXPERT_PAD - ne, LANES), jnp.float32)
    row_ref[...] = row1_p
    rowt_ref[...] = row1_p.T
    gatet_ref[...] = gate_p.T.astype(jnp.bfloat16)
    cnt_ref[...] = jnp.concatenate([cnt16_b, pad_b], axis=0)
    off_ref[...] = jnp.concatenate([off_b, pad_b], axis=0)
    basetab_ref[...] = jnp.broadcast_to(base_ref[...], (ne, LANES))
    base_ref[...] = base_ref[...] + cnt16
    tot_ref[...] = base_ref[...]


def _router(u2, router_w, router_bias, n_tok):
    tm = MOE_TILE
    n_tiles = n_tok // tm
    tri = jnp.triu(jnp.ones((tm, tm), jnp.float32), k=1).astype(jnp.bfloat16)
    tab = lambda rows: (jax.ShapeDtypeStruct((n_tiles, rows, LANES), jnp.float32),
                        pl.BlockSpec((None, rows, LANES), lambda i: (i, 0, 0)))
    outs = [
        (jax.ShapeDtypeStruct((n_tiles, EXPERT_PAD, tm), jnp.float32),
         pl.BlockSpec((None, EXPERT_PAD, tm), lambda i: (i, 0, 0))),
        (jax.ShapeDtypeStruct((n_tok, EXPERT_PAD), jnp.float32),
         pl.BlockSpec((tm, EXPERT_PAD), lambda i: (i, 0))),
        (jax.ShapeDtypeStruct((n_tok, EXPERT_PAD), jnp.bfloat16),
         pl.BlockSpec((tm, EXPERT_PAD), lambda i: (i, 0))),
        tab(EXPERT_PAD), tab(EXPERT_PAD), tab(N_EXPERTS),
        (jax.ShapeDtypeStruct((N_EXPERTS, 1), jnp.float32),
         pl.BlockSpec((N_EXPERTS, 1), lambda i: (0, 0))),
    ]
    return pl.pallas_call(
        _router_kernel,
        out_shape=tuple(o[0] for o in outs),
        grid=(n_tiles,),
        in_specs=[pl.BlockSpec((tm, D_MODEL), lambda i: (i, 0)),
                  pl.BlockSpec((N_EXPERTS, D_MODEL), lambda i: (0, 0)),
                  pl.BlockSpec((N_EXPERTS, 1), lambda i: (0, 0)),
                  pl.BlockSpec((tm, tm), lambda i: (0, 0))],
        out_specs=tuple(o[1] for o in outs),
        scratch_shapes=[pltpu.VMEM((N_EXPERTS, 1), jnp.float32)],
        compiler_params=_cparams(1),
        name="moe_router",
    )(u2, router_w.T, router_bias.reshape(N_EXPERTS, 1), tri)


def _run_copies(cnt, make_copy, pieces):
    done = jnp.int32(0)
    for piece in pieces:
        hit = (cnt & piece) != 0

        @pl.when(hit)
        def _(done=done, piece=piece):
            make_copy(done, piece)

        done = done + (cnt & piece)


def _piece_lists(cnt, off, dst):
    counts, srcs, dsts = [], [], []
    slot_ids = jnp.arange(N_EXPERTS, dtype=jnp.int32)
    for piece in RUN_PIECES:
        has = ((cnt & piece) != 0).astype(jnp.int32)
        above = cnt & ~(2 * piece - 1)
        rank = jnp.cumsum(has, axis=-1) - has
        place = (rank[:, None, :] == slot_ids[None, :, None]).astype(jnp.int32) * has[:, None, :]
        counts.append(jnp.sum(has, axis=-1))
        srcs.append(jnp.sum(place * (off + above)[:, None, :], axis=-1))
        dsts.append(jnp.sum(place * (dst + above)[:, None, :], axis=-1))
    flat = lambda parts: jnp.stack(parts, axis=1).reshape(-1).astype(jnp.int32)
    return flat(counts), flat(srcs), flat(dsts)


def _for_each_piece(tile, pn_ref, psrc_ref, pdst_ref, make_copy, start):
    for p_idx, piece in enumerate(RUN_PIECES):
        seg = tile * len(RUN_PIECES) + p_idx
        base = seg * N_EXPERTS

        def body(k, carry, piece=piece, base=base):
            cp = make_copy(pl.multiple_of(psrc_ref[base + k], ROW_ALIGN),
                           pl.multiple_of(pdst_ref[base + k], ROW_ALIGN), piece)
            if start:
                cp.start()
            else:
                cp.wait()
            return carry

        lax.fori_loop(0, pn_ref[seg], body, 0)


def _split_rows(row1):
    hi = jnp.floor(row1 * (1.0 / ROW_SPLIT))
    lo = row1 - hi * ROW_SPLIT
    return hi.astype(jnp.bfloat16), lo.astype(jnp.bfloat16)


def _dispatch_kernel(pn_ref, psrc_ref, pdst_ref, tail_ref, nblk_ref, used_ref, u_ref, row_ref,
                     bounds_ref, xs_hbm, sorted_ref, zero_ref, sems):
    i = pl.program_id(0)
    last = pl.num_programs(0) - 1
    slot = i % 2
    tm = u_ref.shape[0]
    n_chunks = sorted_ref.shape[1] // tm
    digits = jnp.concatenate(_split_rows(row_ref[...]), axis=0)
    lower = bounds_ref[0:1, :]
    upper = bounds_ref[1:2, :]
    row_e = lax.broadcasted_iota(jnp.int32, (tm, EXPERT_PAD), 0).astype(jnp.float32)
    row_t = lax.broadcasted_iota(jnp.int32, (tm, tm), 0).astype(jnp.float32)

    def sort_chunk(j):
        r0 = float(j * tm)
        in_run = jnp.logical_and(row_e + r0 >= lower, row_e + r0 < upper)
        owner = jnp.concatenate([jnp.where(in_run, ROW_SPLIT, 0.0),
                                 jnp.where(in_run, 1.0, 0.0)], axis=1).astype(jnp.bfloat16)
        want = jnp.dot(owner, digits, preferred_element_type=jnp.float32)
        perm = jnp.where(want == row_t + (r0 + 1.0), 1.0, 0.0).astype(jnp.bfloat16)
        sorted_ref[slot, j * tm:(j + 1) * tm, :] = jnp.dot(
            perm, u_ref[...], preferred_element_type=jnp.float32).astype(jnp.bfloat16)

    for j in range(n_chunks):
        if (j + 1) * tm <= TOP_K * tm:
            sort_chunk(j)
        else:
            pl.when(used_ref[i] > j * tm)(functools.partial(sort_chunk, j))

    def run_copy(buf):
        def make(src, dst, piece):
            return pltpu.make_async_copy(sorted_ref.at[buf, pl.ds(src, piece), :],
                                         xs_hbm.at[pl.ds(dst, piece), :], sems.at[buf])
        return make

    _for_each_piece(i, pn_ref, psrc_ref, pdst_ref, run_copy(slot), start=True)

    @pl.when(i > 0)
    def _():
        _for_each_piece(i - 1, pn_ref, psrc_ref, pdst_ref, run_copy(1 - slot), start=False)

    @pl.when(i == last)
    def _():
        _for_each_piece(i, pn_ref, psrc_ref, pdst_ref, run_copy(slot), start=False)
        zero_ref[...] = jnp.zeros_like(zero_ref)
        blk = zero_ref.shape[0]
        n_blk_total = xs_hbm.shape[0] // blk

        def zero_copy(dst, piece):
            return pltpu.make_async_copy(zero_ref.at[pl.ds(0, piece), :],
                                         xs_hbm.at[pl.ds(dst, piece), :], sems.at[0])

        def tail_body(start):
            def body(e, carry):
                dst = pl.multiple_of(tail_ref[e], ROW_ALIGN)

                def piece_copy(done, piece):
                    cp = zero_copy(pl.multiple_of(dst + done, ROW_ALIGN), piece)
                    if start:
                        cp.start()
                    else:
                        cp.wait()

                _run_copies(tail_ref[N_EXPERTS + e], piece_copy, TAIL_PIECES)
                return carry
            return body

        def blk_body(start):
            def body(b, carry):
                cp = zero_copy(pl.multiple_of(b * blk, blk), blk)
                if start:
                    cp.start()
                else:
                    cp.wait()
                return carry
            return body

        lax.fori_loop(0, N_EXPERTS, tail_body(True), 0)
        lax.fori_loop(nblk_ref[0], n_blk_total, blk_body(True), 0)
        lax.fori_loop(0, N_EXPERTS, tail_body(False), 0)
        lax.fori_loop(nblk_ref[0], n_blk_total, blk_body(False), 0)


def _dispatch(u2b, row1, bounds, pieces, tail, n_used_blk, used, n_blk):
    tm = MOE_TILE
    d = D_MODEL
    n_tok = u2b.shape[0]
    grid_spec = pltpu.PrefetchScalarGridSpec(
        num_scalar_prefetch=6,
        grid=(n_tok // tm,),
        in_specs=[pl.BlockSpec((tm, d), lambda i, *_: (i, 0)),
                  pl.BlockSpec((None, EXPERT_PAD, tm), lambda i, *_: (i, 0, 0)),
                  pl.BlockSpec((None, 2, EXPERT_PAD), lambda i, *_: (i, 0, 0))],
        out_specs=pl.BlockSpec(memory_space=pl.ANY),
        scratch_shapes=[pltpu.VMEM((2, SORTED_ROWS, d), jnp.bfloat16),
                        pltpu.VMEM((MOE_BLK, d), jnp.bfloat16),
                        pltpu.SemaphoreType.DMA((2,))],
    )
    return pl.pallas_call(
        _dispatch_kernel,
        out_shape=jax.ShapeDtypeStruct((n_blk * MOE_BLK, d), jnp.bfloat16),
        grid_spec=grid_spec,
        compiler_params=_cparams(1),
        name="moe_dispatch",
    )(*pieces, tail, n_used_blk, used, u2b, row1, bounds)


def _expert_kernel(blk_e_ref, blk_valid_ref, x_ref, w1_ref, w3_ref, w2_ref, y_ref,
                   w13_ref, w2b_ref):
    i = pl.program_id(0)

    @pl.when(blk_valid_ref[i] > 0)
    def _():
        @pl.when(jnp.logical_or(i == 0, blk_e_ref[i] != blk_e_ref[jnp.maximum(i - 1, 0)]))
        def _():
            w13_ref[:, :EXPERT_FF] = w1_ref[...].astype(jnp.bfloat16)
            w13_ref[:, EXPERT_FF:] = w3_ref[...].astype(jnp.bfloat16)
            w2b_ref[...] = w2_ref[...].astype(jnp.bfloat16)

        h = jnp.dot(x_ref[...], w13_ref[...], preferred_element_type=jnp.float32)
        hb = (_silu(h[:, :EXPERT_FF]) * h[:, EXPERT_FF:]).astype(jnp.bfloat16)
        y_ref[...] = jnp.dot(hb, w2b_ref[...],
                             preferred_element_type=jnp.float32).astype(jnp.bfloat16)

    @pl.when(blk_valid_ref[i] == 0)
    def _():
        y_ref[...] = jnp.zeros_like(y_ref)


def _experts(xs, blk_e, blk_valid, w1, w3, w2, layer, n_blk):
    blk = MOE_BLK
    d = D_MODEL
    grid_spec = pltpu.PrefetchScalarGridSpec(
        num_scalar_prefetch=2,
        grid=(n_blk,),
        in_specs=[
            pl.BlockSpec((blk, d), lambda i, be, bv: (i, 0)),
            pl.BlockSpec((None, None, d, EXPERT_FF), lambda i, be, bv: (layer, be[i], 0, 0)),
            pl.BlockSpec((None, None, d, EXPERT_FF), lambda i, be, bv: (layer, be[i], 0, 0)),
            pl.BlockSpec((None, None, EXPERT_FF, d), lambda i, be, bv: (layer, be[i], 0, 0)),
        ],
        out_specs=pl.BlockSpec((blk, d), lambda i, be, bv: (i, 0)),
        scratch_shapes=[pltpu.VMEM((d, 2 * EXPERT_FF), jnp.bfloat16),
                        pltpu.VMEM((EXPERT_FF, d), jnp.bfloat16)],
    )
    return pl.pallas_call(
        _expert_kernel,
        out_shape=jax.ShapeDtypeStruct((n_blk * blk, d), jnp.bfloat16),
        grid_spec=grid_spec,
        compiler_params=_cparams(1),
        name="moe_experts",
    )(blk_e, blk_valid, xs, w1, w3, w2)


def _final_kernel(pn_ref, psrc_ref, pdst_ref, used_ref, rowt_ref, gatet_ref, cnt_tab_ref,
                  off_tab_ref, y_hbm, x_ref, u_ref, g2_ref, lng_ref, lnb_ref, ws1_ref, ws3_ref,
                  ws2_ref, o_ref, sorted_ref, acc_ref, sems, *, tiles_per_seq, n_batch, alpha):
    i = pl.program_id(0)
    last = pl.num_programs(0) - 1
    slot = i % 2
    bidx = jnp.minimum(i // tiles_per_seq, n_batch)
    tm = x_ref.shape[0]
    n_chunks = sorted_ref.shape[1] // tm

    def run_copy(buf):
        def make(src, dst, piece):
            return pltpu.make_async_copy(y_hbm.at[pl.ds(dst, piece), :],
                                         sorted_ref.at[buf, pl.ds(src, piece), :], sems.at[buf])
        return make

    @pl.when(i == 0)
    def _():
        sorted_ref[...] = jnp.zeros_like(sorted_ref)
        _for_each_piece(i, pn_ref, psrc_ref, pdst_ref, run_copy(slot), start=True)

    @pl.when(i < last)
    def _():
        _for_each_piece(i + 1, pn_ref, psrc_ref, pdst_ref, run_copy(1 - slot), start=True)

    _for_each_piece(i, pn_ref, psrc_ref, pdst_ref, run_copy(slot), start=False)

    digits = jnp.concatenate(_split_rows(rowt_ref[...]), axis=1)
    gate = gatet_ref[...]
    lower = off_tab_ref[:, 0:1]
    upper = lower + cnt_tab_ref[:, 0:1]
    lane_e = lax.broadcasted_iota(jnp.int32, (EXPERT_PAD, tm), 1).astype(jnp.float32)
    lane_t = lax.broadcasted_iota(jnp.int32, (tm, tm), 1).astype(jnp.float32)

    def chunk_sum(j):
        r0 = float(j * tm)
        in_run = jnp.logical_and(lane_e + r0 >= lower, lane_e + r0 < upper)
        owner = jnp.where(in_run, 1.0, 0.0).astype(jnp.bfloat16)
        owner_digits = jnp.concatenate([jnp.where(in_run, ROW_SPLIT, 0.0).astype(jnp.bfloat16),
                                        owner], axis=0)
        want = jnp.dot(digits, owner_digits, preferred_element_type=jnp.float32)
        weight = jnp.dot(gate, owner, preferred_element_type=jnp.float32)
        comb = jnp.where(want == lane_t + (r0 + 1.0), weight, 0.0).astype(jnp.bfloat16)
        return jnp.dot(comb, sorted_ref[slot, j * tm:(j + 1) * tm, :],
                       preferred_element_type=jnp.float32)

    always = TOP_K
    routed = chunk_sum(0)
    for j in range(1, always):
        routed = routed + chunk_sum(j)
    acc_ref[...] = routed
    for j in range(always, n_chunks):
        @pl.when(used_ref[i] > j * tm)
        def _(j=j):
            acc_ref[...] += chunk_sum(j)
    routed = acc_ref[...]
    u = u_ref[...]
    h1 = jnp.dot(u, ws1_ref[...], preferred_element_type=jnp.float32)
    h3 = jnp.dot(u, ws3_ref[...], preferred_element_type=jnp.float32)
    shared = jnp.dot((_silu(h1) * h3).astype(jnp.bfloat16), ws2_ref[...],
                     preferred_element_type=jnp.float32)
    r = alpha * x_ref[...] + g2_ref[pl.ds(bidx, 1), :] * (routed + shared)
    o_ref[...] = _layer_norm_f32(r) * lng_ref[...] + lnb_ref[...]


def _final(pieces, used, row1_t, gate_t, cnt_tab, off_tab, y_slots, x_mid, u2b, mod,
           ln_g, ln_b, ws1, ws3, ws2, n_rows, seq_len, n_batch, alpha):
    tm = MOE_TILE
    d = D_MODEL
    row = lambda i, *_: (i, 0)
    const = lambda i, *_: (0, 0)
    tab_spec = pl.BlockSpec((None, EXPERT_PAD, LANES), lambda i, *_: (i, 0, 0))
    grid_spec = pltpu.PrefetchScalarGridSpec(
        num_scalar_prefetch=4,
        grid=(n_rows // tm,),
        in_specs=[
            pl.BlockSpec((tm, EXPERT_PAD), row),
            pl.BlockSpec((tm, EXPERT_PAD), row),
            tab_spec,
            tab_spec,
            pl.BlockSpec(memory_space=pl.ANY),
            pl.BlockSpec((tm, d), row),
            pl.BlockSpec((tm, d), row),
            pl.BlockSpec((8, d), lambda i, *_: (0, 5)),
            pl.BlockSpec((1, d), const),
            pl.BlockSpec((1, d), const),
            pl.BlockSpec((d, SHARED_FF), const),
            pl.BlockSpec((d, SHARED_FF), const),
            pl.BlockSpec((SHARED_FF, d), const),
        ],
        out_specs=pl.BlockSpec((tm, d), row),
        scratch_shapes=[pltpu.VMEM((2, SORTED_ROWS, d), jnp.bfloat16),
                        pltpu.VMEM((tm, d), jnp.float32),
                        pltpu.SemaphoreType.DMA((2,))],
    )
    return pl.pallas_call(
        functools.partial(_final_kernel, tiles_per_seq=seq_len // tm, n_batch=n_batch, alpha=alpha),
        out_shape=jax.ShapeDtypeStruct((n_rows, d), jnp.float32),
        grid_spec=grid_spec,
        compiler_params=_cparams(1),
        name="moe_combine_postnorm",
    )(*pieces, used, row1_t, gate_t, cnt_tab, off_tab, y_slots, x_mid, u2b, mod,
      ln_g, ln_b, ws1, ws3, ws2)


def _moe_sublayer(x_mid, u2, u2b, mod, ln_g, ln_b, router_w, router_bias, w1, w3, w2, layer,
                  ws1, ws3, ws2, n_rows, seq_len, n_batch, alpha):
    bf = jnp.bfloat16
    blk = MOE_BLK
    n_tiles = n_rows // MOE_TILE
    row1, row1_t, gate_t, cnt_tab, off_tab, base_tab, total = _router(
        u2, router_w, router_bias, n_rows)
    as_int = lambda t: t[:, :N_EXPERTS, 0].astype(jnp.int32)
    cnt, off, base = as_int(cnt_tab), as_int(off_tab), as_int(base_tab)
    used = off[:, -1] + cnt[:, -1]
    bounds = jnp.stack([off_tab[:, :, 0], off_tab[:, :, 0] + cnt_tab[:, :, 0]], axis=1)
    total = total.reshape(N_EXPERTS).astype(jnp.int32)
    padded = (total + blk - 1) // blk * blk
    pends = jnp.cumsum(padded)
    pstart = pends - padded
    dst = pstart[None, :] + base
    tail = jnp.concatenate([pstart + total, padded - total])
    n_used_blk = (pends[-1:] // blk).astype(jnp.int32)
    max_rows = n_rows * TOP_K + n_tiles * N_EXPERTS * (ROW_ALIGN - 1) + N_EXPERTS * (blk - 1)
    n_blk = -(-max_rows // blk)
    blk_start = jnp.arange(n_blk, dtype=jnp.int32) * blk
    blk_e = jnp.minimum(jnp.sum((blk_start[:, None] >= pends[None, :]).astype(jnp.int32), axis=1),
                        N_EXPERTS - 1)
    blk_valid = (blk_start < pends[-1]).astype(jnp.int32)
    pieces = _piece_lists(cnt, off, dst)
    xs = _dispatch(u2b, row1, bounds, pieces, tail, n_used_blk, used, n_blk)
    y_slots = _experts(xs, blk_e, blk_valid, w1, w3, w2, layer, n_blk)
    return _final(pieces, used, row1_t, gate_t, cnt_tab, off_tab, y_slots, x_mid, u2b, mod,
                  ln_g, ln_b, ws1.astype(bf), ws3.astype(bf), ws2.astype(bf),
                  n_rows, seq_len, n_batch, alpha)


def _pack_in_proj(w_in):
    d = w_in.shape[0]
    gap = jnp.zeros((d, DT_BWD_LANE - SSM_HEADS), w_in.dtype)
    rest = jnp.zeros((d, P_FOUR - P_DT - DT_BWD_LANE - SSM_HEADS), w_in.dtype)
    return jnp.concatenate([
        w_in[:, OFF_XBC:OFF_DT],
        w_in[:, OFF_DT:OFF_DT + SSM_HEADS], gap, w_in[:, OFF_DT + SSM_HEADS:OFF_FOUR], rest,
        w_in[:, OFF_FOUR:OFF_GATE], w_in[:, OFF_GATE:], w_in[:, :OFF_XBC]],
        axis=1).astype(jnp.bfloat16)


def _pack_head_rows(v):
    out = jnp.zeros((1, LANES), jnp.float32)
    v = v.reshape(2, SSM_HEADS).astype(jnp.float32)
    out = out.at[0, 0:SSM_HEADS].set(v[0])
    return out.at[0, DT_BWD_LANE:DT_BWD_LANE + SSM_HEADS].set(v[1])


def kernel(x, c, ctx, c_ctx, w_ada, b_ada, w_in, conv_w, conv_b, dt_bias, a_log, d_skip,
           ssm_norm_w, w_br_ssm, w_br_four, w_out, ln1_g, ln1_b, ln2_g, ln2_b,
           router_w, router_bias, w1, w3, w2, ws1, ws3, ws2):
    n_batch, seq_len, d = x.shape
    ctx_len = ctx.shape[1]
    depth = w_ada.shape[0]
    bf = jnp.bfloat16
    alpha = float((2 * depth) ** 0.25)
    n_lat = n_batch * seq_len
    assert d == D_MODEL and n_batch + 1 <= 8
    assert seq_len % K1_TM == 0 and (n_batch * ctx_len) % K1_TM == 0
    assert seq_len % (LANES * 8) == 0

    x_all = _assemble_stream(x, ctx)
    c_rows = jnp.zeros((8, d), jnp.float32).at[:n_batch].set(c).at[n_batch].set(c_ctx)
    mod_all = _ada_mod(c_rows, w_ada, b_ada)
    tables = _dft_tables(seq_len)

    for i in range(depth):
        last = i == depth - 1
        mod = mod_all[i]
        p_all, four_all = _in_proj(x_all, mod, _pack_in_proj(w_in[i]), seq_len, n_batch)
        a_row = _pack_head_rows(-jnp.exp(a_log[i].astype(jnp.float32)))
        xbc, dt_all, cum_all, cumt_all, bt_all = _conv(
            p_all, conv_w[i], conv_b[i], _pack_head_rows(dt_bias[i]), a_row,
            seq_len, n_lat, ctx_len)
        yf, yb = _ssd(xbc, dt_all, cum_all, cumt_all, bt_all, n_batch, seq_len, ctx_len)
        f_lat = _fourier_latent(four_all, n_batch, seq_len, tables)
        n_rows = n_lat if last else x_all.shape[0]
        f_ctx = f_lat if last else _fourier_ctx(p_all, n_batch, seq_len, ctx_len, tables)
        d_cols = jnp.repeat(d_skip[i].astype(jnp.float32), SSM_HEAD_DIM).reshape(1, D_INNER)
        x_mid, u2, u2b = _merge(p_all, xbc, yf, yb, f_lat, f_ctx, x_all, mod, d_cols,
                           ssm_norm_w[i].reshape(1, D_INNER), ln1_g[i].reshape(1, d),
                           ln1_b[i].reshape(1, d), w_br_ssm[i].astype(bf),
                           w_br_four[i].astype(bf), w_out[i].astype(bf),
                           n_rows, seq_len, n_batch, alpha)
        x_all = _moe_sublayer(x_mid, u2, u2b, mod, ln2_g[i].reshape(1, d), ln2_b[i].reshape(1, d),
                              router_w[i], router_bias[i], w1, w3, w2, i,
                              ws1[i], ws3[i], ws2[i], n_rows, seq_len, n_batch, alpha)
    return x_all[:n_lat].reshape(n_batch, seq_len, d)
```

```python
import functools

import jax
import jax.numpy as jnp
import numpy as np
from jax import lax
from jax.experimental import pallas as pl
from jax.experimental.pallas import tpu as pltpu

D_MODEL = 1024
GRID_W = 64
POS_BASE = 10000.0
LN_EPS = 1e-6

SSM_HEADS = 24
SSM_HEAD_DIM = 64
D_INNER = SSM_HEADS * SSM_HEAD_DIM
SSM_GROUPS = 4
HEADS_PER_GROUP = SSM_HEADS // SSM_GROUPS
SSM_STATE = 128
CONV_W = 5
CONV_CH = D_INNER + 2 * SSM_GROUPS * SSM_STATE
CHUNK = 128

FOURIER_GROUPS = 4
FOURIER_GROUP_DIM = 256
FOURIER_W = FOURIER_GROUPS * FOURIER_GROUP_DIM

OFF_XBC = D_INNER
OFF_DT = OFF_XBC + CONV_CH
OFF_FOUR = OFF_DT + 2 * SSM_HEADS
OFF_GATE = OFF_FOUR + FOURIER_W

N_EXPERTS = 64
TOP_K = 8
N_EXPERT_GROUPS = 8
EXPERTS_PER_GROUP = N_EXPERTS // N_EXPERT_GROUPS
TOPK_GROUPS = 4
EXPERT_FF = 256
SHARED_FF = 256
ROUTE_SCALE = 2.5

LANES = 128
VMEM_LIMIT_BYTES = 56 * 1024 * 1024

P_XBC = 0
P_DT = CONV_CH
P_FOUR = P_DT + SSM_GROUPS * LANES
P_GATE = P_FOUR + FOURIER_W
P_Z = P_GATE + 2 * D_MODEL
P_WIDTH = P_Z + D_INNER
DT_BWD_LANE = 32
CUMT_ROWS = 64

DFT2_ROWS_PER_STEP = 4
K1_TM = 512
K1_TN = 1536
ROW_TILE = 256
MOE_TILE = 512
MOE_BLK = 1024
ROW_ALIGN = 16
EXPERT_PAD = LANES
ROW_SPLIT = 64.0
RUN_PIECES = tuple(MOE_TILE >> s for s in range(6))
TAIL_PIECES = tuple(p for p in RUN_PIECES if p < MOE_BLK)
SORTED_ROWS = -(-(MOE_TILE * TOP_K + N_EXPERTS * (ROW_ALIGN - 1)) // MOE_TILE) * MOE_TILE


def _cparams(n_axes=1):
    return pltpu.CompilerParams(
        dimension_semantics=("arbitrary",) * n_axes,
        vmem_limit_bytes=VMEM_LIMIT_BYTES)


def _layer_norm_f32(x):
    mu = jnp.mean(x, axis=-1, keepdims=True)
    xc = x - mu
    var = jnp.mean(xc * xc, axis=-1, keepdims=True)
    return xc * lax.rsqrt(var + LN_EPS)


def _sigmoid(x):
    return 1.0 / (1.0 + jnp.exp(-x))


def _silu(x):
    return x * _sigmoid(x)


def _pos_kernel(x_ref, ctx_ref, er_ref, ec_ref, o_ref, *, n_lat):
    i = pl.program_id(0)
    half = D_MODEL // 2

    @pl.when(i < n_lat)
    def _():
        ec = ec_ref[...]
        for r in range(8):
            rows = slice(GRID_W * r, GRID_W * (r + 1))
            o_ref[rows, :half] = x_ref[rows, :half] + er_ref[r:r + 1, :]
            o_ref[rows, half:] = x_ref[rows, half:] + ec

    @pl.when(i >= n_lat)
    def _():
        o_ref[...] = ctx_ref[...]


def _assemble_stream(x, ctx):
    b, l, d = x.shape
    lc = ctx.shape[1]
    tile = 8 * GRID_W
    n_lat = (b * l) // tile
    n_ctx = (b * lc) // tile
    rows = l // GRID_W
    quarter = D_MODEL // 4
    omega = 1.0 / (POS_BASE ** (jnp.arange(quarter, dtype=jnp.float32) / quarter))
    ang_r = jnp.arange(rows, dtype=jnp.float32)[:, None] * omega
    ang_c = jnp.arange(GRID_W, dtype=jnp.float32)[:, None] * omega
    emb_r = jnp.concatenate([jnp.sin(ang_r), jnp.cos(ang_r)], -1)
    emb_c = jnp.concatenate([jnp.sin(ang_c), jnp.cos(ang_c)], -1)
    tiles_per_seq = l // tile
    return pl.pallas_call(
        functools.partial(_pos_kernel, n_lat=n_lat),
        out_shape=jax.ShapeDtypeStruct((b * l + b * lc, d), jnp.float32),
        grid=(n_lat + n_ctx,),
        in_specs=[
            pl.BlockSpec((tile, d), lambda i: (jnp.minimum(i, n_lat - 1), 0)),
            pl.BlockSpec((tile, d), lambda i: (jnp.maximum(i - n_lat, 0), 0)),
            pl.BlockSpec((8, d // 2), lambda i: (i % tiles_per_seq, 0)),
            pl.BlockSpec((GRID_W, d // 2), lambda i: (0, 0)),
        ],
        out_specs=pl.BlockSpec((tile, d), lambda i: (i, 0)),
        compiler_params=_cparams(1),
        name="assemble_stream",
    )(x.reshape(b * l, d), ctx.reshape(b * lc, d), emb_r, emb_c)


def _ada_kernel(c_ref, w_ref, b_ref, o_ref):
    c = c_ref[...]
    o_ref[...] = jnp.dot(_silu(c), w_ref[...], preferred_element_type=jnp.float32,
                         precision=lax.Precision.HIGHEST) + b_ref[...]


def _ada_mod(c_rows, w_ada, b_ada):
    depth, d, n6 = w_ada.shape
    tn = 1536
    return pl.pallas_call(
        _ada_kernel,
        out_shape=jax.ShapeDtypeStruct((depth, 8, n6), jnp.float32),
        grid=(depth, n6 // tn),
        in_specs=[
            pl.BlockSpec((8, d), lambda a, j: (0, 0)),
            pl.BlockSpec((None, d, tn), lambda a, j: (a, 0, j)),
            pl.BlockSpec((None, 1, tn), lambda a, j: (a, 0, j)),
        ],
        out_specs=pl.BlockSpec((None, 8, tn), lambda a, j: (a, 0, j)),
        compiler_params=_cparams(2),
        name="ada_mod",
    )(c_rows, w_ada, b_ada.reshape(depth, 1, n6))


def _k1_kernel(x_ref, sc_ref, sh_ref, w_ref, o_ref, four_ref, u_ref, *, tiles_per_seq, n_batch):
    i = pl.program_id(0)
    bidx = jnp.minimum(i // tiles_per_seq, n_batch)
    xn = _layer_norm_f32(x_ref[...])
    u = xn * (1.0 + sc_ref[pl.ds(bidx, 1), :]) + sh_ref[pl.ds(bidx, 1), :]
    u_ref[...] = u.astype(jnp.bfloat16)
    for c in range(o_ref.shape[1] // K1_TN):
        cols = slice(c * K1_TN, (c + 1) * K1_TN)
        res = jnp.dot(u_ref[...], w_ref[:, cols],
                      preferred_element_type=jnp.float32).astype(jnp.bfloat16)
        o_ref[:, cols] = res
        if c * K1_TN == P_FOUR:
            @pl.when(i < n_batch * tiles_per_seq)
            def _(res=res):
                four_ref[...] = res[:, :FOURIER_W]


def _in_proj(x_all, mod, w_all, seq_len, n_batch):
    nt, d = x_all.shape
    tm = K1_TM
    assert P_FOUR % K1_TN == 0 and FOURIER_W <= K1_TN
    n_lat_tiles = n_batch * (seq_len // tm)
    return pl.pallas_call(
        functools.partial(_k1_kernel, tiles_per_seq=seq_len // tm, n_batch=n_batch),
        out_shape=(jax.ShapeDtypeStruct((nt, P_WIDTH), jnp.bfloat16),
                   jax.ShapeDtypeStruct((n_lat_tiles * tm, FOURIER_W), jnp.bfloat16)),
        grid=(nt // tm,),
        in_specs=[
            pl.BlockSpec((tm, d), lambda i: (i, 0)),
            pl.BlockSpec((8, d), lambda i: (0, 1)),
            pl.BlockSpec((8, d), lambda i: (0, 0)),
            pl.BlockSpec((d, P_WIDTH), lambda i: (0, 0), pipeline_mode=pl.Buffered(1)),
        ],
        out_specs=(pl.BlockSpec((tm, P_WIDTH), lambda i: (i, 0)),
                   pl.BlockSpec((tm, FOURIER_W),
                                lambda i: (jnp.minimum(i, n_lat_tiles - 1), 0))),
        scratch_shapes=[pltpu.VMEM((tm, d), jnp.bfloat16)],
        compiler_params=_cparams(1),
        name="in_proj",
    )(x_all, mod, mod, w_all)


def _softplus(x):
    return jnp.maximum(x, 0.0) + jnp.log1p(jnp.exp(-jnp.abs(x)))


def _conv_kernel(cur_ref, prev_ref, next_ref, w_ref, b_ref, dtraw_ref, bias_ref, a_ref,
                 o_ref, dt_ref, cum_ref, cumt_ref, bt_ref, ext_ref, *, tiles_per_seq, n_lat):
    i = pl.program_id(0)
    is_ctx = i >= n_lat
    is_start = jnp.logical_or(i % tiles_per_seq == 0, is_ctx)
    is_end = jnp.logical_or(i % tiles_per_seq == tiles_per_seq - 1, is_ctx)
    halo = prev_ref.shape[0]
    rows = cur_ref.shape[0]
    ext_ref[0:halo, :] = jnp.where(is_start, 0.0, prev_ref[...].astype(jnp.float32))
    ext_ref[halo:halo + rows, :] = cur_ref[...].astype(jnp.float32)
    ext_ref[halo + rows:, :] = jnp.where(is_end, 0.0, next_ref[...].astype(jnp.float32))

    def conv_lane_block(c, carry):
        lanes = pl.ds(pl.multiple_of(c * LANES, LANES), LANES)
        acc = b_ref[:, lanes] + w_ref[0:1, lanes] * ext_ref[pl.ds(halo - 2, rows), lanes]
        for k in range(1, CONV_W):
            acc = acc + w_ref[k:k + 1, lanes] * ext_ref[pl.ds(halo - 2 + k, rows), lanes]
        o_ref[:, lanes] = _silu(acc).astype(jnp.bfloat16)
        return carry

    lax.fori_loop(0, CONV_CH // LANES, conv_lane_block, 0)

    q = CHUNK
    dt = _softplus(dtraw_ref[...].astype(jnp.float32) + bias_ref[...])
    dt_ref[...] = dt
    adt = dt * a_ref[...]
    row_i = lax.broadcasted_iota(jnp.int32, (q, q), 0)
    col_i = lax.broadcasted_iota(jnp.int32, (q, q), 1)
    lower = (row_i >= col_i).astype(jnp.float32)
    upper = (row_i <= col_i).astype(jnp.float32)
    is_fwd = lax.broadcasted_iota(jnp.int32, (1, LANES), 1) < DT_BWD_LANE
    for ch in range(rows // q):
        rs = slice(q * ch, q * (ch + 1))
        cum_f = jnp.dot(lower, adt[rs, :], preferred_element_type=jnp.float32,
                        precision=lax.Precision.HIGHEST)
        cum_b = jnp.dot(upper, adt[rs, :], preferred_element_type=jnp.float32,
                        precision=lax.Precision.HIGHEST)
        cum = jnp.where(is_fwd, cum_f, cum_b)
        cum_ref[rs, :] = cum
        cumt_ref[ch] = cum.T[:CUMT_ROWS, :]
        for g in range(SSM_GROUPS):
            b_cols = slice(D_INNER + SSM_STATE * g, D_INNER + SSM_STATE * (g + 1))
            bt_ref[ch, SSM_STATE * g:SSM_STATE * (g + 1), :] = (
                o_ref[rs, b_cols].astype(jnp.float32).T.astype(jnp.bfloat16))


def _conv(p_all, conv_w, conv_b, dt_bias_row, a_row, seq_len, n_lat_rows, ctx_len):
    nt = p_all.shape[0]
    tl = ROW_TILE
    assert ctx_len == tl, "context sequences must span exactly one conv tile"
    halo = 16
    hb = tl // halo
    n_halo_blocks = nt // halo
    dtw = LANES
    cpt = tl // CHUNK
    w8 = jnp.zeros((8, CONV_CH), jnp.float32).at[:CONV_W].set(conv_w)
    return pl.pallas_call(
        functools.partial(_conv_kernel, tiles_per_seq=seq_len // tl, n_lat=n_lat_rows // tl),
        out_shape=(jax.ShapeDtypeStruct((nt, CONV_CH), jnp.bfloat16),
                   jax.ShapeDtypeStruct((nt, dtw), jnp.float32),
                   jax.ShapeDtypeStruct((nt, dtw), jnp.float32),
                   jax.ShapeDtypeStruct((nt // CHUNK, CUMT_ROWS, CHUNK), jnp.float32),
                   jax.ShapeDtypeStruct((nt // CHUNK, SSM_GROUPS * SSM_STATE, CHUNK), jnp.bfloat16)),
        grid=(nt // tl,),
        in_specs=[
            pl.BlockSpec((tl, CONV_CH), lambda i: (i, 0)),
            pl.BlockSpec((halo, CONV_CH), lambda i: (jnp.maximum(i * hb - 1, 0), 0)),
            pl.BlockSpec((halo, CONV_CH),
                         lambda i: (jnp.minimum((i + 1) * hb, n_halo_blocks - 1), 0)),
            pl.BlockSpec((8, CONV_CH), lambda i: (0, 0)),
            pl.BlockSpec((1, CONV_CH), lambda i: (0, 0)),
            pl.BlockSpec((tl, dtw), lambda i: (i, P_DT // dtw)),
            pl.BlockSpec((1, dtw), lambda i: (0, 0)),
            pl.BlockSpec((1, dtw), lambda i: (0, 0)),
        ],
        out_specs=(pl.BlockSpec((tl, CONV_CH), lambda i: (i, 0)),
                   pl.BlockSpec((tl, dtw), lambda i: (i, 0)),
                   pl.BlockSpec((tl, dtw), lambda i: (i, 0)),
                   pl.BlockSpec((cpt, CUMT_ROWS, CHUNK), lambda i: (i, 0, 0)),
                   pl.BlockSpec((cpt, SSM_GROUPS * SSM_STATE, CHUNK), lambda i: (i, 0, 0))),
        scratch_shapes=[pltpu.VMEM((tl + 2 * halo, CONV_CH), jnp.float32)],
        compiler_params=_cparams(1),
        name="dwconv_silu",
    )(p_all, p_all, p_all, w8, conv_b.reshape(1, CONV_CH), p_all, dt_bias_row, a_row)


def _ssd_direction(refs, s_ref, y_ref, gi, *, lane0, forward):
    x_ref, b_ref, c_ref, dt_ref, cum_ref, cumt_ref, bt_ref = refs
    q = CHUNK
    gw = HEADS_PER_GROUP * SSM_HEAD_DIM
    row_i = lax.broadcasted_iota(jnp.int32, (q, q), 0)
    col_i = lax.broadcasted_iota(jnp.int32, (q, q), 1)
    tri = (row_i >= col_i) if forward else (row_i <= col_i)
    lane_lo = lax.broadcasted_iota(jnp.int32, (q, LANES), 1) < SSM_HEAD_DIM
    lane_lo_row = lax.broadcasted_iota(jnp.int32, (1, LANES), 1) < SSM_HEAD_DIM

    g_lanes = slice(LANES * gi, LANES * (gi + 1))
    dt = dt_ref[...]
    cum = cum_ref[...]
    cum_t = cumt_ref[...]
    total = cum[q - 1:q, :] if forward else cum[0:1, :]

    cm = c_ref[:, g_lanes]
    bm = b_ref[:, g_lanes]
    scores = lax.dot_general(cm, bm, (((1,), (1,)), ((), ())),
                             preferred_element_type=jnp.float32)
    bm_t = bt_ref[SSM_STATE * gi:SSM_STATE * (gi + 1), :]

    for pr in range(HEADS_PER_GROUP // 2):
        r0 = lane0 + 2 * pr
        r1 = r0 + 1
        lanes = slice(gw * gi + LANES * pr, gw * gi + LANES * (pr + 1))
        xp = x_ref[:, lanes].astype(jnp.float32)
        dt_pair = jnp.where(lane_lo, dt[:, r0:r0 + 1], dt[:, r1:r1 + 1])
        cum_pair = jnp.where(lane_lo, cum[:, r0:r0 + 1], cum[:, r1:r1 + 1])
        tot_pair = jnp.where(lane_lo_row, total[:, r0:r0 + 1], total[:, r1:r1 + 1])
        xdt = xp * dt_pair
        l0 = jnp.exp(jnp.where(tri, cum[:, r0:r0 + 1] - cum_t[r0:r0 + 1, :], -jnp.inf))
        l1 = jnp.exp(jnp.where(tri, cum[:, r1:r1 + 1] - cum_t[r1:r1 + 1, :], -jnp.inf))
        w = jnp.concatenate([(scores * l0).astype(jnp.bfloat16),
                             (scores * l1).astype(jnp.bfloat16)], axis=1)
        xdt_b = xdt.astype(jnp.bfloat16)
        zero = jnp.zeros_like(xdt_b)
        rhs = jnp.concatenate([jnp.where(lane_lo, xdt_b, zero),
                               jnp.where(lane_lo, zero, xdt_b)], axis=0)
        y_diag = jnp.dot(w, rhs, preferred_element_type=jnp.float32)
        s_old = s_ref[:, lanes]
        y_off = jnp.dot(cm, s_old.astype(jnp.bfloat16),
                        preferred_element_type=jnp.float32) * jnp.exp(cum_pair)
        y_ref[:, lanes] = (y_diag + y_off).astype(jnp.bfloat16)
        decayed = (xdt * jnp.exp(tot_pair - cum_pair)).astype(jnp.bfloat16)
        s_ref[:, lanes] = jnp.exp(tot_pair) * s_old + jnp.dot(
            bm_t, decayed, preferred_element_type=jnp.float32)


def _ssd_kernel(*refs):
    n_in = 7
    fwd_refs, bwd_refs = refs[:n_in], refs[n_in:2 * n_in]
    yf_ref, yb_ref, sf_ref, sb_ref = refs[2 * n_in:]

    @pl.when(pl.program_id(2) == 0)
    def _():
        sf_ref[...] = jnp.zeros_like(sf_ref)
        sb_ref[...] = jnp.zeros_like(sb_ref)

    for gi in range(SSM_GROUPS):
        _ssd_direction(fwd_refs, sf_ref, yf_ref, gi, lane0=HEADS_PER_GROUP * gi, forward=True)
        _ssd_direction(bwd_refs, sb_ref, yb_ref, gi,
                       lane0=DT_BWD_LANE + HEADS_PER_GROUP * gi, forward=False)


def _ssd(xbc, dt_all, cum_all, cumt_all, bt_all, n_batch, seq_len, ctx_len):
    nt = xbc.shape[0]
    q = CHUNK
    nc_lat = seq_len // q
    nc_ctx = ctx_len // q
    ctx_blk0 = (n_batch * seq_len) // q
    n_steps = nc_ctx + nc_lat
    gw = D_INNER
    sw = SSM_GROUPS * SSM_STATE
    b_blk0 = D_INNER // sw
    c_blk0 = b_blk0 + 1

    def fwd_row(b, j):
        return jnp.where(j < nc_ctx, ctx_blk0 + nc_ctx * b + j, nc_lat * b + (j - nc_ctx))

    def bwd_row(b, j):
        return jnp.where(j < nc_ctx, ctx_blk0 + nc_ctx * b + (nc_ctx - 1 - j),
                         nc_lat * b + (n_steps - 1 - j))

    def specs(row):
        return [
            pl.BlockSpec((q, gw), lambda b, g, j: (row(b, j), g)),
            pl.BlockSpec((q, sw), lambda b, g, j: (row(b, j), b_blk0 + g)),
            pl.BlockSpec((q, sw), lambda b, g, j: (row(b, j), c_blk0 + g)),
            pl.BlockSpec((q, LANES), lambda b, g, j: (row(b, j), 0)),
            pl.BlockSpec((q, LANES), lambda b, g, j: (row(b, j), 0)),
            pl.BlockSpec((None, CUMT_ROWS, q), lambda b, g, j: (row(b, j), 0, 0)),
            pl.BlockSpec((None, sw, q), lambda b, g, j: (row(b, j), 0, 0)),
        ]

    out_sds = jax.ShapeDtypeStruct((nt, D_INNER), jnp.bfloat16)
    operands = (xbc, xbc, xbc, dt_all, cum_all, cumt_all, bt_all)
    return pl.pallas_call(
        _ssd_kernel,
        out_shape=(out_sds, out_sds),
        grid=(n_batch, 1, n_steps),
        in_specs=specs(fwd_row) + specs(bwd_row),
        out_specs=(pl.BlockSpec((q, gw), lambda b, g, j: (fwd_row(b, j), g)),
                   pl.BlockSpec((q, gw), lambda b, g, j: (bwd_row(b, j), g))),
        scratch_shapes=[pltpu.VMEM((SSM_STATE, gw), jnp.float32),
                        pltpu.VMEM((SSM_STATE, gw), jnp.float32)],
        compiler_params=_cparams(3),
        name="ssd_scan",
    )(*operands, *operands)


def _dft_tables(seq_len):
    l1n = seq_len // LANES
    two_pi = 2.0 * np.pi
    gd = FOURIER_GROUP_DIM
    jj = jnp.arange(gd, dtype=jnp.int32)
    ang_c = ((jj[:, None] * jj[None, :]) % gd).astype(jnp.float32) * (two_pi / gd)
    cc, sc = jnp.cos(ang_c), jnp.sin(ang_c)
    k1 = jnp.arange(l1n, dtype=jnp.int32)
    ang1 = ((k1[:, None] * k1[None, :]) % l1n).astype(jnp.float32) * (two_pi / l1n)
    w1 = jnp.concatenate([jnp.cos(ang1), -jnp.sin(ang1)], axis=0)
    k2 = jnp.arange(LANES, dtype=jnp.int32)
    kk = k1[:, None, None] + l1n * k2[None, :, None]
    ang2 = ((kk * k2[None, None, :]) % seq_len).astype(jnp.float32) * (two_pi / seq_len)
    er, ei = jnp.cos(ang2), -jnp.sin(ang2)
    e = jnp.concatenate([jnp.concatenate([er, -ei], axis=2),
                         jnp.concatenate([ei, er], axis=2)], axis=1)
    return cc, sc, w1, e


def _dft1_kernel(w_ref, x_ref, o_ref):
    o_ref[...] = jnp.dot(w_ref[...], x_ref[...],
                         preferred_element_type=jnp.float32).astype(jnp.bfloat16)


def _channel_mix(gr, gi, cs_ref, o_ref):
    gd = FOURIER_GROUP_DIM
    for g in range(FOURIER_GROUPS):
        cols = slice(gd * g, gd * (g + 1))
        lhs = jnp.concatenate([gr[:, cols], gi[:, cols]], axis=1).astype(jnp.bfloat16)
        o_ref[:, cols] = jnp.dot(lhs, cs_ref[...],
                                 preferred_element_type=jnp.float32).astype(jnp.bfloat16)


def _dft2_kernel(e_ref, ar_ref, ai_ref, cs_ref, o_ref):
    for k in range(e_ref.shape[0]):
        a = jnp.concatenate([ar_ref[k], ai_ref[k]], axis=0)
        g = jnp.dot(e_ref[k], a, preferred_element_type=jnp.float32)
        half = g.shape[0] // 2
        _channel_mix(g[:half], g[half:], cs_ref, o_ref.at[k])


def _dft_ctx_kernel(w_ref, x_ref, cs_ref, o_ref):
    g = jnp.dot(w_ref[...], x_ref[...], preferred_element_type=jnp.float32)
    half = g.shape[0] // 2
    _channel_mix(g[:half], g[half:], cs_ref, o_ref)


def _fourier_latent(four_all, n_batch, seq_len, tables):
    cc, sc, w1, e = tables
    c = FOURIER_W
    l1n = seq_len // LANES
    ncol = LANES * c
    four = four_all.reshape(n_batch, l1n, ncol)
    tn = 4096
    a = pl.pallas_call(
        _dft1_kernel,
        out_shape=jax.ShapeDtypeStruct((n_batch, 2 * l1n, ncol), jnp.bfloat16),
        grid=(n_batch, ncol // tn),
        in_specs=[pl.BlockSpec((2 * l1n, l1n), lambda b, j: (0, 0)),
                  pl.BlockSpec((None, l1n, tn), lambda b, j: (b, 0, j))],
        out_specs=pl.BlockSpec((None, 2 * l1n, tn), lambda b, j: (b, 0, j)),
        compiler_params=_cparams(2),
        name="dft_stage1",
    )(w1.astype(jnp.bfloat16), four)
    a4 = a.reshape(n_batch, 2 * l1n, LANES, c)
    norm = 1.0 / np.sqrt(float(seq_len) * FOURIER_GROUP_DIM)
    cs = (jnp.concatenate([cc, sc], axis=0) * norm).astype(jnp.bfloat16)
    kp = DFT2_ROWS_PER_STEP
    o = pl.pallas_call(
        _dft2_kernel,
        out_shape=jax.ShapeDtypeStruct((n_batch, l1n, LANES, c), jnp.bfloat16),
        grid=(n_batch, l1n // kp),
        in_specs=[pl.BlockSpec((kp, 2 * LANES, 2 * LANES), lambda b, k: (k, 0, 0)),
                  pl.BlockSpec((None, kp, LANES, c), lambda b, k: (b, k, 0, 0)),
                  pl.BlockSpec((None, kp, LANES, c), lambda b, k: (b, l1n // kp + k, 0, 0)),
                  pl.BlockSpec((2 * FOURIER_GROUP_DIM, FOURIER_GROUP_DIM), lambda b, k: (0, 0))],
        out_specs=pl.BlockSpec((None, kp, LANES, c), lambda b, k: (b, k, 0, 0)),
        compiler_params=_cparams(2),
        name="dft_stage2",
    )(e.astype(jnp.bfloat16), a4, a4, cs)
    return o.transpose(0, 2, 1, 3).reshape(n_batch * seq_len, c)


def _fourier_ctx(p_all, n_batch, seq_len, ctx_len, tables):
    cc, sc, _, _ = tables
    assert ctx_len == FOURIER_GROUP_DIM
    c = FOURIER_W
    wc = jnp.concatenate([cc, -sc], axis=0).astype(jnp.bfloat16)
    norm = 1.0 / np.sqrt(float(ctx_len) * FOURIER_GROUP_DIM)
    cs = (jnp.concatenate([cc, sc], axis=0) * norm).astype(jnp.bfloat16)
    blk0 = (n_batch * seq_len) // ctx_len
    return pl.pallas_call(
        _dft_ctx_kernel,
        out_shape=jax.ShapeDtypeStruct((n_batch * ctx_len, c), jnp.bfloat16),
        grid=(n_batch,),
        in_specs=[pl.BlockSpec((2 * ctx_len, ctx_len), lambda b: (0, 0)),
                  pl.BlockSpec((ctx_len, c), lambda b: (blk0 + b, P_FOUR // c)),
                  pl.BlockSpec((2 * FOURIER_GROUP_DIM, FOURIER_GROUP_DIM), lambda b: (0, 0))],
        out_specs=pl.BlockSpec((ctx_len, c), lambda b: (b, 0)),
        compiler_params=_cparams(1),
        name="dft_ctx",
    )(wc, p_all, cs)


def _merge_kernel(gate_ref, z_ref, xs_ref, yf_ref, yb_ref, flat_ref, fctx_ref, x_ref,
                  g1_ref, sc2_ref, sh2_ref, d_ref, nw_ref, lng_ref, lnb_ref,
                  wssm_ref, wfour_ref, wout_ref, xo_ref, u2_ref, u2b_ref,
                  *, tiles_per_seq, n_batch, alpha):
    bidx = jnp.minimum(pl.program_id(0) // tiles_per_seq, n_batch)
    f_tile = jnp.where(bidx >= n_batch, fctx_ref[...], flat_ref[...])
    y = (yf_ref[...].astype(jnp.float32) + yb_ref[...].astype(jnp.float32)
         + xs_ref[...].astype(jnp.float32) * d_ref[...])
    h = y * _silu(z_ref[...].astype(jnp.float32))
    h = h * lax.rsqrt(jnp.mean(h * h, axis=-1, keepdims=True) + LN_EPS) * nw_ref[...]
    ssm = jnp.dot(h.astype(jnp.bfloat16), wssm_ref[...], preferred_element_type=jnp.float32)
    four = jnp.dot(f_tile, wfour_ref[...], preferred_element_type=jnp.float32)
    gates = _sigmoid(gate_ref[...].astype(jnp.float32))
    merged = gates[:, :D_MODEL] * ssm + gates[:, D_MODEL:] * four
    out = jnp.dot(merged.astype(jnp.bfloat16), wout_ref[...], preferred_element_type=jnp.float32)
    r = alpha * x_ref[...] + g1_ref[pl.ds(bidx, 1), :] * out
    xn = _layer_norm_f32(r) * lng_ref[...] + lnb_ref[...]
    xo_ref[...] = xn
    u2 = (_layer_norm_f32(xn) * (1.0 + sc2_ref[pl.ds(bidx, 1), :])
          + sh2_ref[pl.ds(bidx, 1), :])
    u2_ref[...] = u2
    u2b_ref[...] = u2.astype(jnp.bfloat16)


def _merge(p_all, xbc, yf, yb, f_lat, f_ctx, x_all, mod, d_cols, norm_w, ln_g, ln_b,
           w_ssm, w_four, w_out, n_rows, seq_len, n_batch, alpha):
    tm = ROW_TILE
    d = D_MODEL
    row = lambda i: (i, 0)
    const = lambda i: (0, 0)
    n_lat_tiles = f_lat.shape[0] // tm
    out_sds = jax.ShapeDtypeStruct((n_rows, d), jnp.float32)
    return pl.pallas_call(
        functools.partial(_merge_kernel, tiles_per_seq=seq_len // tm, n_batch=n_batch, alpha=alpha),
        out_shape=(out_sds, out_sds, jax.ShapeDtypeStruct((n_rows, d), jnp.bfloat16)),
        grid=(n_rows // tm,),
        in_specs=[
            pl.BlockSpec((tm, 2 * d), lambda i: (i, P_GATE // (2 * d))),
            pl.BlockSpec((tm, D_INNER), lambda i: (i, P_Z // D_INNER)),
            pl.BlockSpec((tm, D_INNER), row),
            pl.BlockSpec((tm, D_INNER), row),
            pl.BlockSpec((tm, D_INNER), row),
            pl.BlockSpec((tm, FOURIER_W), lambda i: (jnp.minimum(i, n_lat_tiles - 1), 0)),
            pl.BlockSpec((tm, FOURIER_W), lambda i: (jnp.maximum(i - n_lat_tiles, 0), 0)),
            pl.BlockSpec((tm, d), row),
            pl.BlockSpec((8, d), lambda i: (0, 2)),
            pl.BlockSpec((8, d), lambda i: (0, 4)),
            pl.BlockSpec((8, d), lambda i: (0, 3)),
            pl.BlockSpec((1, D_INNER), const),
            pl.BlockSpec((1, D_INNER), const),
            pl.BlockSpec((1, d), const),
            pl.BlockSpec((1, d), const),
            pl.BlockSpec((D_INNER, d), const),
            pl.BlockSpec((FOURIER_W, d), const),
            pl.BlockSpec((d, d), const),
        ],
        out_specs=(pl.BlockSpec((tm, d), row), pl.BlockSpec((tm, d), row),
                   pl.BlockSpec((tm, d), row)),
        compiler_params=_cparams(1),
        name="merge_postnorm",
    )(p_all, p_all, xbc, yf, yb, f_lat, f_ctx, x_all, mod, mod, mod, d_cols, norm_w,
      ln_g, ln_b, w_ssm, w_four, w_out)


def _first_index_of_max(v, iota, big):
    m = jnp.max(v, axis=0, keepdims=True)
    idx = jnp.min(jnp.where(v == m, iota, big), axis=0, keepdims=True)
    return m, idx


def _router_kernel(u_ref, wt_ref, bias_ref, tri_ref, row_ref, rowt_ref, gatet_ref, cnt_ref,
                   off_ref, basetab_ref, tot_ref, base_ref):
    @pl.when(pl.program_id(0) == 0)
    def _():
        base_ref[...] = jnp.zeros_like(base_ref)

    tm = u_ref.shape[0]
    ne, epg = N_EXPERTS, EXPERTS_PER_GROUP
    neg = -jnp.inf
    logits = lax.dot_general(wt_ref[...], u_ref[...], (((1,), (1,)), ((), ())),
                             preferred_element_type=jnp.float32,
                             precision=lax.Precision.HIGHEST)
    scores = _sigmoid(logits)
    sel = scores + bias_ref[...]
    iota_g = lax.broadcasted_iota(jnp.int32, (epg, tm), 0).astype(jnp.float32)
    grp_rows = []
    for g in range(N_EXPERT_GROUPS):
        v = sel[epg * g:epg * (g + 1), :]
        m1, i1 = _first_index_of_max(v, iota_g, epg)
        m2 = jnp.max(jnp.where(iota_g == i1, neg, v), axis=0, keepdims=True)
        grp_rows.append(m1 + m2)
    grp = jnp.concatenate(grp_rows, axis=0)
    iota_n = lax.broadcasted_iota(jnp.int32, (N_EXPERT_GROUPS, tm), 0).astype(jnp.float32)
    chosen = jnp.zeros((N_EXPERT_GROUPS, tm), jnp.float32)
    for _ in range(TOPK_GROUPS):
        _, gi = _first_index_of_max(grp, iota_n, N_EXPERT_GROUPS)
        hit = iota_n == gi
        chosen = jnp.where(hit, 1.0, chosen)
        grp = jnp.where(hit, neg, grp)
    masked = jnp.concatenate(
        [jnp.where(chosen[g:g + 1, :] > 0.0, sel[epg * g:epg * (g + 1), :], neg)
         for g in range(N_EXPERT_GROUPS)], axis=0)
    iota_e = lax.broadcasted_iota(jnp.int32, (ne, tm), 0).astype(jnp.float32)
    picked = jnp.zeros((ne, tm), jnp.float32)
    for _ in range(TOP_K):
        _, ei = _first_index_of_max(masked, iota_e, ne)
        hit = iota_e == ei
        picked = jnp.where(hit, 1.0, picked)
        masked = jnp.where(hit, neg, masked)
    chosen_scores = picked * scores
    gate = chosen_scores / jnp.sum(chosen_scores, axis=0, keepdims=True) * ROUTE_SCALE
    before = jnp.dot(picked.astype(jnp.bfloat16), tri_ref[...],
                     preferred_element_type=jnp.float32)
    cnt = jnp.sum(picked, axis=1, keepdims=True)
    cnt16 = jnp.ceil(cnt * (1.0 / ROW_ALIGN)) * ROW_ALIGN
    cnt16_b = jnp.broadcast_to(cnt16, (ne, LANES))
    e_row = lax.broadcasted_iota(jnp.int32, (ne, ne), 0)
    e_col = lax.broadcasted_iota(jnp.int32, (ne, ne), 1)
    off_b = jnp.dot((e_col < e_row).astype(jnp.float32), cnt16_b,
                    preferred_element_type=jnp.float32,
                    precision=lax.Precision.HIGHEST)
    row1 = picked * (before + off_b[:, 0:1] + 1.0)
    pad = jnp.zeros((EXPERT_PAD - ne, tm), jnp.float32)
    row1_p = jnp.concatenate([row1, pad], axis=0)
    gate_p = jnp.concatenate([gate, pad], axis=0)
    pad_b = jnp.zeros((EXPERT_PAD - ne, LANES), jnp.float32)
    row_ref[...] = row1_p
    rowt_ref[...] = row1_p.T
    gatet_ref[...] = gate_p.T.astype(jnp.bfloat16)
    cnt_ref[...] = jnp.concatenate([cnt16_b, pad_b], axis=0)
    off_ref[...] = jnp.concatenate([off_b, pad_b], axis=0)
    basetab_ref[...] = jnp.broadcast_to(base_ref[...], (ne, LANES))
    base_ref[...] = base_ref[...] + cnt16
    tot_ref[...] = base_ref[...]


def _router(u2, router_w, router_bias, n_tok):
    tm = MOE_TILE
    n_tiles = n_tok // tm
    tri = jnp.triu(jnp.ones((tm, tm), jnp.float32), k=1).astype(jnp.bfloat16)
    tab = lambda rows: (jax.ShapeDtypeStruct((n_tiles, rows, LANES), jnp.float32),
                        pl.BlockSpec((None, rows, LANES), lambda i: (i, 0, 0)))
    outs = [
        (jax.ShapeDtypeStruct((n_tiles, EXPERT_PAD, tm), jnp.float32),
         pl.BlockSpec((None, EXPERT_PAD, tm), lambda i: (i, 0, 0))),
        (jax.ShapeDtypeStruct((n_tok, EXPERT_PAD), jnp.float32),
         pl.BlockSpec((tm, EXPERT_PAD), lambda i: (i, 0))),
        (jax.ShapeDtypeStruct((n_tok, EXPERT_PAD), jnp.bfloat16),
         pl.BlockSpec((tm, EXPERT_PAD), lambda i: (i, 0))),
        tab(EXPERT_PAD), tab(EXPERT_PAD), tab(N_EXPERTS),
        (jax.ShapeDtypeStruct((N_EXPERTS, 1), jnp.float32),
         pl.BlockSpec((N_EXPERTS, 1), lambda i: (0, 0))),
    ]
    return pl.pallas_call(
        _router_kernel,
        out_shape=tuple(o[0] for o in outs),
        grid=(n_tiles,),
        in_specs=[pl.BlockSpec((tm, D_MODEL), lambda i: (i, 0)),
                  pl.BlockSpec((N_EXPERTS, D_MODEL), lambda i: (0, 0)),
                  pl.BlockSpec((N_EXPERTS, 1), lambda i: (0, 0)),
                  pl.BlockSpec((tm, tm), lambda i: (0, 0))],
        out_specs=tuple(o[1] for o in outs),
        scratch_shapes=[pltpu.VMEM((N_EXPERTS, 1), jnp.float32)],
        compiler_params=_cparams(1),
        name="moe_router",
    )(u2, router_w.T, router_bias.reshape(N_EXPERTS, 1), tri)


def _run_copies(cnt, make_copy, pieces):
    done = jnp.int32(0)
    for piece in pieces:
        hit = (cnt & piece) != 0

        @pl.when(hit)
        def _(done=done, piece=piece):
            make_copy(done, piece)

        done = done + (cnt & piece)


def _piece_lists(cnt, off, dst):
    counts, srcs, dsts = [], [], []
    slot_ids = jnp.arange(N_EXPERTS, dtype=jnp.int32)
    for piece in RUN_PIECES:
        has = ((cnt & piece) != 0).astype(jnp.int32)
        above = cnt & ~(2 * piece - 1)
        rank = jnp.cumsum(has, axis=-1) - has
        place = (rank[:, None, :] == slot_ids[None, :, None]).astype(jnp.int32) * has[:, None, :]
        counts.append(jnp.sum(has, axis=-1))
        srcs.append(jnp.sum(place * (off + above)[:, None, :], axis=-1))
        dsts.append(jnp.sum(place * (dst + above)[:, None, :], axis=-1))
    flat = lambda parts: jnp.stack(parts, axis=1).reshape(-1).astype(jnp.int32)
    return flat(counts), flat(srcs), flat(dsts)


def _for_each_piece(tile, pn_ref, psrc_ref, pdst_ref, make_copy, start):
    for p_idx, piece in enumerate(RUN_PIECES):
        seg = tile * len(RUN_PIECES) + p_idx
        base = seg * N_EXPERTS

        def body(k, carry, piece=piece, base=base):
            cp = make_copy(pl.multiple_of(psrc_ref[base + k], ROW_ALIGN),
                           pl.multiple_of(pdst_ref[base + k], ROW_ALIGN), piece)
            if start:
                cp.start()
            else:
                cp.wait()
            return carry

        lax.fori_loop(0, pn_ref[seg], body, 0)


def _split_rows(row1):
    hi = jnp.floor(row1 * (1.0 / ROW_SPLIT))
    lo = row1 - hi * ROW_SPLIT
    return hi.astype(jnp.bfloat16), lo.astype(jnp.bfloat16)


def _dispatch_kernel(pn_ref, psrc_ref, pdst_ref, tail_ref, nblk_ref, used_ref, u_ref, row_ref,
                     bounds_ref, xs_hbm, sorted_ref, zero_ref, sems):
    i = pl.program_id(0)
    last = pl.num_programs(0) - 1
    slot = i % 2
    tm = u_ref.shape[0]
    n_chunks = sorted_ref.shape[1] // tm
    digits = jnp.concatenate(_split_rows(row_ref[...]), axis=0)
    lower = bounds_ref[0:1, :]
    upper = bounds_ref[1:2, :]
    row_e = lax.broadcasted_iota(jnp.int32, (tm, EXPERT_PAD), 0).astype(jnp.float32)
    row_t = lax.broadcasted_iota(jnp.int32, (tm, tm), 0).astype(jnp.float32)

    def sort_chunk(j):
        r0 = float(j * tm)
        in_run = jnp.logical_and(row_e + r0 >= lower, row_e + r0 < upper)
        owner = jnp.concatenate([jnp.where(in_run, ROW_SPLIT, 0.0),
                                 jnp.where(in_run, 1.0, 0.0)], axis=1).astype(jnp.bfloat16)
        want = jnp.dot(owner, digits, preferred_element_type=jnp.float32)
        perm = jnp.where(want == row_t + (r0 + 1.0), 1.0, 0.0).astype(jnp.bfloat16)
        sorted_ref[slot, j * tm:(j + 1) * tm, :] = jnp.dot(
            perm, u_ref[...], preferred_element_type=jnp.float32).astype(jnp.bfloat16)

    for j in range(n_chunks):
        if (j + 1) * tm <= TOP_K * tm:
            sort_chunk(j)
        else:
            pl.when(used_ref[i] > j * tm)(functools.partial(sort_chunk, j))

    def run_copy(buf):
        def make(src, dst, piece):
            return pltpu.make_async_copy(sorted_ref.at[buf, pl.ds(src, piece), :],
                                         xs_hbm.at[pl.ds(dst, piece), :], sems.at[buf])
        return make

    _for_each_piece(i, pn_ref, psrc_ref, pdst_ref, run_copy(slot), start=True)

    @pl.when(i > 0)
    def _():
        _for_each_piece(i - 1, pn_ref, psrc_ref, pdst_ref, run_copy(1 - slot), start=False)

    @pl.when(i == last)
    def _():
        _for_each_piece(i, pn_ref, psrc_ref, pdst_ref, run_copy(slot), start=False)
        zero_ref[...] = jnp.zeros_like(zero_ref)
        blk = zero_ref.shape[0]
        n_blk_total = xs_hbm.shape[0] // blk

        def zero_copy(dst, piece):
            return pltpu.make_async_copy(zero_ref.at[pl.ds(0, piece), :],
                                         xs_hbm.at[pl.ds(dst, piece), :], sems.at[0])

        def tail_body(start):
            def body(e, carry):
                dst = pl.multiple_of(tail_ref[e], ROW_ALIGN)

                def piece_copy(done, piece):
                    cp = zero_copy(pl.multiple_of(dst + done, ROW_ALIGN), piece)
                    if start:
                        cp.start()
                    else:
                        cp.wait()

                _run_copies(tail_ref[N_EXPERTS + e], piece_copy, TAIL_PIECES)
                return carry
            return body

        def blk_body(start):
            def body(b, carry):
                cp = zero_copy(pl.multiple_of(b * blk, blk), blk)
                if start:
                    cp.start()
                else:
                    cp.wait()
                return carry
            return body

        lax.fori_loop(0, N_EXPERTS, tail_body(True), 0)
        lax.fori_loop(nblk_ref[0], n_blk_total, blk_body(True), 0)
        lax.fori_loop(0, N_EXPERTS, tail_body(False), 0)
        lax.fori_loop(nblk_ref[0], n_blk_total, blk_body(False), 0)


def _dispatch(u2b, row1, bounds, pieces, tail, n_used_blk, used, n_blk):
    tm = MOE_TILE
    d = D_MODEL
    n_tok = u2b.shape[0]
    grid_spec = pltpu.PrefetchScalarGridSpec(
        num_scalar_prefetch=6,
        grid=(n_tok // tm,),
        in_specs=[pl.BlockSpec((tm, d), lambda i, *_: (i, 0)),
                  pl.BlockSpec((None, EXPERT_PAD, tm), lambda i, *_: (i, 0, 0)),
                  pl.BlockSpec((None, 2, EXPERT_PAD), lambda i, *_: (i, 0, 0))],
        out_specs=pl.BlockSpec(memory_space=pl.ANY),
        scratch_shapes=[pltpu.VMEM((2, SORTED_ROWS, d), jnp.bfloat16),
                        pltpu.VMEM((MOE_BLK, d), jnp.bfloat16),
                        pltpu.SemaphoreType.DMA((2,))],
    )
    return pl.pallas_call(
        _dispatch_kernel,
        out_shape=jax.ShapeDtypeStruct((n_blk * MOE_BLK, d), jnp.bfloat16),
        grid_spec=grid_spec,
        compiler_params=_cparams(1),
        name="moe_dispatch",
    )(*pieces, tail, n_used_blk, used, u2b, row1, bounds)


def _expert_kernel(blk_e_ref, blk_valid_ref, x_ref, w1_ref, w3_ref, w2_ref, y_ref,
                   w13_ref, w2b_ref):
    i = pl.program_id(0)

    @pl.when(blk_valid_ref[i] > 0)
    def _():
        @pl.when(jnp.logical_or(i == 0, blk_e_ref[i] != blk_e_ref[jnp.maximum(i - 1, 0)]))
        def _():
            w13_ref[:, :EXPERT_FF] = w1_ref[...].astype(jnp.bfloat16)
            w13_ref[:, EXPERT_FF:] = w3_ref[...].astype(jnp.bfloat16)
            w2b_ref[...] = w2_ref[...].astype(jnp.bfloat16)

        h = jnp.dot(x_ref[...], w13_ref[...], preferred_element_type=jnp.float32)
        hb = (_silu(h[:, :EXPERT_FF]) * h[:, EXPERT_FF:]).astype(jnp.bfloat16)
        y_ref[...] = jnp.dot(hb, w2b_ref[...],
                             preferred_element_type=jnp.float32).astype(jnp.bfloat16)

    @pl.when(blk_valid_ref[i] == 0)
    def _():
        y_ref[...] = jnp.zeros_like(y_ref)


def _experts(xs, blk_e, blk_valid, w1, w3, w2, layer, n_blk):
    blk = MOE_BLK
    d = D_MODEL
    grid_spec = pltpu.PrefetchScalarGridSpec(
        num_scalar_prefetch=2,
        grid=(n_blk,),
        in_specs=[
            pl.BlockSpec((blk, d), lambda i, be, bv: (i, 0)),
            pl.BlockSpec((None, None, d, EXPERT_FF), lambda i, be, bv: (layer, be[i], 0, 0)),
            pl.BlockSpec((None, None, d, EXPERT_FF), lambda i, be, bv: (layer, be[i], 0, 0)),
            pl.BlockSpec((None, None, EXPERT_FF, d), lambda i, be, bv: (layer, be[i], 0, 0)),
        ],
        out_specs=pl.BlockSpec((blk, d), lambda i, be, bv: (i, 0)),
        scratch_shapes=[pltpu.VMEM((d, 2 * EXPERT_FF), jnp.bfloat16),
                        pltpu.VMEM((EXPERT_FF, d), jnp.bfloat16)],
    )
    return pl.pallas_call(
        _expert_kernel,
        out_shape=jax.ShapeDtypeStruct((n_blk * blk, d), jnp.bfloat16),
        grid_spec=grid_spec,
        compiler_params=_cparams(1),
        name="moe_experts",
    )(blk_e, blk_valid, xs, w1, w3, w2)


def _final_kernel(pn_ref, psrc_ref, pdst_ref, used_ref, rowt_ref, gatet_ref, cnt_tab_ref,
                  off_tab_ref, y_hbm, x_ref, u_ref, g2_ref, lng_ref, lnb_ref, ws1_ref, ws3_ref,
                  ws2_ref, o_ref, sorted_ref, acc_ref, sems, *, tiles_per_seq, n_batch, alpha):
    i = pl.program_id(0)
    last = pl.num_programs(0) - 1
    slot = i % 2
    bidx = jnp.minimum(i // tiles_per_seq, n_batch)
    tm = x_ref.shape[0]
    n_chunks = sorted_ref.shape[1] // tm

    def run_copy(buf):
        def make(src, dst, piece):
            return pltpu.make_async_copy(y_hbm.at[pl.ds(dst, piece), :],
                                         sorted_ref.at[buf, pl.ds(src, piece), :], sems.at[buf])
        return make

    @pl.when(i == 0)
    def _():
        sorted_ref[...] = jnp.zeros_like(sorted_ref)
        _for_each_piece(i, pn_ref, psrc_ref, pdst_ref, run_copy(slot), start=True)

    @pl.when(i < last)
    def _():
        _for_each_piece(i + 1, pn_ref, psrc_ref, pdst_ref, run_copy(1 - slot), start=True)

    _for_each_piece(i, pn_ref, psrc_ref, pdst_ref, run_copy(slot), start=False)

    digits = jnp.concatenate(_split_rows(rowt_ref[...]), axis=1)
    gate = gatet_ref[...]
    lower = off_tab_ref[:, 0:1]
    upper = lower + cnt_tab_ref[:, 0:1]
    lane_e = lax.broadcasted_iota(jnp.int32, (EXPERT_PAD, tm), 1).astype(jnp.float32)
    lane_t = lax.broadcasted_iota(jnp.int32, (tm, tm), 1).astype(jnp.float32)

    def chunk_sum(j):
        r0 = float(j * tm)
        in_run = jnp.logical_and(lane_e + r0 >= lower, lane_e + r0 < upper)
        owner = jnp.where(in_run, 1.0, 0.0).astype(jnp.bfloat16)
        owner_digits = jnp.concatenate([jnp.where(in_run, ROW_SPLIT, 0.0).astype(jnp.bfloat16),
                                        owner], axis=0)
        want = jnp.dot(digits, owner_digits, preferred_element_type=jnp.float32)
        weight = jnp.dot(gate, owner, preferred_element_type=jnp.float32)
        comb = jnp.where(want == lane_t + (r0 + 1.0), weight, 0.0).astype(jnp.bfloat16)
        return jnp.dot(comb, sorted_ref[slot, j * tm:(j + 1) * tm, :],
                       preferred_element_type=jnp.float32)

    always = TOP_K
    routed = chunk_sum(0)
    for j in range(1, always):
        routed = routed + chunk_sum(j)
    acc_ref[...] = routed
    for j in range(always, n_chunks):
        @pl.when(used_ref[i] > j * tm)
        def _(j=j):
            acc_ref[...] += chunk_sum(j)
    routed = acc_ref[...]
    u = u_ref[...]
    h1 = jnp.dot(u, ws1_ref[...], preferred_element_type=jnp.float32)
    h3 = jnp.dot(u, ws3_ref[...], preferred_element_type=jnp.float32)
    shared = jnp.dot((_silu(h1) * h3).astype(jnp.bfloat16), ws2_ref[...],
                     preferred_element_type=jnp.float32)
    r = alpha * x_ref[...] + g2_ref[pl.ds(bidx, 1), :] * (routed + shared)
    o_ref[...] = _layer_norm_f32(r) * lng_ref[...] + lnb_ref[...]


def _final(pieces, used, row1_t, gate_t, cnt_tab, off_tab, y_slots, x_mid, u2b, mod,
           ln_g, ln_b, ws1, ws3, ws2, n_rows, seq_len, n_batch, alpha):
    tm = MOE_TILE
    d = D_MODEL
    row = lambda i, *_: (i, 0)
    const = lambda i, *_: (0, 0)
    tab_spec = pl.BlockSpec((None, EXPERT_PAD, LANES), lambda i, *_: (i, 0, 0))
    grid_spec = pltpu.PrefetchScalarGridSpec(
        num_scalar_prefetch=4,
        grid=(n_rows // tm,),
        in_specs=[
            pl.BlockSpec((tm, EXPERT_PAD), row),
            pl.BlockSpec((tm, EXPERT_PAD), row),
            tab_spec,
            tab_spec,
            pl.BlockSpec(memory_space=pl.ANY),
            pl.BlockSpec((tm, d), row),
            pl.BlockSpec((tm, d), row),
            pl.BlockSpec((8, d), lambda i, *_: (0, 5)),
            pl.BlockSpec((1, d), const),
            pl.BlockSpec((1, d), const),
            pl.BlockSpec((d, SHARED_FF), const),
            pl.BlockSpec((d, SHARED_FF), const),
            pl.BlockSpec((SHARED_FF, d), const),
        ],
        out_specs=pl.BlockSpec((tm, d), row),
        scratch_shapes=[pltpu.VMEM((2, SORTED_ROWS, d), jnp.bfloat16),
                        pltpu.VMEM((tm, d), jnp.float32),
                        pltpu.SemaphoreType.DMA((2,))],
    )
    return pl.pallas_call(
        functools.partial(_final_kernel, tiles_per_seq=seq_len // tm, n_batch=n_batch, alpha=alpha),
        out_shape=jax.ShapeDtypeStruct((n_rows, d), jnp.float32),
        grid_spec=grid_spec,
        compiler_params=_cparams(1),
        name="moe_combine_postnorm",
    )(*pieces, used, row1_t, gate_t, cnt_tab, off_tab, y_slots, x_mid, u2b, mod,
      ln_g, ln_b, ws1, ws3, ws2)


def _moe_sublayer(x_mid, u2, u2b, mod, ln_g, ln_b, router_w, router_bias, w1, w3, w2, layer,
                  ws1, ws3, ws2, n_rows, seq_len, n_batch, alpha):
    bf = jnp.bfloat16
    blk = MOE_BLK
    n_tiles = n_rows // MOE_TILE
    row1, row1_t, gate_t, cnt_tab, off_tab, base_tab, total = _router(
        u2, router_w, router_bias, n_rows)
    as_int = lambda t: t[:, :N_EXPERTS, 0].astype(jnp.int32)
    cnt, off, base = as_int(cnt_tab), as_int(off_tab), as_int(base_tab)
    used = off[:, -1] + cnt[:, -1]
    bounds = jnp.stack([off_tab[:, :, 0], off_tab[:, :, 0] + cnt_tab[:, :, 0]], axis=1)
    total = total.reshape(N_EXPERTS).astype(jnp.int32)
    padded = (total + blk - 1) // blk * blk
    pends = jnp.cumsum(padded)
    pstart = pends - padded
    dst = pstart[None, :] + base
    tail = jnp.concatenate([pstart + total, padded - total])
    n_used_blk = (pends[-1:] // blk).astype(jnp.int32)
    max_rows = n_rows * TOP_K + n_tiles * N_EXPERTS * (ROW_ALIGN - 1) + N_EXPERTS * (blk - 1)
    n_blk = -(-max_rows // blk)
    blk_start = jnp.arange(n_blk, dtype=jnp.int32) * blk
    blk_e = jnp.minimum(jnp.sum((blk_start[:, None] >= pends[None, :]).astype(jnp.int32), axis=1),
                        N_EXPERTS - 1)
    blk_valid = (blk_start < pends[-1]).astype(jnp.int32)
    pieces = _piece_lists(cnt, off, dst)
    xs = _dispatch(u2b, row1, bounds, pieces, tail, n_used_blk, used, n_blk)
    y_slots = _experts(xs, blk_e, blk_valid, w1, w3, w2, layer, n_blk)
    return _final(pieces, used, row1_t, gate_t, cnt_tab, off_tab, y_slots, x_mid, u2b, mod,
                  ln_g, ln_b, ws1.astype(bf), ws3.astype(bf), ws2.astype(bf),
                  n_rows, seq_len, n_batch, alpha)


def _pack_in_proj(w_in):
    d = w_in.shape[0]
    gap = jnp.zeros((d, DT_BWD_LANE - SSM_HEADS), w_in.dtype)
    rest = jnp.zeros((d, P_FOUR - P_DT - DT_BWD_LANE - SSM_HEADS), w_in.dtype)
    return jnp.concatenate([
        w_in[:, OFF_XBC:OFF_DT],
        w_in[:, OFF_DT:OFF_DT + SSM_HEADS], gap, w_in[:, OFF_DT + SSM_HEADS:OFF_FOUR], rest,
        w_in[:, OFF_FOUR:OFF_GATE], w_in[:, OFF_GATE:], w_in[:, :OFF_XBC]],
        axis=1).astype(jnp.bfloat16)


def _pack_head_rows(v):
    out = jnp.zeros((1, LANES), jnp.float32)
    v = v.reshape(2, SSM_HEADS).astype(jnp.float32)
    out = out.at[0, 0:SSM_HEADS].set(v[0])
    return out.at[0, DT_BWD_LANE:DT_BWD_LANE + SSM_HEADS].set(v[1])


def kernel(x, c, ctx, c_ctx, w_ada, b_ada, w_in, conv_w, conv_b, dt_bias, a_log, d_skip,
           ssm_norm_w, w_br_ssm, w_br_four, w_out, ln1_g, ln1_b, ln2_g, ln2_b,
           router_w, router_bias, w1, w3, w2, ws1, ws3, ws2):
    n_batch, seq_len, d = x.shape
    ctx_len = ctx.shape[1]
    depth = w_ada.shape[0]
    bf = jnp.bfloat16
    alpha = float((2 * depth) ** 0.25)
    n_lat = n_batch * seq_len
    assert d == D_MODEL and n_batch + 1 <= 8
    assert seq_len % K1_TM == 0 and (n_batch * ctx_len) % K1_TM == 0
    assert seq_len % (LANES * 8) == 0

    x_all = _assemble_stream(x, ctx)
    c_rows = jnp.zeros((8, d), jnp.float32).at[:n_batch].set(c).at[n_batch].set(c_ctx)
    mod_all = _ada_mod(c_rows, w_ada, b_ada)
    tables = _dft_tables(seq_len)

    for i in range(depth):
        last = i == depth - 1
        mod = mod_all[i]
        p_all, four_all = _in_proj(x_all, mod, _pack_in_proj(w_in[i]), seq_len, n_batch)
        a_row = _pack_head_rows(-jnp.exp(a_log[i].astype(jnp.float32)))
        xbc, dt_all, cum_all, cumt_all, bt_all = _conv(
            p_all, conv_w[i], conv_b[i], _pack_head_rows(dt_bias[i]), a_row,
            seq_len, n_lat, ctx_len)
        yf, yb = _ssd(xbc, dt_all, cum_all, cumt_all, bt_all, n_batch, seq_len, ctx_len)
        f_lat = _fourier_latent(four_all, n_batch, seq_len, tables)
        n_rows = n_lat if last else x_all.shape[0]
        f_ctx = f_lat if last else _fourier_ctx(p_all, n_batch, seq_len, ctx_len, tables)
        d_cols = jnp.repeat(d_skip[i].astype(jnp.float32), SSM_HEAD_DIM).reshape(1, D_INNER)
        x_mid, u2, u2b = _merge(p_all, xbc, yf, yb, f_lat, f_ctx, x_all, mod, d_cols,
                           ssm_norm_w[i].reshape(1, D_INNER), ln1_g[i].reshape(1, d),
                           ln1_b[i].reshape(1, d), w_br_ssm[i].astype(bf),
                           w_br_four[i].astype(bf), w_out[i].astype(bf),
                           n_rows, seq_len, n_batch, alpha)
        x_all = _moe_sublayer(x_mid, u2, u2b, mod, ln2_g[i].reshape(1, d), ln2_b[i].reshape(1, d),
                              router_w[i], router_bias[i], w1, w3, w2, i,
                              ws1[i], ws3[i], ws2[i], n_rows, seq_len, n_batch, alpha)
    return x_all[:n_lat].reshape(n_batch, seq_len, d)
```
